```python
import math
import jax
import jax.numpy as jnp
from jax import lax
import numpy as np

D_MODEL = 2048
BATCH = 2
SEQ = 4096
DEPTH = 2
DEC_BATCH = 32
DEC_SEQ = 4
PAST_LEN = 16384
PAGE_SIZE = 128

BRANCH_WIDTH = D_MODEL // 2
N_BRANCH = 3
WINDOW = 128
A_HEAD_DIM = 64
A_HEADS = BRANCH_WIDTH // A_HEAD_DIM
A_KV_HEADS = 4
A_GROUP = A_HEADS // A_KV_HEADS
DN_DK = 128
DN_DV = 128
DN_HEADS = BRANCH_WIDTH // DN_DV
DN_CHUNK = 64
DN_CONV_CH = DN_HEADS * (2 * DN_DK + DN_DV)
CONV_WIDTH = 4
LRU_WIDTH = BRANCH_WIDTH
LRU_BLOCKS = 8
LRU_BLOCK = LRU_WIDTH // LRU_BLOCKS
LRU_C = 8.0
N_EXPERTS = 32
TOP_K = 4
D_FF = D_MODEL
SWIGLU_ALPHA = 1.702
SWIGLU_LIMIT = 7.0
MOE_BLOCK = 256
EPS = 1e-6
IN_SIZES = (A_HEADS * A_HEAD_DIM, A_KV_HEADS * A_HEAD_DIM, A_KV_HEADS * A_HEAD_DIM,
            DN_CONV_CH, DN_HEADS, DN_HEADS, DN_HEADS * DN_DV, LRU_WIDTH, N_BRANCH * D_MODEL)
IN_WIDTH = sum(IN_SIZES)

kernel_name = 'hybrid_swa_deltanet_rglru_moe_adaln_step'


def rms_norm(x, g):
    xf = x.astype(jnp.float32)
    y = xf * lax.rsqrt(jnp.mean(xf * xf, axis=-1, keepdims=True) + EPS)
    return (y * g.astype(jnp.float32)).astype(x.dtype)


def l2_norm(x):
    return x * lax.rsqrt(jnp.sum(x * x, axis=-1, keepdims=True) + EPS)


def causal_conv(x, buf, w):
    t = x.shape[1]
    xp = jnp.concatenate([buf.astype(x.dtype), x], axis=1)
    y = xp[:, :t] * w[0]
    for j in range(1, CONV_WIDTH):
        y = y + xp[:, j:j + t] * w[j]
    return y, xp[:, t:]


def window_mask(q_pos, k_pos):
    d = q_pos - k_pos
    return (d >= 0) & (d < WINDOW) & (k_pos >= 0)


def sink_attend(q, k, v, sinks, mask):
    s = jnp.einsum('bnqkgd,bnskd->bnkgqs', q, k).astype(jnp.float32) * (A_HEAD_DIM ** -0.5)
    s = jnp.where(mask[None, :, None, None], s, -jnp.inf)
    sink = jnp.broadcast_to(sinks.astype(jnp.float32).reshape(1, 1, A_KV_HEADS, A_GROUP, 1, 1),
                            s.shape[:-1] + (1,))
    p = jax.nn.softmax(jnp.concatenate([s, sink], axis=-1), axis=-1)[..., :-1]
    return jnp.einsum('bnkgqs,bnskd->bnqkgd', p.astype(v.dtype), v)


def swa_mixer(q, k, v, sinks, kv_buf):
    b, t = q.shape[:2]
    if kv_buf is None:
        nb = t // WINDOW
        qb = q.reshape(b, nb, WINDOW, A_KV_HEADS, A_GROUP, A_HEAD_DIM)

        def band(z):
            zb = z.reshape(b, nb, WINDOW, A_KV_HEADS, A_HEAD_DIM)
            prev = jnp.concatenate([jnp.zeros_like(zb[:, :1]), zb[:, :-1]], axis=1)
            return jnp.concatenate([prev, zb], axis=2)

        blk = jnp.arange(nb)[:, None]
        q_pos = blk * WINDOW + jnp.arange(WINDOW)[None]
        k_pos = (blk - 1) * WINDOW + jnp.arange(2 * WINDOW)[None]
        mask = window_mask(q_pos[:, :, None], k_pos[:, None, :])
        o = sink_attend(qb, band(k), band(v), sinks, mask)
        nbuf = min(WINDOW, t)
        new_k, new_v = k[:, t - nbuf:], v[:, t - nbuf:]
    else:
        k_buf, v_buf = kv_buf
        nbuf = k_buf.shape[1]
        kc = jnp.concatenate([k_buf.astype(k.dtype), k], axis=1)
        vc = jnp.concatenate([v_buf.astype(v.dtype), v], axis=1)
        q_pos = nbuf + jnp.arange(t)
        k_pos = jnp.arange(nbuf + t)
        mask = window_mask(q_pos[:, None], k_pos[None])[None]
        o = sink_attend(q.reshape(b, 1, t, A_KV_HEADS, A_GROUP, A_HEAD_DIM),
                        kc[:, None], vc[:, None], sinks, mask)
        new_k, new_v = kc[:, t:], vc[:, t:]
    return o.reshape(b, t, A_HEADS * A_HEAD_DIM), new_k, new_v


def gated_delta_rule(q, k, v, g, beta, s0):
    b, t = q.shape[:2]
    c = min(DN_CHUNK, t)
    n = -(-t // c)
    pad = n * c - t

    def chunks(z):
        z = jnp.pad(z, [(0, 0), (0, pad)] + [(0, 0)] * (z.ndim - 2))
        z = z.reshape((b, n, c) + z.shape[2:])
        return jnp.transpose(z, (1, 0, 3, 2) + tuple(range(4, z.ndim)))

    qc, kc, vc, bc = chunks(q), chunks(k), chunks(v), chunks(beta)
    gc = jnp.cumsum(chunks(g), axis=-1)
    incl = jnp.tril(jnp.ones((c, c), bool))
    strict = jnp.tril(jnp.ones((c, c), bool), -1)
    decay = jnp.exp(jnp.where(incl, gc[..., :, None] - gc[..., None, :], -jnp.inf))
    kb = kc * bc[..., None]
    a_mat = jnp.where(strict, jnp.einsum('nbhid,nbhjd->nbhij', kb, kc) * decay, 0.0)
    rhs = jnp.concatenate([vc * bc[..., None], kb * jnp.exp(gc)[..., None]], axis=-1)
    sol = lax.linalg.triangular_solve(a_mat, rhs, left_side=True, lower=True, unit_diagonal=True)
    u_in, w_in = sol[..., :DN_DV], sol[..., DN_DV:]
    qk = jnp.einsum('nbhid,nbhjd->nbhij', qc, kc) * decay
    q_dec = qc * jnp.exp(gc)[..., None]
    k_dec = kc * jnp.exp(gc[..., -1:] - gc)[..., None]
    g_last = jnp.exp(gc[..., -1])

    def step(s, xs):
        u_in_n, w_n, qk_n, qd_n, kd_n, gl_n = xs
        u = u_in_n - jnp.einsum('bhcd,bhde->bhce', w_n, s)
        o = jnp.einsum('bhcd,bhde->bhce', qd_n, s) + jnp.einsum('bhij,bhje->bhie', qk_n, u)
        s = s * gl_n[..., None, None] + jnp.einsum('bhcd,bhce->bhde', kd_n, u)
        return s, o

    s_fin, o = lax.scan(step, s0, (u_in, w_in, qk, q_dec, k_dec, g_last))
    o = jnp.transpose(o, (1, 0, 3, 2, 4)).reshape(b, n * c, DN_HEADS, DN_DV)[:, :t]
    return o, s_fin


def delta_mixer(qkv_raw, a_raw, b_raw, z, conv_w, a_log, dt_bias, norm_g, conv_buf, s0):
    b, t = qkv_raw.shape[:2]
    qkv, new_buf = causal_conv(qkv_raw, conv_buf, conv_w)
    qkv = jax.nn.silu(qkv.astype(jnp.float32))
    q, k, v = jnp.split(qkv, [DN_HEADS * DN_DK, 2 * DN_HEADS * DN_DK], axis=-1)
    q = l2_norm(q.reshape(b, t, DN_HEADS, DN_DK)) * (DN_DK ** -0.5)
    k = l2_norm(k.reshape(b, t, DN_HEADS, DN_DK))
    v = v.reshape(b, t, DN_HEADS, DN_DV)
    beta = jax.nn.sigmoid(b_raw.astype(jnp.float32))
    g = -jnp.exp(a_log.astype(jnp.float32)) * jax.nn.softplus(a_raw.astype(jnp.float32) + dt_bias.astype(jnp.float32))
    o, s_new = gated_delta_rule(q, k, v, g, beta, s0.astype(jnp.float32))
    o = rms_norm(o, norm_g) * jax.nn.silu(z.reshape(b, t, DN_HEADS, DN_DV).astype(jnp.float32))
    return o.reshape(b, t, DN_HEADS * DN_DV).astype(z.dtype), new_buf, s_new.astype(s0.dtype)


def lru_combine(left, right):
    a1, b1 = left
    a2, b2 = right
    return a1 * a2, a2 * b1 + b2


def rglru_mixer(x_raw, conv_w, conv_b, w_r, b_r, w_i, b_i, lam, conv_buf, h0):
    b, t = x_raw.shape[:2]
    xc, new_buf = causal_conv(x_raw, conv_buf, conv_w)
    xc = xc + conv_b
    xb = xc.reshape(b, t, LRU_BLOCKS, LRU_BLOCK)
    r = jax.nn.sigmoid((jnp.einsum('btnc,ncd->btnd', xb, w_r) + b_r).astype(jnp.float32)).reshape(b, t, LRU_WIDTH)
    i = jax.nn.sigmoid((jnp.einsum('btnc,ncd->btnd', xb, w_i) + b_i).astype(jnp.float32)).reshape(b, t, LRU_WIDTH)
    log_a = -LRU_C * r * jax.nn.softplus(-lam.astype(jnp.float32))
    a = jnp.exp(log_a)
    u = jnp.sqrt(-jnp.expm1(2.0 * log_a)) * (i * xc.astype(jnp.float32))
    u = u.at[:, 0].add(a[:, 0] * h0.astype(jnp.float32))
    _, h = lax.associative_scan(lru_combine, (a, u), axis=1)
    return h.astype(x_raw.dtype), new_buf, h[:, -1].astype(h0.dtype)


def clamped_swiglu(gu):
    glu, lin = gu[..., :D_FF], gu[..., D_FF:]
    glu = jnp.minimum(glu, SWIGLU_LIMIT)
    lin = jnp.clip(lin, -SWIGLU_LIMIT, SWIGLU_LIMIT)
    return glu * jax.nn.sigmoid(SWIGLU_ALPHA * glu) * (lin + 1.0)


def moe_ffn(h, w_router, b_router, w_gu, b_gu, w_down, b_down):
    n = h.shape[0]
    logits = (h @ w_router + b_router).astype(jnp.float32)
    top_logit, top_e = lax.top_k(logits, TOP_K)
    top_w = jax.nn.softmax(top_logit, axis=-1)
    n_assign = n * TOP_K
    blk = max(1, min(MOE_BLOCK, -(-n_assign // N_EXPERTS)))
    n_blocks = -(-n_assign // blk) + N_EXPERTS
    rows = n_blocks * blk
    flat_e = top_e.reshape(-1)
    order = jnp.argsort(flat_e)
    sorted_e = flat_e[order]
    counts = jnp.bincount(flat_e, length=N_EXPERTS)
    padded = (counts + blk - 1) // blk * blk
    pad_end = jnp.cumsum(padded)
    start = jnp.cumsum(counts) - counts
    dest = pad_end[sorted_e] - padded[sorted_e] + jnp.arange(n_assign) - start[sorted_e]
    row_tok = jnp.full((rows,), n, jnp.int32).at[dest].set((order // TOP_K).astype(jnp.int32))
    row_w = jnp.zeros((rows,), jnp.float32).at[dest].set(top_w.reshape(-1)[order])
    block_e = jnp.minimum(jnp.searchsorted(pad_end, jnp.arange(n_blocks) * blk, side='right'), N_EXPERTS - 1)

    def expert_block(args):
        tok, e = args
        xb = h.at[tok].get(mode='fill', fill_value=0)
        act = clamped_swiglu(xb @ w_gu[e] + b_gu[e])
        return act @ w_down[e] + b_down[e]

    out_rows = lax.map(expert_block, (row_tok.reshape(n_blocks, blk), block_e))
    out = jnp.zeros((n, D_MODEL), jnp.float32).at[row_tok].add(
        out_rows.reshape(rows, D_MODEL).astype(jnp.float32) * row_w[:, None], mode='drop')
    return out.astype(h.dtype)


def decoder_layer(x, c, p, st):
    b, t, _ = x.shape
    mod = (jax.nn.silu(c) @ p['w_ada'] + p['b_ada']).reshape(b, 6, 1, D_MODEL)
    sh1, sc1, gt1, sh2, sc2, gt2 = (mod[:, j] for j in range(6))
    if st is None:
        kv_buf = None
        s0 = jnp.zeros((b, DN_HEADS, DN_DK, DN_DV), jnp.float32)
        dconv0 = jnp.zeros((b, CONV_WIDTH - 1, DN_CONV_CH), x.dtype)
        h0 = jnp.zeros((b, LRU_WIDTH), jnp.float32)
        lconv0 = jnp.zeros((b, CONV_WIDTH - 1, LRU_WIDTH), x.dtype)
    else:
        k_buf, v_buf, s0, dconv0, h0, lconv0 = st
        kv_buf = (k_buf, v_buf)

    h = rms_norm(x, p['norm1_g']) * (1.0 + sc1) + sh1
    proj = h @ p['w_in']
    aq, ak, av, dqkv, da, db, dz, lx, gates = jnp.split(proj, list(np.cumsum(IN_SIZES)[:-1]), axis=-1)

    q = rms_norm(aq.reshape(b, t, A_HEADS, A_HEAD_DIM), p['q_norm_g'])
    k = rms_norm(ak.reshape(b, t, A_KV_HEADS, A_HEAD_DIM), p['k_norm_g'])
    v = av.reshape(b, t, A_KV_HEADS, A_HEAD_DIM)
    o_a, new_k, new_v = swa_mixer(q, k, v, p['sinks'], kv_buf)
    o_b, new_dconv, new_s = delta_mixer(dqkv, da, db, dz, p['dn_conv_w'], p['dn_a_log'], p['dn_dt_bias'],
                                        p['dn_norm_g'], dconv0, s0)
    o_c, new_lconv, new_h = rglru_mixer(lx, p['lru_conv_w'], p['lru_conv_b'], p['lru_w_r'], p['lru_b_r'],
                                        p['lru_w_i'], p['lru_b_i'], p['lru_lambda'], lconv0, h0)

    branches = jnp.stack([o_a, o_b, o_c], axis=2)
    up = jnp.einsum('btgw,gwd->btgd', branches, p['w_branch'])
    gsig = jax.nn.sigmoid(gates.reshape(b, t, N_BRANCH, D_MODEL).astype(jnp.float32))
    merged = jnp.sum(gsig * up.astype(jnp.float32), axis=2).astype(x.dtype)
    x = x + gt1 * (merged @ p['w_out'])

    h2 = rms_norm(x, p['norm2_g']) * (1.0 + sc2) + sh2
    y = moe_ffn(h2.reshape(b * t, D_MODEL), p['w_router'], p['b_router'], p['w_gate_up'],
                p['b_gate_up'], p['w_down'], p['b_down']).reshape(b, t, D_MODEL)
    x = x + gt2 * y
    return x, (new_k, new_v, new_s, new_dconv, new_h, new_lconv)


def setup_inputs(seed: int = 0) -> dict:
    key = jax.random.key(seed)
    keys = iter(jax.random.split(key, 48))

    def nrm(shape, scale):
        return jax.random.normal(next(keys), shape, jnp.float32) * scale

    def unif(shape, lo, hi):
        return jax.random.uniform(next(keys), shape, jnp.float32, lo, hi)

    L = DEPTH
    nbuf = min(WINDOW, PAST_LEN)
    lru_p = unif((L, LRU_WIDTH), 0.9, 0.999) ** (1.0 / LRU_C)
    dt = jnp.exp(unif((L, DN_HEADS), math.log(1e-3), math.log(1e-1)))
    return {
        'x_prompt': nrm((BATCH, SEQ, D_MODEL), 1.0),
        'x_sample': nrm((DEC_BATCH, DEC_SEQ, D_MODEL), 1.0),
        'cache_k': nrm((L, DEC_BATCH, nbuf, A_KV_HEADS, A_HEAD_DIM), 1.0),
        'cache_v': nrm((L, DEC_BATCH, nbuf, A_KV_HEADS, A_HEAD_DIM), 1.0),
        'state_delta': nrm((L, DEC_BATCH, DN_HEADS, DN_DK, DN_DV), 0.5),
        'state_delta_conv': nrm((L, DEC_BATCH, CONV_WIDTH - 1, DN_CONV_CH), 1.0),
        'state_lru': nrm((L, DEC_BATCH, LRU_WIDTH), 0.5),
        'state_lru_conv': nrm((L, DEC_BATCH, CONV_WIDTH - 1, LRU_WIDTH), 1.0),
        'c_prompt': nrm((BATCH, D_MODEL), 1.0),
        'c_sample': nrm((DEC_BATCH, D_MODEL), 1.0),
        'w_ada': nrm((L, D_MODEL, 6 * D_MODEL), 0.5 * D_MODEL ** -0.5),
        'b_ada': nrm((L, 6 * D_MODEL), 0.01),
        'norm1_g': 1.0 + nrm((L, D_MODEL), 0.1),
        'norm2_g': 1.0 + nrm((L, D_MODEL), 0.1),
        'w_in': nrm((L, D_MODEL, IN_WIDTH), D_MODEL ** -0.5),
        'q_norm_g': 1.0 + nrm((L, A_HEAD_DIM), 0.1),
        'k_norm_g': 1.0 + nrm((L, A_HEAD_DIM), 0.1),
        'sinks': nrm((L, A_HEADS), 0.5),
        'dn_conv_w': nrm((L, CONV_WIDTH, DN_CONV_CH), CONV_WIDTH ** -0.5),
        'dn_a_log': jnp.log(unif((L, DN_HEADS), 1.0, 16.0)),
        'dn_dt_bias': dt + jnp.log(-jnp.expm1(-dt)),
        'dn_norm_g': 1.0 + nrm((L, DN_DV), 0.1),
        'lru_conv_w': nrm((L, CONV_WIDTH, LRU_WIDTH), CONV_WIDTH ** -0.5),
        'lru_conv_b': nrm((L, LRU_WIDTH), 0.01),
        'lru_w_r': nrm((L, LRU_BLOCKS, LRU_BLOCK, LRU_BLOCK), LRU_BLOCK ** -0.5),
        'lru_b_r': nrm((L, LRU_BLOCKS, LRU_BLOCK), 0.01),
        'lru_w_i': nrm((L, LRU_BLOCKS, LRU_BLOCK, LRU_BLOCK), LRU_BLOCK ** -0.5),
        'lru_b_i': nrm((L, LRU_BLOCKS, LRU_BLOCK), 0.01),
        'lru_lambda': jnp.log(lru_p) - jnp.log1p(-lru_p),
        'w_branch': nrm((L, N_BRANCH, BRANCH_WIDTH, D_MODEL), BRANCH_WIDTH ** -0.5),
        'w_out': nrm((L, D_MODEL, D_MODEL), D_MODEL ** -0.5),
        'w_router': nrm((L, D_MODEL, N_EXPERTS), D_MODEL ** -0.5),
        'b_router': nrm((L, N_EXPERTS), 0.01),
        'w_gate_up': nrm((L, N_EXPERTS, D_MODEL, 2 * D_FF), D_MODEL ** -0.5),
        'b_gate_up': nrm((L, N_EXPERTS, 2 * D_FF), 0.01),
        'w_down': nrm((L, N_EXPERTS, D_FF, D_MODEL), D_FF ** -0.5),
        'b_down': nrm((L, N_EXPERTS, D_MODEL), 0.01),
    }


def reference(x_prompt, x_sample, cache_k, cache_v, state_delta, state_delta_conv, state_lru,
              state_lru_conv, c_prompt, c_sample, w_ada, b_ada, norm1_g, norm2_g, w_in, q_norm_g,
              k_norm_g, sinks, dn_conv_w, dn_a_log, dn_dt_bias, dn_norm_g, lru_conv_w, lru_conv_b,
              lru_w_r, lru_b_r, lru_w_i, lru_b_i, lru_lambda, w_branch, w_out, w_router, b_router,
              w_gate_up, b_gate_up, w_down, b_down):
    xp, xs = x_prompt, x_sample
    new_p, new_s = [], []
    for l in range(DEPTH):
        p = {
            'w_ada': w_ada[l], 'b_ada': b_ada[l], 'norm1_g': norm1_g[l], 'norm2_g': norm2_g[l],
            'w_in': w_in[l], 'q_norm_g': q_norm_g[l], 'k_norm_g': k_norm_g[l], 'sinks': sinks[l],
            'dn_conv_w': dn_conv_w[l], 'dn_a_log': dn_a_log[l], 'dn_dt_bias': dn_dt_bias[l],
            'dn_norm_g': dn_norm_g[l], 'lru_conv_w': lru_conv_w[l], 'lru_conv_b': lru_conv_b[l],
            'lru_w_r': lru_w_r[l], 'lru_b_r': lru_b_r[l], 'lru_w_i': lru_w_i[l], 'lru_b_i': lru_b_i[l],
            'lru_lambda': lru_lambda[l], 'w_branch': w_branch[l], 'w_out': w_out[l],
            'w_router': w_router[l], 'b_router': b_router[l], 'w_gate_up': w_gate_up[l],
            'b_gate_up': b_gate_up[l], 'w_down': w_down[l], 'b_down': b_down[l],
        }
        xp, st_p = decoder_layer(xp, c_prompt, p, None)
        xs, st_s = decoder_layer(xs, c_sample, p, (cache_k[l], cache_v[l], state_delta[l],
                                                    state_delta_conv[l], state_lru[l], state_lru_conv[l]))
        new_p.append(st_p)
        new_s.append(st_s)
    k_p, v_p, d_p, dc_p, h_p, hc_p = (jnp.stack(z) for z in zip(*new_p))
    k_s, v_s, d_s, dc_s, h_s, hc_s = (jnp.stack(z) for z in zip(*new_s))
    return (xp, xs, k_p, v_p, d_p, dc_p, h_p, hc_p, k_s, v_s, d_s, dc_s, h_s, hc_s)
```

```python
import functools

import jax
import jax.numpy as jnp
from jax import lax
from jax.experimental import pallas as pl
from jax.experimental.pallas import tpu as pltpu

F32 = jnp.float32
BF16 = jnp.bfloat16
I32 = jnp.int32

D = 2048
BATCH = 2
SEQ = 4096
DEC_BATCH = 32
DEC_SEQ = 4
DEPTH = 2
MP = BATCH * SEQ
MS = DEC_BATCH * DEC_SEQ
NTOK = MP + MS

WINDOW = 128
HD = 64
NH = 16
NKV = 4
GRP = NH // NKV
ATT_SCALE = HD ** -0.5

DK = 128
DNH = 8
CHUNK = 64
CONV_CH = 3 * DNH * DK
CONV_W = 4

LW = 1024
LBLK = 128
LNB = LW // LBLK
LRU_C = 8.0

NE = 32
TOPK = 4
DFF = 2048
SW_ALPHA = 1.702
SW_LIMIT = 7.0
EPS = 1e-6

P_DQKV = 0
P_G = 3072
P_DZ = 9216
P_LX = 10240
P_Q = 11264
P_K = 12288
P_V = 12544
P_W = 12800

RB = 256
NB_MAX = -(-(NTOK * TOPK) // RB) + NE
ROWS = NB_MAX * RB
TN_E = 512
NJ = DFF // TN_E
S_MAX = NB_MAX * NJ
TOK_TILE = 128
N_TT = NTOK // TOK_TILE

VMEM_BIG = 56 * 1024 * 1024


def _sds(shape, dtype):
    return jax.ShapeDtypeStruct(shape, dtype)


def _cparams(n_axes, vmem=None):
    return pltpu.CompilerParams(dimension_semantics=("arbitrary",) * n_axes, vmem_limit_bytes=vmem)


def _dot(a, b):
    return jnp.dot(a, b, preferred_element_type=F32)


def _dot_nt(a, b):
    return lax.dot_general(a, b, (((1,), (1,)), ((), ())), preferred_element_type=F32)


def _dot_tn(a, b):
    return lax.dot_general(a, b, (((0,), (0,)), ((), ())), preferred_element_type=F32)


def _rms(x, g):
    return x * lax.rsqrt(jnp.mean(x * x, axis=-1, keepdims=True) + EPS) * g


def _sigmoid(x):
    return 1.0 / (1.0 + jnp.exp(-x))


def _silu(x):
    return x * _sigmoid(x)


def _softplus(x):
    return jnp.maximum(x, 0.0) + jnp.log1p(jnp.exp(-jnp.abs(x)))


def _ada_kernel(c_ref, w_ref, b_ref, o_ref):
    a = _silu(c_ref[...]).astype(BF16)
    o_ref[0] = _dot(a, w_ref[0].astype(BF16)) + b_ref[0]


def _ada_mod(c_all, w_ada, b_ada):
    rows = c_all.shape[0]
    tn = 1024
    return pl.pallas_call(
        _ada_kernel,
        grid=(DEPTH, 6 * D // tn),
        in_specs=[
            pl.BlockSpec((rows, D), lambda l, n: (0, 0)),
            pl.BlockSpec((1, D, tn), lambda l, n: (l, 0, n)),
            pl.BlockSpec((1, 1, tn), lambda l, n: (l, 0, n)),
        ],
        out_specs=pl.BlockSpec((1, rows, tn), lambda l, n: (l, 0, n)),
        out_shape=_sds((DEPTH, rows, 6 * D), F32),
        compiler_params=_cparams(2, VMEM_BIG),
        name="ada_mod",
    )(c_all, w_ada, b_ada.reshape(DEPTH, 1, 6 * D))


def _adaln_kernel(x_ref, g_ref, sh_ref, sc_ref, o_ref):
    y = _rms(x_ref[...], g_ref[...])
    o_ref[...] = (y * (1.0 + sc_ref[0, 0]) + sh_ref[0, 0]).astype(o_ref.dtype)


def _mod_spec(mod, j, tiles, tn=D, with_n=False):
    g, rb = mod.shape[1], mod.shape[2]
    tpg = tiles // g
    if with_n:
        return pl.BlockSpec((1, 1, rb, tn), lambda n, m: (j, m // tpg, 0, n))
    return pl.BlockSpec((1, 1, rb, tn), lambda m: (j, m // tpg, 0, 0))


def _adaln(x, g, mod, j_sh, j_sc, tm):
    m_rows = x.shape[0]
    tiles = m_rows // tm
    return pl.pallas_call(
        _adaln_kernel,
        grid=(tiles,),
        in_specs=[
            pl.BlockSpec((tm, D), lambda m: (m, 0)),
            pl.BlockSpec((1, D), lambda m: (0, 0)),
            _mod_spec(mod, j_sh, tiles),
            _mod_spec(mod, j_sc, tiles),
        ],
        out_specs=pl.BlockSpec((tm, D), lambda m: (m, 0)),
        out_shape=_sds((m_rows, D), BF16),
        compiler_params=_cparams(1),
        name="adaln1",
    )(x, g, mod, mod)


def _mm_kernel(x_ref, w_ref, o_ref):
    o_ref[...] = _dot(x_ref[...], w_ref[...]).astype(o_ref.dtype)


def _matmul(x, w, out_dtype, tm, tn, name):
    m_rows, k = x.shape
    n_cols = w.shape[1]
    return pl.pallas_call(
        _mm_kernel,
        grid=(n_cols // tn, m_rows // tm),
        in_specs=[
            pl.BlockSpec((tm, k), lambda n, m: (m, 0)),
            pl.BlockSpec((k, tn), lambda n, m: (0, n)),
        ],
        out_specs=pl.BlockSpec((tm, tn), lambda n, m: (m, n)),
        out_shape=_sds((m_rows, n_cols), out_dtype),
        compiler_params=_cparams(2, VMEM_BIG),
        name=name,
    )(x, w)


def _attn_prompt_kernel(sinks_ref, q_ref, kc_ref, vc_ref, kp_ref, vp_ref, qg_ref, kg_ref,
                        o_ref, ko_ref, vo_ref):
    nblk = SEQ // WINDOW
    first = (pl.program_id(0) % nblk) == 0
    row = lax.broadcasted_iota(I32, (WINDOW, 2 * WINDOW), 0)
    col = lax.broadcasted_iota(I32, (WINDOW, 2 * WINDOW), 1)
    lo = jnp.where(first, WINDOW, 0)
    mask = (col > row) & (col <= row + WINDOW) & (col >= lo)
    qg = qg_ref[...]
    kg = kg_ref[...]
    outs = []
    k_out = []
    for kh in range(NKV):
        ks = slice(kh * HD, (kh + 1) * HD)
        kc_n = _rms(kc_ref[:, ks], kg)
        kp_n = _rms(kp_ref[:, ks], kg)
        k_out.append(kc_n)
        kk = jnp.concatenate([kp_n, kc_n], axis=0).astype(BF16)
        vv = jnp.concatenate([vp_ref[:, ks], vc_ref[:, ks]], axis=0).astype(BF16)
        for g in range(GRP):
            h = kh * GRP + g
            qh = _rms(q_ref[:, h * HD:(h + 1) * HD], qg).astype(BF16)
            s = _dot_nt(qh, kk) * ATT_SCALE
            s = jnp.where(mask, s, -jnp.inf)
            sink = sinks_ref[h]
            m = jnp.maximum(jnp.max(s, axis=-1, keepdims=True), sink)
            p = jnp.exp(s - m)
            den = jnp.sum(p, axis=-1, keepdims=True) + jnp.exp(sink - m)
            outs.append(_dot(p.astype(BF16), vv) / den)
    o_ref[...] = jnp.concatenate(outs, axis=-1).astype(o_ref.dtype)
    ko_ref[0] = jnp.concatenate(k_out, axis=-1)
    vo_ref[0] = vc_ref[...]


def _attn_prompt(p_act, qg, kg, sinks):
    nblk = SEQ // WINDOW
    cq, ck, cv = P_Q // 1024, P_K // 256, P_V // 256
    return pl.pallas_call(
        _attn_prompt_kernel,
        grid=(MP // WINDOW,),
        in_specs=[
            pl.BlockSpec(memory_space=pltpu.SMEM),
            pl.BlockSpec((WINDOW, 1024), lambda g: (g, cq)),
            pl.BlockSpec((WINDOW, 256), lambda g: (g, ck)),
            pl.BlockSpec((WINDOW, 256), lambda g: (g, cv)),
            pl.BlockSpec((WINDOW, 256), lambda g: (jnp.maximum(g - 1, 0), ck)),
            pl.BlockSpec((WINDOW, 256), lambda g: (jnp.maximum(g - 1, 0), cv)),
            pl.BlockSpec((1, HD), lambda g: (0, 0)),
            pl.BlockSpec((1, HD), lambda g: (0, 0)),
        ],
        out_specs=[
            pl.BlockSpec((WINDOW, 1024), lambda g: (g, 0)),
            pl.BlockSpec((1, WINDOW, 256), lambda g: (g // nblk, 0, 0)),
            pl.BlockSpec((1, WINDOW, 256), lambda g: (g // nblk, 0, 0)),
        ],
        out_shape=[_sds((MP, 1024), BF16), _sds((BATCH, WINDOW, 256), F32),
                   _sds((BATCH, WINDOW, 256), F32)],
        compiler_params=_cparams(1),
        name="attn_prompt",
    )(sinks, p_act, p_act, p_act, p_act, p_act, qg, kg)


def _attn_sample_kernel(sinks_ref, q_ref, kn_ref, vn_ref, ck_ref, cv_ref, qg_ref, kg_ref,
                        o_ref, kno_ref):
    rows = GRP * DEC_SEQ
    t = lax.broadcasted_iota(I32, (rows, 1), 0) % DEC_SEQ
    g_of_row = lax.broadcasted_iota(I32, (rows, 1), 0) // DEC_SEQ
    col_c = lax.broadcasted_iota(I32, (rows, WINDOW), 1)
    col_n = lax.broadcasted_iota(I32, (rows, 8), 1)
    qg = qg_ref[...]
    kg = kg_ref[...]
    for kh in range(NKV):
        q16 = _rms(q_ref[0, kh], qg).astype(BF16)
        kn = _rms(kn_ref[0, kh], kg)
        kno_ref[0, kh] = kn
        s_c = _dot_nt(q16, ck_ref[0, kh].astype(BF16)) * ATT_SCALE
        s_n = _dot_nt(q16, kn.astype(BF16)) * ATT_SCALE
        s_c = jnp.where(col_c > t, s_c, -jnp.inf)
        s_n = jnp.where(col_n <= t, s_n, -jnp.inf)
        sink = jnp.zeros((rows, 1), F32)
        for g in range(GRP):
            sink = jnp.where(g_of_row == g, sinks_ref[kh * GRP + g], sink)
        m = jnp.maximum(jnp.maximum(jnp.max(s_c, axis=-1, keepdims=True),
                                    jnp.max(s_n, axis=-1, keepdims=True)), sink)
        p_c = jnp.exp(s_c - m)
        p_n = jnp.exp(s_n - m)
        den = (jnp.sum(p_c, axis=-1, keepdims=True) + jnp.sum(p_n, axis=-1, keepdims=True)
               + jnp.exp(sink - m))
        o = _dot(p_c.astype(BF16), cv_ref[0, kh].astype(BF16)) + _dot(
            p_n.astype(BF16), vn_ref[0, kh].astype(BF16))
        o_ref[0, kh] = o / den


def _attn_sample(q_r, kn_r, vn_r, ck_r, cv_r, qg, kg, sinks):
    rows = GRP * DEC_SEQ
    return pl.pallas_call(
        _attn_sample_kernel,
        grid=(DEC_BATCH,),
        in_specs=[
            pl.BlockSpec(memory_space=pltpu.SMEM),
            pl.BlockSpec((1, NKV, rows, HD), lambda b: (b, 0, 0, 0)),
            pl.BlockSpec((1, NKV, 8, HD), lambda b: (b, 0, 0, 0)),
            pl.BlockSpec((1, NKV, 8, HD), lambda b: (b, 0, 0, 0)),
            pl.BlockSpec((1, NKV, WINDOW, HD), lambda b: (b, 0, 0, 0)),
            pl.BlockSpec((1, NKV, WINDOW, HD), lambda b: (b, 0, 0, 0)),
            pl.BlockSpec((1, HD), lambda b: (0, 0)),
            pl.BlockSpec((1, HD), lambda b: (0, 0)),
        ],
        out_specs=[
            pl.BlockSpec((1, NKV, rows, HD), lambda b: (b, 0, 0, 0)),
            pl.BlockSpec((1, NKV, 8, HD), lambda b: (b, 0, 0, 0)),
        ],
        out_shape=[_sds((DEC_BATCH, NKV, rows, HD), F32), _sds((DEC_BATCH, NKV, 8, HD), F32)],
        compiler_params=_cparams(1),
        name="attn_sample",
    )(sinks, q_r, kn_r, vn_r, ck_r, cv_r, qg, kg)


DT_KK = BF16
DT_STATE = BF16


def _split(a):
    hi = a.astype(BF16)
    return hi, (a - hi.astype(F32)).astype(BF16)


def _dot_inv(a, b):
    a_hi, a_lo = _split(a)
    b_hi, b_lo = _split(b)
    return _dot(a_hi, b_hi) + (_dot(a_hi, b_lo) + _dot(a_lo, b_hi))
DT_OUT = BF16


def _cumsum_rows(x, c):
    row = lax.broadcasted_iota(I32, x.shape, 0)
    s = 1
    while s < c:
        x = x + jnp.where(row >= s, pltpu.roll(x, s, 0), 0.0)
        s *= 2
    return x


def _delta_chunk(c, conv_slice, g_full, beta_full, z_slice, norm_g, s_get, s_put, o_put):
    r = lax.broadcasted_iota(I32, (c, c), 0)
    cc = lax.broadcasted_iota(I32, (c, c), 1)
    incl = r >= cc
    strict = r > cc
    eye_b = r == cc
    eye = eye_b.astype(F32)
    gc = _cumsum_rows(g_full, c)
    n_sq = c.bit_length() - 2
    for h in range(DNH):
        qh = conv_slice(slice(h * DK, (h + 1) * DK))
        kh = conv_slice(slice(DNH * DK + h * DK, DNH * DK + (h + 1) * DK))
        vh = conv_slice(slice(2 * DNH * DK + h * DK, 2 * DNH * DK + (h + 1) * DK))
        qh = qh * lax.rsqrt(jnp.sum(qh * qh, axis=-1, keepdims=True) + EPS) * (DK ** -0.5)
        kh = kh * lax.rsqrt(jnp.sum(kh * kh, axis=-1, keepdims=True) + EPS)
        gcol = gc[:, h:h + 1]
        grow = jnp.sum(jnp.where(eye_b, gcol, 0.0), axis=0, keepdims=True)
        bcol = beta_full[:, DNH + h:DNH + h + 1]
        decay = jnp.exp(jnp.where(incl, gcol - grow, -jnp.inf))
        egc = jnp.exp(gcol)
        kb = kh * bcol
        a_mat = jnp.where(strict, _dot_nt(kb.astype(DT_KK), kh.astype(DT_KK)) * decay, 0.0)
        t_inv = eye - a_mat
        pw = a_mat
        for _ in range(n_sq):
            pw = _dot_inv(pw, pw)
            t_inv = t_inv + _dot_inv(t_inv, pw)
        rhs = jnp.concatenate([vh * bcol, kb * egc], axis=-1)
        sol = _dot_inv(t_inv, rhs)
        u_in = sol[:, :DK]
        w = sol[:, DK:]
        qk = _dot_nt(qh.astype(DT_OUT), kh.astype(DT_OUT)) * decay
        g_last = gcol[c - 1:c, :]
        k_dec = kh * jnp.exp(g_last - gcol)
        s_old = s_get(h)
        s16 = s_old.astype(DT_STATE)
        u = u_in - _dot(w.astype(DT_STATE), s16)
        u16 = u.astype(DT_STATE)
        o = _dot((qh * egc).astype(DT_OUT), s_old.astype(DT_OUT)) + _dot(qk.astype(DT_OUT),
                                                                         u.astype(DT_OUT))
        s_put(h, s_old * jnp.exp(g_last) + _dot_tn(k_dec.astype(DT_STATE), u16))
        zz = z_slice(slice(h * DK, (h + 1) * DK))
        o_put(h, _rms(o, norm_g) * _silu(zz))


def _delta_prompt_kernel(x_ref, prev_ref, dz_ref, dab_ref, cw_ref, alog_ref, dtb_ref, ng_ref,
                         o_ref, s_ref, dc_ref):
    c = pl.program_id(1)

    @pl.when(c == 0)
    def _():
        s_ref[...] = jnp.zeros_like(s_ref)

    row8 = lax.broadcasted_iota(I32, (8, DK), 0)
    keep_prev = c > 0

    def conv_slice(cs):
        x = x_ref[:, cs]
        prev = jnp.where(keep_prev, prev_ref[:, cs], 0.0)
        y = x * cw_ref[CONV_W - 1:CONV_W, cs]
        for s in range(1, CONV_W):
            xr = pltpu.roll(x, s, 0)
            top = jnp.where(row8 < s, pltpu.roll(prev, s, 0), xr[0:8])
            sh = jnp.concatenate([top, xr[8:]], axis=0)
            y = y + sh * cw_ref[CONV_W - 1 - s:CONV_W - s, cs]
        return _silu(y)

    dab = dab_ref[...]
    g_full = -jnp.exp(alog_ref[...]) * _softplus(dab + dtb_ref[...])
    beta_full = _sigmoid(dab)

    def s_get(h):
        return s_ref[0, h]

    def s_put(h, v):
        s_ref[0, h] = v

    def o_put(h, v):
        o_ref[:, h * DK:(h + 1) * DK] = v.astype(o_ref.dtype)

    _delta_chunk(CHUNK, conv_slice, g_full, beta_full, lambda cs: dz_ref[:, cs], ng_ref[...],
                 s_get, s_put, o_put)

    @pl.when(c == pl.num_programs(1) - 1)
    def _():
        dc_ref[0] = x_ref[CHUNK - (CONV_W - 1):CHUNK, :]


def _delta_prompt(p_act, p_ab, conv_w, alog_row, dtb_row, norm_g):
    nck = SEQ // CHUNK
    return pl.pallas_call(
        _delta_prompt_kernel,
        grid=(BATCH, nck),
        in_specs=[
            pl.BlockSpec((CHUNK, CONV_CH), lambda b, c: (b * nck + c, P_DQKV // CONV_CH)),
            pl.BlockSpec((8, CONV_CH),
                         lambda b, c: (jnp.maximum(b * (SEQ // 8) + c * (CHUNK // 8) - 1, 0), 0)),
            pl.BlockSpec((CHUNK, 1024), lambda b, c: (b * nck + c, P_DZ // 1024)),
            pl.BlockSpec((CHUNK, 128), lambda b, c: (b * nck + c, 0)),
            pl.BlockSpec((CONV_W, CONV_CH), lambda b, c: (0, 0)),
            pl.BlockSpec((1, 128), lambda b, c: (0, 0)),
            pl.BlockSpec((1, 128), lambda b, c: (0, 0)),
            pl.BlockSpec((1, DK), lambda b, c: (0, 0)),
        ],
        out_specs=[
            pl.BlockSpec((CHUNK, 1024), lambda b, c: (b * nck + c, 0)),
            pl.BlockSpec((1, DNH, DK, DK), lambda b, c: (b, 0, 0, 0)),
            pl.BlockSpec((1, CONV_W - 1, CONV_CH), lambda b, c: (b, 0, 0)),
        ],
        out_shape=[_sds((MP, 1024), BF16), _sds((BATCH, DNH, DK, DK), F32),
                   _sds((BATCH, CONV_W - 1, CONV_CH), F32)],
        compiler_params=_cparams(2),
        name="delta_prompt",
    )(p_act, p_act, p_act, p_ab, conv_w, alog_row, dtb_row, norm_g)


def _delta_sample_kernel(xp_ref, dz_ref, dab_ref, s0_ref, cw_ref, alog_ref, dtb_ref, ng_ref,
                         o_ref, s_ref):
    row = lax.broadcasted_iota(I32, (8, DK), 0)
    live = row < DEC_SEQ

    def conv_slice(cs):
        xp = xp_ref[0, :, cs]
        y = jnp.zeros((8, DK), F32)
        for j in range(CONV_W):
            y = y + pltpu.roll(xp, 8 - 1 - j, 0) * cw_ref[j:j + 1, cs]
        return jnp.where(live, _silu(y), 0.0)

    dab = dab_ref[0]
    g_full = jnp.where(live, -jnp.exp(alog_ref[...]) * _softplus(dab + dtb_ref[...]), 0.0)
    beta_full = jnp.where(live, _sigmoid(dab), 0.0)

    def s_put(h, v):
        s_ref[0, h] = v

    def o_put(h, v):
        o_ref[0, :, h * DK:(h + 1) * DK] = v

    _delta_chunk(8, conv_slice, g_full, beta_full, lambda cs: dz_ref[0, :, cs], ng_ref[...],
                 lambda h: s0_ref[0, h], s_put, o_put)


def _delta_sample(xp8, dz8, dab8, s0, conv_w, alog_row, dtb_row, norm_g):
    return pl.pallas_call(
        _delta_sample_kernel,
        grid=(DEC_BATCH,),
        in_specs=[
            pl.BlockSpec((1, 8, CONV_CH), lambda b: (b, 0, 0)),
            pl.BlockSpec((1, 8, 1024), lambda b: (b, 0, 0)),
            pl.BlockSpec((1, 8, 128), lambda b: (b, 0, 0)),
            pl.BlockSpec((1, DNH, DK, DK), lambda b: (b, 0, 0, 0)),
            pl.BlockSpec((CONV_W, CONV_CH), lambda b: (0, 0)),
            pl.BlockSpec((1, 128), lambda b: (0, 0)),
            pl.BlockSpec((1, 128), lambda b: (0, 0)),
            pl.BlockSpec((1, DK), lambda b: (0, 0)),
        ],
        out_specs=[
            pl.BlockSpec((1, 8, 1024), lambda b: (b, 0, 0)),
            pl.BlockSpec((1, DNH, DK, DK), lambda b: (b, 0, 0, 0)),
        ],
        out_shape=[_sds((DEC_BATCH, 8, 1024), F32), _sds((DEC_BATCH, DNH, DK, DK), F32)],
        compiler_params=_cparams(1),
        name="delta_sample",
    )(xp8, dz8, dab8, s0, conv_w, alog_row, dtb_row, norm_g)


def _lru_gates(xc, wr, br, wi, bi, sp):
    x16 = xc.astype(BF16)
    r = _sigmoid(_dot(x16, wr) + br)
    i = _sigmoid(_dot(x16, wi) + bi)
    log_a = -LRU_C * r * sp
    a = jnp.exp(log_a)
    th = jnp.tanh(log_a)
    u = jnp.sqrt(-2.0 * th / (1.0 - th)) * (i * xc)
    return a, u


LRU_TT = 256


def _lru_prompt_kernel(x_ref, prev_ref, cw_ref, cb_ref, wr_ref, br_ref, wi_ref, bi_ref, lam_ref,
                       o_ref, h_ref, lc_ref):
    t = pl.program_id(1)
    row8 = lax.broadcasted_iota(I32, (8, LBLK), 0)
    row = lax.broadcasted_iota(I32, (LRU_TT, LBLK), 0)
    keep = t > 0
    for n in range(LNB):
        cs = slice(n * LBLK, (n + 1) * LBLK)
        x = x_ref[:, cs]
        prev = jnp.where(keep, prev_ref[:, cs], 0.0)
        y = x * cw_ref[CONV_W - 1:CONV_W, cs]
        for s in range(1, CONV_W):
            xr = pltpu.roll(x, s, 0)
            top = jnp.where(row8 < s, pltpu.roll(prev, s, 0), xr[0:8])
            y = y + jnp.concatenate([top, xr[8:]], axis=0) * cw_ref[CONV_W - 1 - s:CONV_W - s, cs]
        xc = y + cb_ref[:, cs]
        sp = _softplus(-lam_ref[:, cs])
        a, u = _lru_gates(xc, wr_ref[n], br_ref[:, cs], wi_ref[n], bi_ref[:, cs], sp)
        h0 = jnp.where(keep, h_ref[0, :, cs], 0.0)
        u = u + jnp.where(row == 0, a * h0, 0.0)
        s = 1
        while s < LRU_TT:
            valid = row >= s
            u_s = pltpu.roll(u, s, 0)
            a_s = pltpu.roll(a, s, 0)
            u = jnp.where(valid, a * u_s + u, u)
            a = jnp.where(valid, a * a_s, a)
            s *= 2
        o_ref[:, cs] = u.astype(o_ref.dtype)
        h_ref[0, :, cs] = u[LRU_TT - 1:LRU_TT, :]

    @pl.when(t == pl.num_programs(1) - 1)
    def _():
        lc_ref[0] = x_ref[LRU_TT - (CONV_W - 1):LRU_TT, :]


def _lru_prompt(p_act, cw, cb, wr, br, wi, bi, lam):
    ntt = SEQ // LRU_TT
    cl = P_LX // LW
    vec = pl.BlockSpec((1, LW), lambda b, t: (0, 0))
    mat = pl.BlockSpec((LNB, LBLK, LBLK), lambda b, t: (0, 0, 0))
    return pl.pallas_call(
        _lru_prompt_kernel,
        grid=(BATCH, ntt),
        in_specs=[
            pl.BlockSpec((LRU_TT, LW), lambda b, t: (b * ntt + t, cl)),
            pl.BlockSpec((8, LW),
                         lambda b, t: (jnp.maximum(b * (SEQ // 8) + t * (LRU_TT // 8) - 1, 0), cl)),
            pl.BlockSpec((CONV_W, LW), lambda b, t: (0, 0)),
            vec, mat, vec, mat, vec, vec,
        ],
        out_specs=[
            pl.BlockSpec((LRU_TT, LW), lambda b, t: (b * ntt + t, 0)),
            pl.BlockSpec((1, 1, LW), lambda b, t: (b, 0, 0)),
            pl.BlockSpec((1, CONV_W - 1, LW), lambda b, t: (b, 0, 0)),
        ],
        out_shape=[_sds((MP, LW), BF16), _sds((BATCH, 1, LW), F32),
                   _sds((BATCH, CONV_W - 1, LW), F32)],
        compiler_params=_cparams(2),
        name="lru_prompt",
    )(p_act, p_act, cw, cb, wr, br, wi, bi, lam)


def _lru_sample_kernel(x_ref, buf_ref, h0_ref, cw_ref, cb_ref, wr_ref, br_ref, wi_ref, bi_ref,
                       lam_ref, o_ref):
    for n in range(LNB):
        cs = slice(n * LBLK, (n + 1) * LBLK)
        xp = [buf_ref[j, :, cs] for j in range(CONV_W - 1)] + [x_ref[t, :, cs] for t in range(DEC_SEQ)]
        xcs = []
        for t in range(DEC_SEQ):
            y = xp[t] * cw_ref[0:1, cs]
            for j in range(1, CONV_W):
                y = y + xp[t + j] * cw_ref[j:j + 1, cs]
            xcs.append(y + cb_ref[:, cs])
        xc = jnp.concatenate(xcs, axis=0)
        sp = _softplus(-lam_ref[:, cs])
        a, u = _lru_gates(xc, wr_ref[n], br_ref[:, cs], wi_ref[n], bi_ref[:, cs], sp)
        h = h0_ref[:, cs]
        for t in range(DEC_SEQ):
            rs = slice(t * DEC_BATCH, (t + 1) * DEC_BATCH)
            h = a[rs] * h + u[rs]
            o_ref[t, :, cs] = h


def _lru_sample(x_tm, buf_tm, h0, cw, cb, wr, br, wi, bi, lam):
    return pl.pallas_call(
        _lru_sample_kernel,
        out_shape=_sds((DEC_SEQ, DEC_BATCH, LW), F32),
        name="lru_sample",
    )(x_tm, buf_tm, h0, cw, cb, wr, br, wi, bi, lam)


def _merge_kernel(oa_ref, ob_ref, oc_ref, w_ref, ga_ref, gb_ref, gc_ref, o_ref):
    acc = _sigmoid(ga_ref[...]) * _dot(oa_ref[...], w_ref[0])
    acc = acc + _sigmoid(gb_ref[...]) * _dot(ob_ref[...], w_ref[1])
    acc = acc + _sigmoid(gc_ref[...]) * _dot(oc_ref[...], w_ref[2])
    o_ref[...] = acc.astype(o_ref.dtype)


def _merge(o_a, o_b, o_c, wb, p_act, tm):
    m_rows = o_a.shape[0]
    tn = 512
    g0 = P_G // tn
    gs = D // tn
    br = pl.BlockSpec((tm, 1024), lambda n, m: (m, 0))
    return pl.pallas_call(
        _merge_kernel,
        grid=(D // tn, m_rows // tm),
        in_specs=[
            br, br, br,
            pl.BlockSpec((3, 1024, tn), lambda n, m: (0, 0, n)),
            pl.BlockSpec((tm, tn), lambda n, m: (m, g0 + n)),
            pl.BlockSpec((tm, tn), lambda n, m: (m, g0 + gs + n)),
            pl.BlockSpec((tm, tn), lambda n, m: (m, g0 + 2 * gs + n)),
        ],
        out_specs=pl.BlockSpec((tm, tn), lambda n, m: (m, n)),
        out_shape=_sds((m_rows, D), BF16),
        compiler_params=_cparams(2, VMEM_BIG),
        name="merge",
    )(o_a, o_b, o_c, wb, p_act, p_act, p_act)


def _outproj_kernel(a_ref, w_ref, x_ref, gt_ref, o_ref):
    o_ref[...] = x_ref[...] + gt_ref[0, 0] * _dot(a_ref[...], w_ref[...])


def _outproj(merged, w_out, x, mod, tm):
    m_rows = x.shape[0]
    tn = 512
    tiles = m_rows // tm
    return pl.pallas_call(
        _outproj_kernel,
        grid=(D // tn, tiles),
        in_specs=[
            pl.BlockSpec((tm, D), lambda n, m: (m, 0)),
            pl.BlockSpec((D, tn), lambda n, m: (0, n)),
            pl.BlockSpec((tm, tn), lambda n, m: (m, n)),
            _mod_spec(mod, 2, tiles, tn, with_n=True),
        ],
        out_specs=pl.BlockSpec((tm, tn), lambda n, m: (m, n)),
        out_shape=_sds((m_rows, D), F32),
        compiler_params=_cparams(2, VMEM_BIG),
        name="outproj",
    )(merged, w_out, x, mod)


def _router_kernel(x_ref, g_ref, sh_ref, sc_ref, wr_ref, br_ref, h_ref, te_ref, tw_ref):
    tm = x_ref.shape[0]
    h = _rms(x_ref[...], g_ref[...]) * (1.0 + sc_ref[0, 0]) + sh_ref[0, 0]
    h_ref[...] = h
    lane = lax.broadcasted_iota(I32, (tm, 128), 1)
    logits = _dot(h.astype(BF16), wr_ref[...]) + br_ref[...]
    logits = jnp.where(lane < NE, logits, -jnp.inf)
    lane4 = lax.broadcasted_iota(I32, (tm, TOPK), 1)
    te = jnp.zeros((tm, TOPK), I32)
    tl = jnp.zeros((tm, TOPK), F32)
    for k in range(TOPK):
        m = jnp.max(logits, axis=-1, keepdims=True)
        idx = jnp.min(jnp.where(logits == m, lane, 128), axis=-1, keepdims=True)
        te = jnp.where(lane4 == k, idx, te)
        tl = jnp.where(lane4 == k, m, tl)
        logits = jnp.where(lane == idx, -jnp.inf, logits)
    e = jnp.exp(tl - tl[:, 0:1])
    te_ref[...] = te
    tw_ref[...] = e / jnp.sum(e, axis=-1, keepdims=True)


def _router(x1, g, mod, wr, br, tm):
    m_rows = x1.shape[0]
    tiles = m_rows // tm
    return pl.pallas_call(
        _router_kernel,
        grid=(tiles,),
        in_specs=[
            pl.BlockSpec((tm, D), lambda m: (m, 0)),
            pl.BlockSpec((1, D), lambda m: (0, 0)),
            _mod_spec(mod, 3, tiles),
            _mod_spec(mod, 4, tiles),
            pl.BlockSpec((D, 128), lambda m: (0, 0)),
            pl.BlockSpec((1, 128), lambda m: (0, 0)),
        ],
        out_specs=[
            pl.BlockSpec((tm, D), lambda m: (m, 0)),
            pl.BlockSpec((tm, TOPK), lambda m: (m, 0)),
            pl.BlockSpec((tm, TOPK), lambda m: (m, 0)),
        ],
        out_shape=[_sds((m_rows, D), F32), _sds((m_rows, TOPK), I32), _sds((m_rows, TOPK), F32)],
        compiler_params=_cparams(1),
        name="router",
    )(x1, g, mod, mod, wr, br)


def _rank_kernel(te_ref, rk_ref, cnt_ref):
    @pl.when(pl.program_id(0) == 0)
    def _():
        cnt_ref[...] = jnp.zeros_like(cnt_ref)

    tt = TOK_TILE
    lane = lax.broadcasted_iota(I32, (tt, 128), 1)
    te = te_ref[...]
    hot = jnp.zeros((tt, 128), F32)
    for k in range(TOPK):
        hot = hot + (lane == te[:, k:k + 1]).astype(F32)
    r = lax.broadcasted_iota(I32, (tt, tt), 0)
    c = lax.broadcasted_iota(I32, (tt, tt), 1)
    before = (r > c).astype(BF16)
    tot = _dot(before, hot.astype(BF16)) + cnt_ref[...]
    lane4 = lax.broadcasted_iota(I32, (tt, TOPK), 1)
    rk = jnp.zeros((tt, TOPK), F32)
    for k in range(TOPK):
        rk_k = jnp.sum(jnp.where(lane == te[:, k:k + 1], tot, 0.0), axis=-1, keepdims=True)
        rk = jnp.where(lane4 == k, rk_k, rk)
    rk_ref[...] = rk.astype(I32)
    cnt_ref[...] = cnt_ref[...] + jnp.sum(hot, axis=0, keepdims=True)


def _rank(te):
    return pl.pallas_call(
        _rank_kernel,
        grid=(N_TT,),
        in_specs=[pl.BlockSpec((TOK_TILE, TOPK), lambda i: (i, 0))],
        out_specs=[pl.BlockSpec((TOK_TILE, TOPK), lambda i: (i, 0)),
                   pl.BlockSpec((1, 128), lambda i: (0, 0))],
        out_shape=[_sds((NTOK, TOPK), I32), _sds((1, 128), F32)],
        compiler_params=_cparams(1),
        name="rank",
    )(te)


def _scatter_kernel(dest_ref, h_hbm, xs_in, xs_out, sem):
    del xs_in
    base = pl.program_id(0) * TOK_TILE

    def body(i, carry):
        for k in range(TOPK):
            d = dest_ref[0, 0, i * TOPK + k]
            pltpu.make_async_copy(h_hbm.at[pl.ds(base + i, 1)], xs_out.at[pl.ds(d, 1)], sem).start()
        return carry

    lax.fori_loop(0, TOK_TILE, body, 0)
    n = TOK_TILE * TOPK
    pltpu.make_async_copy(h_hbm.at[pl.ds(0, n)], xs_out.at[pl.ds(0, n)], sem).wait()


def _scatter_rows(dest3, h2, xs_init):
    return pl.pallas_call(
        _scatter_kernel,
        grid=(N_TT,),
        in_specs=[
            pl.BlockSpec((1, 1, TOK_TILE * TOPK), lambda i: (i, 0, 0), memory_space=pltpu.SMEM),
            pl.BlockSpec(memory_space=pl.ANY),
            pl.BlockSpec(memory_space=pl.ANY),
        ],
        out_specs=pl.BlockSpec(memory_space=pl.ANY),
        out_shape=_sds((ROWS, D), F32),
        scratch_shapes=[pltpu.SemaphoreType.DMA(())],
        input_output_aliases={2: 0},
        compiler_params=_cparams(1),
        name="scatter_rows",
    )(dest3, h2, xs_init)


STEP_FILL = 0
STEP_RUN = 1
STEP_LOAD = 2


def _gmm_up_kernel(se, sjw, sr, sjo, kind, x_ref, wg_ref, wl_ref, bg_ref, bl_ref, o_ref,
                   wg_s, wl_s):
    s = pl.program_id(0)

    @pl.when(kind[s] == STEP_LOAD)
    def _():
        wg_s[...] = wg_ref[0, 0].astype(BF16)
        wl_s[...] = wl_ref[0, 0].astype(BF16)

    @pl.when(kind[s] != STEP_FILL)
    def _():
        x = x_ref[...].astype(BF16)
        glu = _dot(x, wg_s[...]) + bg_ref[0, 0]
        lin = _dot(x, wl_s[...]) + bl_ref[0, 0]
        glu = jnp.minimum(glu, SW_LIMIT)
        lin = jnp.clip(lin, -SW_LIMIT, SW_LIMIT)
        o_ref[...] = (glu * _sigmoid(SW_ALPHA * glu) * (lin + 1.0)).astype(o_ref.dtype)

    @pl.when(kind[s] == STEP_FILL)
    def _():
        o_ref[...] = jnp.zeros_like(o_ref)


def _gmm_up(steps, xs, w_gu, b_gu, layer):
    grid_spec = pltpu.PrefetchScalarGridSpec(
        num_scalar_prefetch=5,
        grid=(S_MAX,),
        in_specs=[
            pl.BlockSpec((RB, D), lambda s, se, sjw, sr, sjo, kd: (sr[s], 0)),
            pl.BlockSpec((1, 1, D, TN_E), lambda s, se, sjw, sr, sjo, kd: (layer, se[s], 0, sjw[s])),
            pl.BlockSpec((1, 1, D, TN_E),
                         lambda s, se, sjw, sr, sjo, kd: (layer, se[s], 0, NJ + sjw[s])),
            pl.BlockSpec((1, 1, 1, TN_E), lambda s, se, sjw, sr, sjo, kd: (layer, se[s], 0, sjw[s])),
            pl.BlockSpec((1, 1, 1, TN_E),
                         lambda s, se, sjw, sr, sjo, kd: (layer, se[s], 0, NJ + sjw[s])),
        ],
        out_specs=pl.BlockSpec((RB, TN_E), lambda s, se, sjw, sr, sjo, kd: (sr[s], sjo[s])),
        scratch_shapes=[pltpu.VMEM((D, TN_E), BF16), pltpu.VMEM((D, TN_E), BF16)],
    )
    return pl.pallas_call(
        _gmm_up_kernel,
        grid_spec=grid_spec,
        out_shape=_sds((ROWS, DFF), BF16),
        compiler_params=_cparams(1, VMEM_BIG),
        name="gmm_up",
    )(*steps, xs, w_gu, w_gu, b_gu, b_gu)


def _gmm_down_kernel(se, sjw, sr, sjo, kind, x_ref, w_ref, b_ref, o_ref, w_s):
    s = pl.program_id(0)

    @pl.when(kind[s] == STEP_LOAD)
    def _():
        w_s[...] = w_ref[0, 0].astype(BF16)

    @pl.when(kind[s] != STEP_FILL)
    def _():
        o_ref[...] = _dot(x_ref[...], w_s[...]) + b_ref[0, 0]

    @pl.when(kind[s] == STEP_FILL)
    def _():
        o_ref[...] = jnp.zeros_like(o_ref)


def _gmm_down(steps, act, w_down, b_down, layer):
    grid_spec = pltpu.PrefetchScalarGridSpec(
        num_scalar_prefetch=5,
        grid=(S_MAX,),
        in_specs=[
            pl.BlockSpec((RB, DFF), lambda s, se, sjw, sr, sjo, kd: (sr[s], 0)),
            pl.BlockSpec((1, 1, DFF, TN_E), lambda s, se, sjw, sr, sjo, kd: (layer, se[s], 0, sjw[s])),
            pl.BlockSpec((1, 1, 1, TN_E), lambda s, se, sjw, sr, sjo, kd: (layer, se[s], 0, sjw[s])),
        ],
        out_specs=pl.BlockSpec((RB, TN_E), lambda s, se, sjw, sr, sjo, kd: (sr[s], sjo[s])),
        scratch_shapes=[pltpu.VMEM((DFF, TN_E), BF16)],
    )
    return pl.pallas_call(
        _gmm_down_kernel,
        grid_spec=grid_spec,
        out_shape=_sds((ROWS, D), F32),
        compiler_params=_cparams(1, VMEM_BIG),
        name="gmm_down",
    )(*steps, act, w_down, b_down)


def _combine_kernel(dest_ref, rows_hbm, tw_ref, x_ref, gt_ref, o_ref, buf, sem):
    def body(i, carry):
        for k in range(TOPK):
            d = dest_ref[0, 0, i * TOPK + k]
            pltpu.make_async_copy(rows_hbm.at[pl.ds(d, 1)], buf.at[k, pl.ds(i, 1)], sem).start()
        return carry

    lax.fori_loop(0, TOK_TILE, body, 0)
    for k in range(TOPK):
        pltpu.make_async_copy(rows_hbm.at[pl.ds(0, TOK_TILE)], buf.at[k], sem).wait()
    tw = tw_ref[...]
    y = tw[:, 0:1] * buf[0]
    for k in range(1, TOPK):
        y = y + tw[:, k:k + 1] * buf[k]
    o_ref[...] = x_ref[...] + gt_ref[0, 0] * y


def _combine(dest3, out_rows, tw, x1, mod):
    m_rows = x1.shape[0]
    tiles = m_rows // TOK_TILE
    return pl.pallas_call(
        _combine_kernel,
        grid=(tiles,),
        in_specs=[
            pl.BlockSpec((1, 1, TOK_TILE * TOPK), lambda m: (m, 0, 0), memory_space=pltpu.SMEM),
            pl.BlockSpec(memory_space=pl.ANY),
            pl.BlockSpec((TOK_TILE, TOPK), lambda m: (m, 0)),
            pl.BlockSpec((TOK_TILE, D), lambda m: (m, 0)),
            _mod_spec(mod, 5, tiles),
        ],
        out_specs=pl.BlockSpec((TOK_TILE, D), lambda m: (m, 0)),
        out_shape=_sds((m_rows, D), F32),
        scratch_shapes=[pltpu.VMEM((TOPK, TOK_TILE, D), F32), pltpu.SemaphoreType.DMA(())],
        compiler_params=_cparams(1),
        name="combine",
    )(dest3, out_rows, tw, x1, mod)


def _routing_tables(te, rank, cnt):
    counts = cnt[0, :NE].astype(I32)
    nblk = (counts + RB - 1) // RB
    bstart = jnp.cumsum(nblk) - nblk
    dest = (bstart * RB)[te] + rank
    nsteps = nblk * NJ
    send = jnp.cumsum(nsteps)
    total = send[-1]
    step = jnp.arange(S_MAX, dtype=I32)
    real = step < total
    s = jnp.minimum(step, total - 1)
    e = jnp.minimum(jnp.searchsorted(send, s, side="right").astype(I32), NE - 1)
    within = s - (send - nsteps)[e]
    nb = jnp.maximum(nblk[e], 1)
    jw = (within // nb).astype(I32)
    spare = step - total
    r = jnp.where(real, bstart[e] + within % nb, jnp.sum(nblk) + spare // NJ).astype(I32)
    jo = jnp.where(real, jw, spare % NJ).astype(I32)
    kind = jnp.where(real, jnp.where((within % nb) == 0, STEP_LOAD, STEP_RUN), STEP_FILL)
    return dest, (e, jw, r, jo, kind.astype(I32))


def _moe(layer, h2_all, te_all, tw_all, w_gu, b_gu, w_down, b_down):
    rank, cnt = _rank(te_all)
    dest, steps = _routing_tables(te_all, rank, cnt)
    dest3 = dest.reshape(N_TT, 1, TOK_TILE * TOPK)
    xs = _scatter_rows(dest3, h2_all, jnp.zeros((ROWS, D), F32))
    act = _gmm_up(steps, xs, w_gu, b_gu.reshape(DEPTH, NE, 1, 2 * DFF), layer)
    out_rows = _gmm_down(steps, act, w_down, b_down.reshape(DEPTH, NE, 1, D), layer)
    return dest3, out_rows


def _layer(l, xp, xs, st, w, mod_p, mod_s):
    cast = lambda a: a.astype(BF16)
    w_in = w["w_in"][l]
    w_main = cast(jnp.concatenate(
        [w_in[:, 1536:4608], w_in[:, 6672:12816], w_in[:, 4624:5648], w_in[:, 5648:6672],
         w_in[:, 0:1024], w_in[:, 1024:1280], w_in[:, 1280:1536]], axis=1))
    w_ab = cast(jnp.pad(w_in[:, 4608:4624], ((0, 0), (0, 112))))
    wb = cast(w["w_branch"][l])
    w_out = cast(w["w_out"][l])
    wr = cast(jnp.pad(w["w_router"][l], ((0, 0), (0, 128 - NE))))
    br = jnp.pad(w["b_router"][l], (0, 128 - NE)).reshape(1, 128)
    g1 = w["norm1_g"][l].reshape(1, D)
    g2 = w["norm2_g"][l].reshape(1, D)
    qg = w["q_norm_g"][l].reshape(1, HD)
    kg = w["k_norm_g"][l].reshape(1, HD)
    sinks = w["sinks"][l]
    dn_cw = w["dn_conv_w"][l]
    alog_row = jnp.pad(w["dn_a_log"][l], (0, 128 - DNH)).reshape(1, 128)
    dtb_row = jnp.pad(w["dn_dt_bias"][l], (0, 128 - DNH)).reshape(1, 128)
    dn_ng = w["dn_norm_g"][l].reshape(1, DK)
    l_cw = w["lru_conv_w"][l]
    l_cb = w["lru_conv_b"][l].reshape(1, LW)
    l_wr = cast(w["lru_w_r"][l])
    l_wi = cast(w["lru_w_i"][l])
    l_br = w["lru_b_r"][l].reshape(1, LW)
    l_bi = w["lru_b_i"][l].reshape(1, LW)
    l_lam = w["lru_lambda"][l].reshape(1, LW)

    h1p = _adaln(xp, g1, mod_p, 0, 1, 256)
    pp = _matmul(h1p, w_main, F32, 1024, 512, "proj")
    pab = _matmul(h1p, w_ab, F32, 1024, 128, "proj_ab")
    oa_p, k_p, v_p = _attn_prompt(pp, qg, kg, sinks)
    ob_p, d_p, dc_p = _delta_prompt(pp, pab, dn_cw, alog_row, dtb_row, dn_ng)
    oc_p, h_p, hc_p = _lru_prompt(pp, l_cw, l_cb, l_wr, l_br, l_wi, l_bi, l_lam)
    mg_p = _merge(oa_p, ob_p, oc_p, wb, pp, 1024)
    x1p = _outproj(mg_p, w_out, xp, mod_p, 1024)
    h2p, te_p, tw_p = _router(x1p, g2, mod_p, wr, br, 256)

    ck, cv, s0, dconv0, h0, lconv0 = st
    h1s = _adaln(xs, g1, mod_s, 0, 1, MS)
    ps = _matmul(h1s, w_main, F32, MS, 512, "proj")
    psab = _matmul(h1s, w_ab, F32, MS, 128, "proj_ab")
    q_r = ps[:, P_Q:P_K].reshape(DEC_BATCH, DEC_SEQ, NKV, GRP, HD).transpose(0, 2, 3, 1, 4)
    q_r = q_r.reshape(DEC_BATCH, NKV, GRP * DEC_SEQ, HD)
    pad_t = ((0, 0), (0, 0), (0, 8 - DEC_SEQ), (0, 0))
    kn_r = jnp.pad(ps[:, P_K:P_V].reshape(DEC_BATCH, DEC_SEQ, NKV, HD).transpose(0, 2, 1, 3), pad_t)
    vn_raw = ps[:, P_V:P_W].reshape(DEC_BATCH, DEC_SEQ, NKV, HD)
    vn_r = jnp.pad(vn_raw.transpose(0, 2, 1, 3), pad_t)
    oa_r, kno = _attn_sample(q_r, kn_r, vn_r, ck.transpose(0, 2, 1, 3), cv.transpose(0, 2, 1, 3),
                             qg, kg, sinks)
    oa_s = oa_r.reshape(DEC_BATCH, NKV, GRP, DEC_SEQ, HD).transpose(0, 3, 1, 2, 4)
    oa_s = oa_s.reshape(MS, 1024).astype(BF16)
    k_s = jnp.concatenate([ck[:, DEC_SEQ:], kno[:, :, :DEC_SEQ].transpose(0, 2, 1, 3)], axis=1)
    v_s = jnp.concatenate([cv[:, DEC_SEQ:], vn_raw], axis=1)

    x_dq = ps[:, P_DQKV:P_DQKV + CONV_CH].reshape(DEC_BATCH, DEC_SEQ, CONV_CH)
    xp8 = jnp.concatenate([jnp.zeros((DEC_BATCH, 1, CONV_CH), F32), dconv0, x_dq], axis=1)
    pad8 = ((0, 0), (0, 8 - DEC_SEQ), (0, 0))
    dz8 = jnp.pad(ps[:, P_DZ:P_LX].reshape(DEC_BATCH, DEC_SEQ, 1024), pad8)
    dab8 = jnp.pad(psab.reshape(DEC_BATCH, DEC_SEQ, 128), pad8)
    ob_r, d_s = _delta_sample(xp8, dz8, dab8, s0, dn_cw, alog_row, dtb_row, dn_ng)
    ob_s = ob_r[:, :DEC_SEQ].reshape(MS, 1024).astype(BF16)
    dc_s = x_dq[:, DEC_SEQ - (CONV_W - 1):]

    x_lx = ps[:, P_LX:P_Q].reshape(DEC_BATCH, DEC_SEQ, LW)
    oc_tm = _lru_sample(x_lx.transpose(1, 0, 2), lconv0.transpose(1, 0, 2), h0, l_cw, l_cb,
                        l_wr, l_br, l_wi, l_bi, l_lam)
    oc_s = oc_tm.transpose(1, 0, 2).reshape(MS, LW).astype(BF16)
    h_s = oc_tm[DEC_SEQ - 1]
    hc_s = x_lx[:, DEC_SEQ - (CONV_W - 1):]

    mg_s = _merge(oa_s, ob_s, oc_s, wb, ps, MS)
    x1s = _outproj(mg_s, w_out, xs, mod_s, MS)
    h2s, te_s, tw_s = _router(x1s, g2, mod_s, wr, br, MS)

    h2_all = jnp.concatenate([h2p, h2s], axis=0)
    te_all = jnp.concatenate([te_p, te_s], axis=0)
    dest3, out_rows = _moe(l, h2_all, te_all, None, w["w_gate_up"], w["b_gate_up"],
                           w["w_down"], w["b_down"])
    np_t = MP // TOK_TILE
    xp_new = _combine(dest3[:np_t], out_rows, tw_p, x1p, mod_p)
    xs_new = _combine(dest3[np_t:], out_rows, tw_s, x1s, mod_s)

    st_p = (k_p.reshape(BATCH, WINDOW, NKV, HD), v_p.reshape(BATCH, WINDOW, NKV, HD), d_p, dc_p,
            h_p.reshape(BATCH, LW), hc_p)
    st_s = (k_s, v_s, d_s, dc_s, h_s, hc_s)
    return xp_new, xs_new, st_p, st_s


def kernel(x_prompt, x_sample, cache_k, cache_v, state_delta, state_delta_conv, state_lru, state_lru_conv, c_prompt, c_sample, w_ada, b_ada, norm1_g, norm2_g, w_in, q_norm_g, k_norm_g, sinks, dn_conv_w, dn_a_log, dn_dt_bias, dn_norm_g, lru_conv_w, lru_conv_b, lru_w_r, lru_b_r, lru_w_i, lru_b_i, lru_lambda, w_branch, w_out, w_router, b_router, w_gate_up, b_gate_up, w_down, b_down):
    w = dict(w_in=w_in, w_branch=w_branch, w_out=w_out, w_router=w_router, b_router=b_router,
             norm1_g=norm1_g, norm2_g=norm2_g, q_norm_g=q_norm_g, k_norm_g=k_norm_g, sinks=sinks,
             dn_conv_w=dn_conv_w, dn_a_log=dn_a_log, dn_dt_bias=dn_dt_bias, dn_norm_g=dn_norm_g,
             lru_conv_w=lru_conv_w, lru_conv_b=lru_conv_b, lru_w_r=lru_w_r, lru_b_r=lru_b_r,
             lru_w_i=lru_w_i, lru_b_i=lru_b_i, lru_lambda=lru_lambda, w_gate_up=w_gate_up,
             b_gate_up=b_gate_up, w_down=w_down, b_down=b_down)
    n_c = BATCH + DEC_BATCH
    c_all = jnp.concatenate([c_prompt, c_sample, jnp.zeros((40 - n_c, D), F32)], axis=0)
    mod_all = _ada_mod(c_all, w_ada, b_ada)

    xp = x_prompt.reshape(MP, D)
    xs = x_sample.reshape(MS, D)
    new_p, new_s = [], []
    for l in range(DEPTH):
        mod_p = mod_all[l, :BATCH].reshape(BATCH, 6, 1, D).transpose(1, 0, 2, 3)
        mod_s = jnp.repeat(mod_all[l, BATCH:n_c].reshape(DEC_BATCH, 6, D), DEC_SEQ, axis=0)
        mod_s = mod_s.transpose(1, 0, 2).reshape(6, 1, MS, D)
        st = (cache_k[l], cache_v[l], state_delta[l], state_delta_conv[l], state_lru[l],
              state_lru_conv[l])
        xp, xs, st_p, st_s = _layer(l, xp, xs, st, w, mod_p, mod_s)
        new_p.append(st_p)
        new_s.append(st_s)
    k_p, v_p, d_p, dc_p, h_p, hc_p = (jnp.stack(z) for z in zip(*new_p))
    k_s, v_s, d_s, dc_s, h_s, hc_s = (jnp.stack(z) for z in zip(*new_s))
    return (xp.reshape(BATCH, SEQ, D), xs.reshape(DEC_BATCH, DEC_SEQ, D),
            k_p, v_p, d_p, dc_p, h_p, hc_p, k_s, v_s, d_s, dc_s, h_s, hc_s)
```

```python
import functools

import jax
import jax.numpy as jnp
from jax import lax
from jax.experimental import pallas as pl
from jax.experimental.pallas import tpu as pltpu

F32 = jnp.float32
BF16 = jnp.bfloat16
I32 = jnp.int32

D = 2048
BATCH = 2
SEQ = 4096
DEC_BATCH = 32
DEC_SEQ = 4
DEPTH = 2
MP = BATCH * SEQ
MS = DEC_BATCH * DEC_SEQ
NTOK = MP + MS

WINDOW = 128
HD = 64
NH = 16
NKV = 4
GRP = NH // NKV
ATT_SCALE = HD ** -0.5

DK = 128
DNH = 8
CHUNK = 64
CONV_CH = 3 * DNH * DK
CONV_W = 4

LW = 1024
LBLK = 128
LNB = LW // LBLK
LRU_C = 8.0

NE = 32
TOPK = 4
DFF = 2048
SW_ALPHA = 1.702
SW_LIMIT = 7.0
EPS = 1e-6

P_DQKV = 0
P_G = 3072
P_DZ = 9216
P_LX = 10240
P_Q = 11264
P_K = 12288
P_V = 12544
P_W = 12800

RB = 256
NB_MAX = -(-(NTOK * TOPK) // RB) + NE
ROWS = NB_MAX * RB
TN_E = 1024
NJ = DFF // TN_E
S_MAX = NB_MAX * NJ
TOK_TILE = 128
N_TT = NTOK // TOK_TILE

VMEM_BIG = 56 * 1024 * 1024


def _sds(shape, dtype):
    return jax.ShapeDtypeStruct(shape, dtype)


def _cparams(n_axes, vmem=None):
    return pltpu.CompilerParams(dimension_semantics=("arbitrary",) * n_axes, vmem_limit_bytes=vmem)


def _dot(a, b):
    return jnp.dot(a, b, preferred_element_type=F32)


def _dot_nt(a, b):
    return lax.dot_general(a, b, (((1,), (1,)), ((), ())), preferred_element_type=F32)


def _dot_tn(a, b):
    return lax.dot_general(a, b, (((0,), (0,)), ((), ())), preferred_element_type=F32)


def _rms(x, g):
    return x * lax.rsqrt(jnp.mean(x * x, axis=-1, keepdims=True) + EPS) * g


def _sigmoid(x):
    return 1.0 / (1.0 + jnp.exp(-x))


def _silu(x):
    return x * _sigmoid(x)


def _softplus(x):
    return jnp.maximum(x, 0.0) + jnp.log1p(jnp.exp(-jnp.abs(x)))


def _ada_kernel(c_ref, w_ref, b_ref, o_ref):
    a = _silu(c_ref[...]).astype(BF16)
    o_ref[0] = _dot(a, w_ref[0].astype(BF16)) + b_ref[0]


def _ada_mod(c_all, w_ada, b_ada):
    rows = c_all.shape[0]
    tn = 1024
    return pl.pallas_call(
        _ada_kernel,
        grid=(DEPTH, 6 * D // tn),
        in_specs=[
            pl.BlockSpec((rows, D), lambda l, n: (0, 0)),
            pl.BlockSpec((1, D, tn), lambda l, n: (l, 0, n)),
            pl.BlockSpec((1, 1, tn), lambda l, n: (l, 0, n)),
        ],
        out_specs=pl.BlockSpec((1, rows, tn), lambda l, n: (l, 0, n)),
        out_shape=_sds((DEPTH, rows, 6 * D), F32),
        compiler_params=_cparams(2, VMEM_BIG),
        name="ada_mod",
    )(c_all, w_ada, b_ada.reshape(DEPTH, 1, 6 * D))


def _adaln_kernel(x_ref, g_ref, sh_ref, sc_ref, o_ref):
    y = _rms(x_ref[...], g_ref[...])
    o_ref[...] = (y * (1.0 + sc_ref[0, 0]) + sh_ref[0, 0]).astype(o_ref.dtype)


def _mod_spec(mod, j, tiles, tn=D, with_n=False):
    g, rb = mod.shape[1], mod.shape[2]
    tpg = tiles // g
    if with_n:
        return pl.BlockSpec((1, 1, rb, tn), lambda n, m: (j, m // tpg, 0, n))
    return pl.BlockSpec((1, 1, rb, tn), lambda m: (j, m // tpg, 0, 0))


def _adaln(x, g, mod, j_sh, j_sc, tm):
    m_rows = x.shape[0]
    tiles = m_rows // tm
    return pl.pallas_call(
        _adaln_kernel,
        grid=(tiles,),
        in_specs=[
            pl.BlockSpec((tm, D), lambda m: (m, 0)),
            pl.BlockSpec((1, D), lambda m: (0, 0)),
            _mod_spec(mod, j_sh, tiles),
            _mod_spec(mod, j_sc, tiles),
        ],
        out_specs=pl.BlockSpec((tm, D), lambda m: (m, 0)),
        out_shape=_sds((m_rows, D), BF16),
        compiler_params=_cparams(1),
        name="adaln1",
    )(x, g, mod, mod)


def _mm_kernel(x_ref, w_ref, o_ref):
    o_ref[...] = _dot(x_ref[...], w_ref[...]).astype(o_ref.dtype)


def _matmul(x, w, out_dtype, tm, tn, name):
    m_rows, k = x.shape
    n_cols = w.shape[1]
    return pl.pallas_call(
        _mm_kernel,
        grid=(n_cols // tn, m_rows // tm),
        in_specs=[
            pl.BlockSpec((tm, k), lambda n, m: (m, 0)),
            pl.BlockSpec((k, tn), lambda n, m: (0, n)),
        ],
        out_specs=pl.BlockSpec((tm, tn), lambda n, m: (m, n)),
        out_shape=_sds((m_rows, n_cols), out_dtype),
        compiler_params=_cparams(2, VMEM_BIG),
        name=name,
    )(x, w)


def _attn_prompt_kernel(sinks_ref, q_ref, kc_ref, vc_ref, kp_ref, vp_ref, qg_ref, kg_ref,
                        o_ref, ko_ref, vo_ref):
    nblk = SEQ // WINDOW
    first = (pl.program_id(0) % nblk) == 0
    row = lax.broadcasted_iota(I32, (WINDOW, 2 * WINDOW), 0)
    col = lax.broadcasted_iota(I32, (WINDOW, 2 * WINDOW), 1)
    lo = jnp.where(first, WINDOW, 0)
    mask = (col > row) & (col <= row + WINDOW) & (col >= lo)
    qg = qg_ref[...]
    kg = kg_ref[...]
    outs = []
    k_out = []
    for kh in range(NKV):
        ks = slice(kh * HD, (kh + 1) * HD)
        kc_n = _rms(kc_ref[:, ks], kg)
        kp_n = _rms(kp_ref[:, ks], kg)
        k_out.append(kc_n)
        kk = jnp.concatenate([kp_n, kc_n], axis=0).astype(BF16)
        vv = jnp.concatenate([vp_ref[:, ks], vc_ref[:, ks]], axis=0).astype(BF16)
        for g in range(GRP):
            h = kh * GRP + g
            qh = _rms(q_ref[:, h * HD:(h + 1) * HD], qg).astype(BF16)
            s = _dot_nt(qh, kk) * ATT_SCALE
            s = jnp.where(mask, s, -jnp.inf)
            sink = sinks_ref[h]
            m = jnp.maximum(jnp.max(s, axis=-1, keepdims=True), sink)
            p = jnp.exp(s - m)
            den = jnp.sum(p, axis=-1, keepdims=True) + jnp.exp(sink - m)
            outs.append(_dot(p.astype(BF16), vv) / den)
    o_ref[...] = jnp.concatenate(outs, axis=-1).astype(o_ref.dtype)
    ko_ref[0] = jnp.concatenate(k_out, axis=-1)
    vo_ref[0] = vc_ref[...]


def _attn_prompt(p_act, qg, kg, sinks):
    nblk = SEQ // WINDOW
    cq, ck, cv = P_Q // 1024, P_K // 256, P_V // 256
    return pl.pallas_call(
        _attn_prompt_kernel,
        grid=(MP // WINDOW,),
        in_specs=[
            pl.BlockSpec(memory_space=pltpu.SMEM),
            pl.BlockSpec((WINDOW, 1024), lambda g: (g, cq)),
            pl.BlockSpec((WINDOW, 256), lambda g: (g, ck)),
            pl.BlockSpec((WINDOW, 256), lambda g: (g, cv)),
            pl.BlockSpec((WINDOW, 256), lambda g: (jnp.maximum(g - 1, 0), ck)),
            pl.BlockSpec((WINDOW, 256), lambda g: (jnp.maximum(g - 1, 0), cv)),
            pl.BlockSpec((1, HD), lambda g: (0, 0)),
            pl.BlockSpec((1, HD), lambda g: (0, 0)),
        ],
        out_specs=[
            pl.BlockSpec((WINDOW, 1024), lambda g: (g, 0)),
            pl.BlockSpec((1, WINDOW, 256), lambda g: (g // nblk, 0, 0)),
            pl.BlockSpec((1, WINDOW, 256), lambda g: (g // nblk, 0, 0)),
        ],
        out_shape=[_sds((MP, 1024), BF16), _sds((BATCH, WINDOW, 256), F32),
                   _sds((BATCH, WINDOW, 256), F32)],
        compiler_params=_cparams(1),
        name="attn_prompt",
    )(sinks, p_act, p_act, p_act, p_act, p_act, qg, kg)


def _attn_sample_kernel(sinks_ref, q_ref, kn_ref, vn_ref, ck_ref, cv_ref, qg_ref, kg_ref,
                        o_ref, kno_ref):
    rows = GRP * DEC_SEQ
    t = lax.broadcasted_iota(I32, (rows, 1), 0) % DEC_SEQ
    g_of_row = lax.broadcasted_iota(I32, (rows, 1), 0) // DEC_SEQ
    col_c = lax.broadcasted_iota(I32, (rows, WINDOW), 1)
    col_n = lax.broadcasted_iota(I32, (rows, 8), 1)
    qg = qg_ref[...]
    kg = kg_ref[...]
    for kh in range(NKV):
        q16 = _rms(q_ref[0, kh], qg).astype(BF16)
        kn = _rms(kn_ref[0, kh], kg)
        kno_ref[0, kh] = kn
        s_c = _dot_nt(q16, ck_ref[0, kh].astype(BF16)) * ATT_SCALE
        s_n = _dot_nt(q16, kn.astype(BF16)) * ATT_SCALE
        s_c = jnp.where(col_c > t, s_c, -jnp.inf)
        s_n = jnp.where(col_n <= t, s_n, -jnp.inf)
        sink = jnp.zeros((rows, 1), F32)
        for g in range(GRP):
            sink = jnp.where(g_of_row == g, sinks_ref[kh * GRP + g], sink)
        m = jnp.maximum(jnp.maximum(jnp.max(s_c, axis=-1, keepdims=True),
                                    jnp.max(s_n, axis=-1, keepdims=True)), sink)
        p_c = jnp.exp(s_c - m)
        p_n = jnp.exp(s_n - m)
        den = (jnp.sum(p_c, axis=-1, keepdims=True) + jnp.sum(p_n, axis=-1, keepdims=True)
               + jnp.exp(sink - m))
        o = _dot(p_c.astype(BF16), cv_ref[0, kh].astype(BF16)) + _dot(
            p_n.astype(BF16), vn_ref[0, kh].astype(BF16))
        o_ref[0, kh] = o / den


def _attn_sample(q_r, kn_r, vn_r, ck_r, cv_r, qg, kg, sinks):
    rows = GRP * DEC_SEQ
    return pl.pallas_call(
        _attn_sample_kernel,
        grid=(DEC_BATCH,),
        in_specs=[
            pl.BlockSpec(memory_space=pltpu.SMEM),
            pl.BlockSpec((1, NKV, rows, HD), lambda b: (b, 0, 0, 0)),
            pl.BlockSpec((1, NKV, 8, HD), lambda b: (b, 0, 0, 0)),
            pl.BlockSpec((1, NKV, 8, HD), lambda b: (b, 0, 0, 0)),
            pl.BlockSpec((1, NKV, WINDOW, HD), lambda b: (b, 0, 0, 0)),
            pl.BlockSpec((1, NKV, WINDOW, HD), lambda b: (b, 0, 0, 0)),
            pl.BlockSpec((1, HD), lambda b: (0, 0)),
            pl.BlockSpec((1, HD), lambda b: (0, 0)),
        ],
        out_specs=[
            pl.BlockSpec((1, NKV, rows, HD), lambda b: (b, 0, 0, 0)),
            pl.BlockSpec((1, NKV, 8, HD), lambda b: (b, 0, 0, 0)),
        ],
        out_shape=[_sds((DEC_BATCH, NKV, rows, HD), F32), _sds((DEC_BATCH, NKV, 8, HD), F32)],
        compiler_params=_cparams(1),
        name="attn_sample",
    )(sinks, q_r, kn_r, vn_r, ck_r, cv_r, qg, kg)


def _cumsum_rows(x, c):
    row = lax.broadcasted_iota(I32, x.shape, 0)
    s = 1
    while s < c:
        x = x + jnp.where(row >= s, pltpu.roll(x, s, 0), 0.0)
        s *= 2
    return x


def _delta_chunk(c, nstack, conv_slice, g_full, beta_full, z_slice, norm_g, s_get, s_put, o_put):
    n = nstack * c
    lg = c.bit_length() - 1
    r = lax.broadcasted_iota(I32, (n, n), 0)
    cc = lax.broadcasted_iota(I32, (n, n), 1)
    same = (r >> lg) == (cc >> lg)
    incl = same & (r >= cc)
    strict = same & (r > cc)
    eye_b = r == cc
    eye = eye_b.astype(F32)
    gc = _cumsum_rows(g_full, c)
    for st in range(DNH // nstack):
        heads = range(st * nstack, (st + 1) * nstack)
        qs, ks, vs, gcs, bs = [], [], [], [], []
        for h in heads:
            qh = conv_slice(slice(h * DK, (h + 1) * DK))
            kh = conv_slice(slice(DNH * DK + h * DK, DNH * DK + (h + 1) * DK))
            vs.append(conv_slice(slice(2 * DNH * DK + h * DK, 2 * DNH * DK + (h + 1) * DK)))
            qs.append(qh * lax.rsqrt(jnp.sum(qh * qh, axis=-1, keepdims=True) + EPS) * (DK ** -0.5))
            ks.append(kh * lax.rsqrt(jnp.sum(kh * kh, axis=-1, keepdims=True) + EPS))
            gcs.append(gc[:, h:h + 1])
            bs.append(beta_full[:, DNH + h:DNH + h + 1])
        q = jnp.concatenate(qs, axis=0)
        k = jnp.concatenate(ks, axis=0)
        v = jnp.concatenate(vs, axis=0)
        gcol = jnp.concatenate(gcs, axis=0)
        bcol = jnp.concatenate(bs, axis=0)
        grow = jnp.sum(jnp.where(eye_b, gcol, 0.0), axis=0, keepdims=True)
        decay = jnp.exp(jnp.where(incl, gcol - grow, -jnp.inf))
        egc = jnp.exp(gcol)
        kb = k * bcol
        k16 = k.astype(BF16)
        a_mat = jnp.where(strict, _dot_nt(kb.astype(BF16), k16) * decay, 0.0)
        blk = 1
        t_inv = eye
        while blk < c:
            sh = blk.bit_length()
            pair = ((r >> sh) == (cc >> sh)) & ((r & blk) != 0) & ((cc & blk) == 0)
            off = jnp.where(pair, a_mat, 0.0)
            if blk == 1:
                t_inv = t_inv - off
            else:
                t16 = t_inv.astype(BF16)
                t_inv = t_inv - _dot(t16, _dot(off.astype(BF16), t16).astype(BF16))
            blk *= 2
        rhs = jnp.concatenate([v * bcol, kb * egc], axis=-1).astype(BF16)
        sol = _dot(t_inv.astype(BF16), rhs)
        qk16 = (_dot_nt(q.astype(BF16), k16) * decay).astype(BF16)
        q_dec = q * egc
        us, s_olds, g_lasts = [], [], []
        for i, h in enumerate(heads):
            hs = slice(i * c, (i + 1) * c)
            s_old = s_get(h)
            us.append(sol[hs, :DK] - _dot(sol[hs, DK:].astype(BF16), s_old.astype(BF16)))
            s_olds.append(s_old)
            g_lasts.append(gcol[(i + 1) * c - 1:(i + 1) * c, :])
        u16 = jnp.concatenate(us, axis=0).astype(BF16)
        o_in = _dot(qk16, u16)
        for i, h in enumerate(heads):
            hs = slice(i * c, (i + 1) * c)
            s16 = s_olds[i].astype(BF16)
            o = _dot(q_dec[hs].astype(BF16), s16) + o_in[hs]
            k_dec = k[hs] * jnp.exp(g_lasts[i] - gcol[hs])
            s_put(h, s_olds[i] * jnp.exp(g_lasts[i]) + _dot_tn(k_dec.astype(BF16), u16[hs]))
            zz = z_slice(slice(h * DK, (h + 1) * DK))
            o_put(h, _rms(o, norm_g) * _silu(zz))


def _delta_prompt_kernel(x_ref, prev_ref, dz_ref, dab_ref, cw_ref, alog_ref, dtb_ref, ng_ref,
                         o_ref, s_ref, dc_ref):
    c = pl.program_id(1)

    @pl.when(c == 0)
    def _():
        s_ref[...] = jnp.zeros_like(s_ref)

    row8 = lax.broadcasted_iota(I32, (8, DK), 0)
    keep_prev = c > 0

    def conv_slice(cs):
        x = x_ref[:, cs]
        prev = jnp.where(keep_prev, prev_ref[:, cs], 0.0)
        y = x * cw_ref[CONV_W - 1:CONV_W, cs]
        for s in range(1, CONV_W):
            xr = pltpu.roll(x, s, 0)
            top = jnp.where(row8 < s, pltpu.roll(prev, s, 0), xr[0:8])
            sh = jnp.concatenate([top, xr[8:]], axis=0)
            y = y + sh * cw_ref[CONV_W - 1 - s:CONV_W - s, cs]
        return _silu(y)

    dab = dab_ref[...]
    g_full = -jnp.exp(alog_ref[...]) * _softplus(dab + dtb_ref[...])
    beta_full = _sigmoid(dab)

    def s_get(h):
        return s_ref[0, h]

    def s_put(h, v):
        s_ref[0, h] = v

    def o_put(h, v):
        o_ref[:, h * DK:(h + 1) * DK] = v.astype(o_ref.dtype)

    _delta_chunk(CHUNK, 4, conv_slice, g_full, beta_full, lambda cs: dz_ref[:, cs], ng_ref[...],
                 s_get, s_put, o_put)

    @pl.when(c == pl.num_programs(1) - 1)
    def _():
        dc_ref[0] = x_ref[CHUNK - (CONV_W - 1):CHUNK, :]


def _delta_prompt(p_act, p_ab, conv_w, alog_row, dtb_row, norm_g):
    nck = SEQ // CHUNK
    return pl.pallas_call(
        _delta_prompt_kernel,
        grid=(BATCH, nck),
        in_specs=[
            pl.BlockSpec((CHUNK, CONV_CH), lambda b, c: (b * nck + c, P_DQKV // CONV_CH)),
            pl.BlockSpec((8, CONV_CH),
                         lambda b, c: (jnp.maximum(b * (SEQ // 8) + c * (CHUNK // 8) - 1, 0), 0)),
            pl.BlockSpec((CHUNK, 1024), lambda b, c: (b * nck + c, P_DZ // 1024)),
            pl.BlockSpec((CHUNK, 128), lambda b, c: (b * nck + c, 0)),
            pl.BlockSpec((CONV_W, CONV_CH), lambda b, c: (0, 0)),
            pl.BlockSpec((1, 128), lambda b, c: (0, 0)),
            pl.BlockSpec((1, 128), lambda b, c: (0, 0)),
            pl.BlockSpec((1, DK), lambda b, c: (0, 0)),
        ],
        out_specs=[
            pl.BlockSpec((CHUNK, 1024), lambda b, c: (b * nck + c, 0)),
            pl.BlockSpec((1, DNH, DK, DK), lambda b, c: (b, 0, 0, 0)),
            pl.BlockSpec((1, CONV_W - 1, CONV_CH), lambda b, c: (b, 0, 0)),
        ],
        out_shape=[_sds((MP, 1024), BF16), _sds((BATCH, DNH, DK, DK), F32),
                   _sds((BATCH, CONV_W - 1, CONV_CH), F32)],
        compiler_params=_cparams(2),
        name="delta_prompt",
    )(p_act, p_act, p_act, p_ab, conv_w, alog_row, dtb_row, norm_g)


def _delta_sample_kernel(xp_ref, dz_ref, dab_ref, s0_ref, cw_ref, alog_ref, dtb_ref, ng_ref,
                         o_ref, s_ref):
    row = lax.broadcasted_iota(I32, (8, DK), 0)
    live = row < DEC_SEQ

    def conv_slice(cs):
        xp = xp_ref[0, :, cs]
        y = jnp.zeros((8, DK), F32)
        for j in range(CONV_W):
            y = y + pltpu.roll(xp, 8 - 1 - j, 0) * cw_ref[j:j + 1, cs]
        return jnp.where(live, _silu(y), 0.0)

    dab = dab_ref[0]
    g_full = jnp.where(live, -jnp.exp(alog_ref[...]) * _softplus(dab + dtb_ref[...]), 0.0)
    beta_full = jnp.where(live, _sigmoid(dab), 0.0)

    def s_put(h, v):
        s_ref[0, h] = v

    def o_put(h, v):
        o_ref[0, :, h * DK:(h + 1) * DK] = v

    _delta_chunk(8, DNH, conv_slice, g_full, beta_full, lambda cs: dz_ref[0, :, cs], ng_ref[...],
                 lambda h: s0_ref[0, h], s_put, o_put)


def _delta_sample(xp8, dz8, dab8, s0, conv_w, alog_row, dtb_row, norm_g):
    return pl.pallas_call(
        _delta_sample_kernel,
        grid=(DEC_BATCH,),
        in_specs=[
            pl.BlockSpec((1, 8, CONV_CH), lambda b: (b, 0, 0)),
            pl.BlockSpec((1, 8, 1024), lambda b: (b, 0, 0)),
            pl.BlockSpec((1, 8, 128), lambda b: (b, 0, 0)),
            pl.BlockSpec((1, DNH, DK, DK), lambda b: (b, 0, 0, 0)),
            pl.BlockSpec((CONV_W, CONV_CH), lambda b: (0, 0)),
            pl.BlockSpec((1, 128), lambda b: (0, 0)),
            pl.BlockSpec((1, 128), lambda b: (0, 0)),
            pl.BlockSpec((1, DK), lambda b: (0, 0)),
        ],
        out_specs=[
            pl.BlockSpec((1, 8, 1024), lambda b: (b, 0, 0)),
            pl.BlockSpec((1, DNH, DK, DK), lambda b: (b, 0, 0, 0)),
        ],
        out_shape=[_sds((DEC_BATCH, 8, 1024), F32), _sds((DEC_BATCH, DNH, DK, DK), F32)],
        compiler_params=_cparams(1),
        name="delta_sample",
    )(xp8, dz8, dab8, s0, conv_w, alog_row, dtb_row, norm_g)


def _lru_gates(xc, wr, br, wi, bi, sp):
    x16 = xc.astype(BF16)
    r = _sigmoid(_dot(x16, wr) + br)
    i = _sigmoid(_dot(x16, wi) + bi)
    log_a = -LRU_C * r * sp
    a = jnp.exp(log_a)
    th = jnp.tanh(log_a)
    u = jnp.sqrt(-2.0 * th / (1.0 - th)) * (i * xc)
    return a, u


LRU_TT = 256


def _lru_prompt_kernel(x_ref, prev_ref, cw_ref, cb_ref, wr_ref, br_ref, wi_ref, bi_ref, lam_ref,
                       o_ref, h_ref, lc_ref):
    t = pl.program_id(1)
    row8 = lax.broadcasted_iota(I32, (8, LBLK), 0)
    row = lax.broadcasted_iota(I32, (LRU_TT, LBLK), 0)
    keep = t > 0
    for n in range(LNB):
        cs = slice(n * LBLK, (n + 1) * LBLK)
        x = x_ref[:, cs]
        prev = jnp.where(keep, prev_ref[:, cs], 0.0)
        y = x * cw_ref[CONV_W - 1:CONV_W, cs]
        for s in range(1, CONV_W):
            xr = pltpu.roll(x, s, 0)
            top = jnp.where(row8 < s, pltpu.roll(prev, s, 0), xr[0:8])
            y = y + jnp.concatenate([top, xr[8:]], axis=0) * cw_ref[CONV_W - 1 - s:CONV_W - s, cs]
        xc = y + cb_ref[:, cs]
        sp = _softplus(-lam_ref[:, cs])
        a, u = _lru_gates(xc, wr_ref[n], br_ref[:, cs], wi_ref[n], bi_ref[:, cs], sp)
        h0 = jnp.where(keep, h_ref[0, :, cs], 0.0)
        u = u + jnp.where(row == 0, a * h0, 0.0)
        s = 1
        while s < LRU_TT:
            valid = row >= s
            u_s = pltpu.roll(u, s, 0)
            a_s = pltpu.roll(a, s, 0)
            u = jnp.where(valid, a * u_s + u, u)
            a = jnp.where(valid, a * a_s, a)
            s *= 2
        o_ref[:, cs] = u.astype(o_ref.dtype)
        h_ref[0, :, cs] = u[LRU_TT - 1:LRU_TT, :]

    @pl.when(t == pl.num_programs(1) - 1)
    def _():
        lc_ref[0] = x_ref[LRU_TT - (CONV_W - 1):LRU_TT, :]


def _lru_prompt(p_act, cw, cb, wr, br, wi, bi, lam):
    ntt = SEQ // LRU_TT
    cl = P_LX // LW
    vec = pl.BlockSpec((1, LW), lambda b, t: (0, 0))
    mat = pl.BlockSpec((LNB, LBLK, LBLK), lambda b, t: (0, 0, 0))
    return pl.pallas_call(
        _lru_prompt_kernel,
        grid=(BATCH, ntt),
        in_specs=[
            pl.BlockSpec((LRU_TT, LW), lambda b, t: (b * ntt + t, cl)),
            pl.BlockSpec((8, LW),
                         lambda b, t: (jnp.maximum(b * (SEQ // 8) + t * (LRU_TT // 8) - 1, 0), cl)),
            pl.BlockSpec((CONV_W, LW), lambda b, t: (0, 0)),
            vec, mat, vec, mat, vec, vec,
        ],
        out_specs=[
            pl.BlockSpec((LRU_TT, LW), lambda b, t: (b * ntt + t, 0)),
            pl.BlockSpec((1, 1, LW), lambda b, t: (b, 0, 0)),
            pl.BlockSpec((1, CONV_W - 1, LW), lambda b, t: (b, 0, 0)),
        ],
        out_shape=[_sds((MP, LW), BF16), _sds((BATCH, 1, LW), F32),
                   _sds((BATCH, CONV_W - 1, LW), F32)],
        compiler_params=_cparams(2),
        name="lru_prompt",
    )(p_act, p_act, cw, cb, wr, br, wi, bi, lam)


def _lru_sample_kernel(x_ref, buf_ref, h0_ref, cw_ref, cb_ref, wr_ref, br_ref, wi_ref, bi_ref,
                       lam_ref, o_ref):
    for n in range(LNB):
        cs = slice(n * LBLK, (n + 1) * LBLK)
        xp = [buf_ref[j, :, cs] for j in range(CONV_W - 1)] + [x_ref[t, :, cs] for t in range(DEC_SEQ)]
        xcs = []
        for t in range(DEC_SEQ):
            y = xp[t] * cw_ref[0:1, cs]
            for j in range(1, CONV_W):
                y = y + xp[t + j] * cw_ref[j:j + 1, cs]
            xcs.append(y + cb_ref[:, cs])
        xc = jnp.concatenate(xcs, axis=0)
        sp = _softplus(-lam_ref[:, cs])
        a, u = _lru_gates(xc, wr_ref[n], br_ref[:, cs], wi_ref[n], bi_ref[:, cs], sp)
        h = h0_ref[:, cs]
        for t in range(DEC_SEQ):
            rs = slice(t * DEC_BATCH, (t + 1) * DEC_BATCH)
            h = a[rs] * h + u[rs]
            o_ref[t, :, cs] = h


def _lru_sample(x_tm, buf_tm, h0, cw, cb, wr, br, wi, bi, lam):
    return pl.pallas_call(
        _lru_sample_kernel,
        out_shape=_sds((DEC_SEQ, DEC_BATCH, LW), F32),
        name="lru_sample",
    )(x_tm, buf_tm, h0, cw, cb, wr, br, wi, bi, lam)


def _merge_kernel(oa_ref, ob_ref, oc_ref, w_ref, ga_ref, gb_ref, gc_ref, o_ref):
    acc = _sigmoid(ga_ref[...]) * _dot(oa_ref[...], w_ref[0])
    acc = acc + _sigmoid(gb_ref[...]) * _dot(ob_ref[...], w_ref[1])
    acc = acc + _sigmoid(gc_ref[...]) * _dot(oc_ref[...], w_ref[2])
    o_ref[...] = acc.astype(o_ref.dtype)


def _merge(o_a, o_b, o_c, wb, p_act, tm):
    m_rows = o_a.shape[0]
    tn = 512
    g0 = P_G // tn
    gs = D // tn
    br = pl.BlockSpec((tm, 1024), lambda n, m: (m, 0))
    return pl.pallas_call(
        _merge_kernel,
        grid=(D // tn, m_rows // tm),
        in_specs=[
            br, br, br,
            pl.BlockSpec((3, 1024, tn), lambda n, m: (0, 0, n)),
            pl.BlockSpec((tm, tn), lambda n, m: (m, g0 + n)),
            pl.BlockSpec((tm, tn), lambda n, m: (m, g0 + gs + n)),
            pl.BlockSpec((tm, tn), lambda n, m: (m, g0 + 2 * gs + n)),
        ],
        out_specs=pl.BlockSpec((tm, tn), lambda n, m: (m, n)),
        out_shape=_sds((m_rows, D), BF16),
        compiler_params=_cparams(2, VMEM_BIG),
        name="merge",
    )(o_a, o_b, o_c, wb, p_act, p_act, p_act)


def _outproj_kernel(a_ref, w_ref, x_ref, gt_ref, o_ref):
    o_ref[...] = x_ref[...] + gt_ref[0, 0] * _dot(a_ref[...], w_ref[...])


def _outproj(merged, w_out, x, mod, tm):
    m_rows = x.shape[0]
    tn = 512
    tiles = m_rows // tm
    return pl.pallas_call(
        _outproj_kernel,
        grid=(D // tn, tiles),
        in_specs=[
            pl.BlockSpec((tm, D), lambda n, m: (m, 0)),
            pl.BlockSpec((D, tn), lambda n, m: (0, n)),
            pl.BlockSpec((tm, tn), lambda n, m: (m, n)),
            _mod_spec(mod, 2, tiles, tn, with_n=True),
        ],
        out_specs=pl.BlockSpec((tm, tn), lambda n, m: (m, n)),
        out_shape=_sds((m_rows, D), F32),
        compiler_params=_cparams(2, VMEM_BIG),
        name="outproj",
    )(merged, w_out, x, mod)


def _router_kernel(x_ref, g_ref, sh_ref, sc_ref, wr_ref, br_ref, h_ref, te_ref, tw_ref):
    tm = x_ref.shape[0]
    h = _rms(x_ref[...], g_ref[...]) * (1.0 + sc_ref[0, 0]) + sh_ref[0, 0]
    h_ref[...] = h
    lane = lax.broadcasted_iota(I32, (tm, 128), 1)
    logits = _dot(h.astype(BF16), wr_ref[...]) + br_ref[...]
    logits = jnp.where(lane < NE, logits, -jnp.inf)
    lane4 = lax.broadcasted_iota(I32, (tm, TOPK), 1)
    te = jnp.zeros((tm, TOPK), I32)
    tl = jnp.zeros((tm, TOPK), F32)
    for k in range(TOPK):
        m = jnp.max(logits, axis=-1, keepdims=True)
        idx = jnp.min(jnp.where(logits == m, lane, 128), axis=-1, keepdims=True)
        te = jnp.where(lane4 == k, idx, te)
        tl = jnp.where(lane4 == k, m, tl)
        logits = jnp.where(lane == idx, -jnp.inf, logits)
    e = jnp.exp(tl - tl[:, 0:1])
    te_ref[...] = te
    tw_ref[...] = e / jnp.sum(e, axis=-1, keepdims=True)


def _router(x1, g, mod, wr, br, tm):
    m_rows = x1.shape[0]
    tiles = m_rows // tm
    return pl.pallas_call(
        _router_kernel,
        grid=(tiles,),
        in_specs=[
            pl.BlockSpec((tm, D), lambda m: (m, 0)),
            pl.BlockSpec((1, D), lambda m: (0, 0)),
            _mod_spec(mod, 3, tiles),
            _mod_spec(mod, 4, tiles),
            pl.BlockSpec((D, 128), lambda m: (0, 0)),
            pl.BlockSpec((1, 128), lambda m: (0, 0)),
        ],
        out_specs=[
            pl.BlockSpec((tm, D), lambda m: (m, 0)),
            pl.BlockSpec((tm, TOPK), lambda m: (m, 0)),
            pl.BlockSpec((tm, TOPK), lambda m: (m, 0)),
        ],
        out_shape=[_sds((m_rows, D), F32), _sds((m_rows, TOPK), I32), _sds((m_rows, TOPK), F32)],
        compiler_params=_cparams(1),
        name="router",
    )(x1, g, mod, mod, wr, br)


def _rank_kernel(te_ref, rk_ref, cnt_ref):
    @pl.when(pl.program_id(0) == 0)
    def _():
        cnt_ref[...] = jnp.zeros_like(cnt_ref)

    tt = TOK_TILE
    lane = lax.broadcasted_iota(I32, (tt, 128), 1)
    te = te_ref[...]
    hot = jnp.zeros((tt, 128), F32)
    for k in range(TOPK):
        hot = hot + (lane == te[:, k:k + 1]).astype(F32)
    r = lax.broadcasted_iota(I32, (tt, tt), 0)
    c = lax.broadcasted_iota(I32, (tt, tt), 1)
    before = (r > c).astype(BF16)
    tot = _dot(before, hot.astype(BF16)) + cnt_ref[...]
    lane4 = lax.broadcasted_iota(I32, (tt, TOPK), 1)
    rk = jnp.zeros((tt, TOPK), F32)
    for k in range(TOPK):
        rk_k = jnp.sum(jnp.where(lane == te[:, k:k + 1], tot, 0.0), axis=-1, keepdims=True)
        rk = jnp.where(lane4 == k, rk_k, rk)
    rk_ref[...] = rk.astype(I32)
    cnt_ref[...] = cnt_ref[...] + jnp.sum(hot, axis=0, keepdims=True)


def _rank(te):
    return pl.pallas_call(
        _rank_kernel,
        grid=(N_TT,),
        in_specs=[pl.BlockSpec((TOK_TILE, TOPK), lambda i: (i, 0))],
        out_specs=[pl.BlockSpec((TOK_TILE, TOPK), lambda i: (i, 0)),
                   pl.BlockSpec((1, 128), lambda i: (0, 0))],
        out_shape=[_sds((NTOK, TOPK), I32), _sds((1, 128), F32)],
        compiler_params=_cparams(1),
        name="rank",
    )(te)


def _scatter_kernel(dest_ref, last_ref, h_ref, xs_out, zbuf, zsem, sem):
    @pl.when(pl.program_id(0) == 0)
    def _():
        zbuf[...] = jnp.zeros_like(zbuf)

        def zero_block(row):
            return pltpu.make_async_copy(zbuf, xs_out.at[pl.ds(row, RB)], zsem)

        def extra_row(i):
            return pl.multiple_of(last_ref[NE] + i * RB, RB)

        n_extra = last_ref[NE + 1]
        for e in range(NE):
            zero_block(pl.multiple_of(last_ref[e], RB)).start()
        lax.fori_loop(0, n_extra, lambda i, c: (zero_block(extra_row(i)).start(), c)[1], 0)
        for e in range(NE):
            zero_block(pl.multiple_of(last_ref[e], RB)).wait()
        lax.fori_loop(0, n_extra, lambda i, c: (zero_block(extra_row(i)).wait(), c)[1], 0)

    def body(i, carry):
        for k in range(TOPK):
            d = dest_ref[0, 0, i * TOPK + k]
            pltpu.make_async_copy(h_ref.at[pl.ds(i, 1)], xs_out.at[pl.ds(d, 1)], sem).start()
        return carry

    lax.fori_loop(0, TOK_TILE, body, 0)
    for k in range(TOPK):
        pltpu.make_async_copy(h_ref, xs_out.at[pl.ds(0, TOK_TILE)], sem).wait()


def _scatter_rows(dest3, last_rows, h2):
    return pl.pallas_call(
        _scatter_kernel,
        grid=(N_TT,),
        in_specs=[
            pl.BlockSpec((1, 1, TOK_TILE * TOPK), lambda i: (i, 0, 0), memory_space=pltpu.SMEM),
            pl.BlockSpec(memory_space=pltpu.SMEM),
            pl.BlockSpec((TOK_TILE, D), lambda i: (i, 0)),
        ],
        out_specs=pl.BlockSpec(memory_space=pl.ANY),
        out_shape=_sds((ROWS, D), F32),
        scratch_shapes=[pltpu.VMEM((RB, D), F32), pltpu.SemaphoreType.DMA(()),
                        pltpu.SemaphoreType.DMA(())],
        compiler_params=_cparams(1),
        name="scatter_rows",
    )(dest3, last_rows, h2)


STEP_FILL = 0
STEP_RUN = 1
STEP_LOAD = 2


def _gmm_up_kernel(se, sjw, srx, sro, sjo, kind, x_ref, wg_ref, wl_ref, bg_ref, bl_ref, o_ref,
                   wg_s, wl_s):
    s = pl.program_id(0)

    @pl.when(kind[s] == STEP_LOAD)
    def _():
        wg_s[...] = wg_ref[0, 0].astype(BF16)
        wl_s[...] = wl_ref[0, 0].astype(BF16)

    @pl.when(kind[s] != STEP_FILL)
    def _():
        x = x_ref[...].astype(BF16)
        glu = _dot(x, wg_s[...]) + bg_ref[0, 0]
        lin = _dot(x, wl_s[...]) + bl_ref[0, 0]
        glu = jnp.minimum(glu, SW_LIMIT)
        lin = jnp.clip(lin, -SW_LIMIT, SW_LIMIT)
        o_ref[...] = (glu * _sigmoid(SW_ALPHA * glu) * (lin + 1.0)).astype(o_ref.dtype)

    @pl.when(kind[s] == STEP_FILL)
    def _():
        o_ref[...] = jnp.zeros_like(o_ref)


def _gmm_up(steps, xs, w_gu, b_gu, layer):
    grid_spec = pltpu.PrefetchScalarGridSpec(
        num_scalar_prefetch=6,
        grid=(S_MAX,),
        in_specs=[
            pl.BlockSpec((RB, D), lambda s, se, sjw, srx, sro, sjo, kd: (srx[s], 0)),
            pl.BlockSpec((1, 1, D, TN_E), lambda s, se, sjw, srx, sro, sjo, kd: (layer, se[s], 0, sjw[s])),
            pl.BlockSpec((1, 1, D, TN_E),
                         lambda s, se, sjw, srx, sro, sjo, kd: (layer, se[s], 0, NJ + sjw[s])),
            pl.BlockSpec((1, 1, 1, TN_E), lambda s, se, sjw, srx, sro, sjo, kd: (layer, se[s], 0, sjw[s])),
            pl.BlockSpec((1, 1, 1, TN_E),
                         lambda s, se, sjw, srx, sro, sjo, kd: (layer, se[s], 0, NJ + sjw[s])),
        ],
        out_specs=pl.BlockSpec((RB, TN_E), lambda s, se, sjw, srx, sro, sjo, kd: (sro[s], sjo[s])),
        scratch_shapes=[pltpu.VMEM((D, TN_E), BF16), pltpu.VMEM((D, TN_E), BF16)],
    )
    return pl.pallas_call(
        _gmm_up_kernel,
        grid_spec=grid_spec,
        out_shape=_sds((ROWS, DFF), BF16),
        compiler_params=_cparams(1, VMEM_BIG),
        name="gmm_up",
    )(*steps, xs, w_gu, w_gu, b_gu, b_gu)


def _gmm_down_kernel(se, sjw, srx, sro, sjo, kind, x_ref, w_ref, b_ref, o_ref, w_s):
    s = pl.program_id(0)

    @pl.when(kind[s] == STEP_LOAD)
    def _():
        w_s[...] = w_ref[0, 0].astype(BF16)

    @pl.when(kind[s] != STEP_FILL)
    def _():
        o_ref[...] = _dot(x_ref[...], w_s[...]) + b_ref[0, 0]

    @pl.when(kind[s] == STEP_FILL)
    def _():
        o_ref[...] = jnp.zeros_like(o_ref)


def _gmm_down(steps, act, w_down, b_down, layer):
    grid_spec = pltpu.PrefetchScalarGridSpec(
        num_scalar_prefetch=6,
        grid=(S_MAX,),
        in_specs=[
            pl.BlockSpec((RB, DFF), lambda s, se, sjw, srx, sro, sjo, kd: (srx[s], 0)),
            pl.BlockSpec((1, 1, DFF, TN_E), lambda s, se, sjw, srx, sro, sjo, kd: (layer, se[s], 0, sjw[s])),
            pl.BlockSpec((1, 1, 1, TN_E), lambda s, se, sjw, srx, sro, sjo, kd: (layer, se[s], 0, sjw[s])),
        ],
        out_specs=pl.BlockSpec((RB, TN_E), lambda s, se, sjw, srx, sro, sjo, kd: (sro[s], sjo[s])),
        scratch_shapes=[pltpu.VMEM((DFF, TN_E), BF16)],
    )
    return pl.pallas_call(
        _gmm_down_kernel,
        grid_spec=grid_spec,
        out_shape=_sds((ROWS, D), F32),
        compiler_params=_cparams(1, VMEM_BIG),
        name="gmm_down",
    )(*steps, act, w_down, b_down)


def _combine_kernel(dest_ref, rows_hbm, tw_ref, x_ref, gt_ref, o_ref, buf, sem):
    def body(i, carry):
        for k in range(TOPK):
            d = dest_ref[0, 0, i * TOPK + k]
            pltpu.make_async_copy(rows_hbm.at[pl.ds(d, 1)], buf.at[k, pl.ds(i, 1)], sem).start()
        return carry

    lax.fori_loop(0, TOK_TILE, body, 0)
    for k in range(TOPK):
        pltpu.make_async_copy(rows_hbm.at[pl.ds(0, TOK_TILE)], buf.at[k], sem).wait()
    tw = tw_ref[...]
    y = tw[:, 0:1] * buf[0]
    for k in range(1, TOPK):
        y = y + tw[:, k:k + 1] * buf[k]
    o_ref[...] = x_ref[...] + gt_ref[0, 0] * y


def _combine(dest3, out_rows, tw, x1, mod):
    m_rows = x1.shape[0]
    tiles = m_rows // TOK_TILE
    return pl.pallas_call(
        _combine_kernel,
        grid=(tiles,),
        in_specs=[
            pl.BlockSpec((1, 1, TOK_TILE * TOPK), lambda m: (m, 0, 0), memory_space=pltpu.SMEM),
            pl.BlockSpec(memory_space=pl.ANY),
            pl.BlockSpec((TOK_TILE, TOPK), lambda m: (m, 0)),
            pl.BlockSpec((TOK_TILE, D), lambda m: (m, 0)),
            _mod_spec(mod, 5, tiles),
        ],
        out_specs=pl.BlockSpec((TOK_TILE, D), lambda m: (m, 0)),
        out_shape=_sds((m_rows, D), F32),
        scratch_shapes=[pltpu.VMEM((TOPK, TOK_TILE, D), F32), pltpu.SemaphoreType.DMA(())],
        compiler_params=_cparams(1),
        name="combine",
    )(dest3, out_rows, tw, x1, mod)


def _routing_tables(te, rank, cnt):
    counts = cnt[0, :NE].astype(I32)
    nblk = (counts + RB - 1) // RB
    bstart = jnp.cumsum(nblk) - nblk
    dest = (bstart * RB)[te] + rank
    nsteps = nblk * NJ
    send = jnp.cumsum(nsteps)
    total = send[-1]
    step = jnp.arange(S_MAX, dtype=I32)
    real = step < total
    s = jnp.minimum(step, total - 1)
    e = jnp.minimum(jnp.sum((s[:, None] >= send[None, :]).astype(I32), axis=1), NE - 1)
    within = s - (send - nsteps)[e]
    nb = jnp.maximum(nblk[e], 1)
    jw = (within // nb).astype(I32)
    spare = step - total
    used = jnp.sum(nblk)
    rx = (bstart[e] + within % nb).astype(I32)
    ro = jnp.where(real, rx, used + spare // NJ).astype(I32)
    jo = jnp.where(real, jw, spare % NJ).astype(I32)
    kind = jnp.where(real, jnp.where((within % nb) == 0, STEP_LOAD, STEP_RUN), STEP_FILL)
    empty = nblk == 0
    n_empty = jnp.sum(empty.astype(I32))
    last_blk = jnp.where(empty, used + jnp.cumsum(empty.astype(I32)) - 1, bstart + nblk - 1)
    extra0 = used + n_empty
    last_rows = jnp.concatenate([last_blk * RB, jnp.stack([extra0 * RB, NB_MAX - extra0])])
    return dest, (e, jw, rx, ro, jo, kind.astype(I32)), last_rows.astype(I32)


def _moe(layer, h2_all, te_all, w_gu, b_gu, w_down, b_down):
    rank, cnt = _rank(te_all)
    dest, steps, last_rows = _routing_tables(te_all, rank, cnt)
    dest3 = dest.reshape(N_TT, 1, TOK_TILE * TOPK)
    xs = _scatter_rows(dest3, last_rows, h2_all)
    act = _gmm_up(steps, xs, w_gu, b_gu.reshape(DEPTH, NE, 1, 2 * DFF), layer)
    out_rows = _gmm_down(steps, act, w_down, b_down.reshape(DEPTH, NE, 1, D), layer)
    return dest3, out_rows


def _layer(l, xp, xs, st, w, mod_p, mod_s):
    cast = lambda a: a.astype(BF16)
    w_in = w["w_in"][l]
    w_main = cast(jnp.concatenate(
        [w_in[:, 1536:4608], w_in[:, 6672:12816], w_in[:, 4624:5648], w_in[:, 5648:6672],
         w_in[:, 0:1024], w_in[:, 1024:1280], w_in[:, 1280:1536]], axis=1))
    w_ab = cast(jnp.pad(w_in[:, 4608:4624], ((0, 0), (0, 112))))
    wb = cast(w["w_branch"][l])
    w_out = cast(w["w_out"][l])
    wr = cast(jnp.pad(w["w_router"][l], ((0, 0), (0, 128 - NE))))
    br = jnp.pad(w["b_router"][l], (0, 128 - NE)).reshape(1, 128)
    g1 = w["norm1_g"][l].reshape(1, D)
    g2 = w["norm2_g"][l].reshape(1, D)
    qg = w["q_norm_g"][l].reshape(1, HD)
    kg = w["k_norm_g"][l].reshape(1, HD)
    sinks = w["sinks"][l]
    dn_cw = w["dn_conv_w"][l]
    alog_row = jnp.pad(w["dn_a_log"][l], (0, 128 - DNH)).reshape(1, 128)
    dtb_row = jnp.pad(w["dn_dt_bias"][l], (0, 128 - DNH)).reshape(1, 128)
    dn_ng = w["dn_norm_g"][l].reshape(1, DK)
    l_cw = w["lru_conv_w"][l]
    l_cb = w["lru_conv_b"][l].reshape(1, LW)
    l_wr = cast(w["lru_w_r"][l])
    l_wi = cast(w["lru_w_i"][l])
    l_br = w["lru_b_r"][l].reshape(1, LW)
    l_bi = w["lru_b_i"][l].reshape(1, LW)
    l_lam = w["lru_lambda"][l].reshape(1, LW)

    h1p = _adaln(xp, g1, mod_p, 0, 1, 256)
    pp = _matmul(h1p, w_main, F32, 1024, 512, "proj")
    pab = _matmul(h1p, w_ab, F32, 1024, 128, "proj_ab")
    oa_p, k_p, v_p = _attn_prompt(pp, qg, kg, sinks)
    ob_p, d_p, dc_p = _delta_prompt(pp, pab, dn_cw, alog_row, dtb_row, dn_ng)
    oc_p, h_p, hc_p = _lru_prompt(pp, l_cw, l_cb, l_wr, l_br, l_wi, l_bi, l_lam)
    mg_p = _merge(oa_p, ob_p, oc_p, wb, pp, 1024)
    x1p = _outproj(mg_p, w_out, xp, mod_p, 1024)
    h2p, te_p, tw_p = _router(x1p, g2, mod_p, wr, br, 256)

    ck, cv, s0, dconv0, h0, lconv0 = st
    h1s = _adaln(xs, g1, mod_s, 0, 1, MS)
    ps = _matmul(h1s, w_main, F32, MS, 512, "proj")
    psab = _matmul(h1s, w_ab, F32, MS, 128, "proj_ab")
    q_r = ps[:, P_Q:P_K].reshape(DEC_BATCH, DEC_SEQ, NKV, GRP, HD).transpose(0, 2, 3, 1, 4)
    q_r = q_r.reshape(DEC_BATCH, NKV, GRP * DEC_SEQ, HD)
    pad_t = ((0, 0), (0, 0), (0, 8 - DEC_SEQ), (0, 0))
    kn_r = jnp.pad(ps[:, P_K:P_V].reshape(DEC_BATCH, DEC_SEQ, NKV, HD).transpose(0, 2, 1, 3), pad_t)
    vn_raw = ps[:, P_V:P_W].reshape(DEC_BATCH, DEC_SEQ, NKV, HD)
    vn_r = jnp.pad(vn_raw.transpose(0, 2, 1, 3), pad_t)
    oa_r, kno = _attn_sample(q_r, kn_r, vn_r, ck.transpose(0, 2, 1, 3), cv.transpose(0, 2, 1, 3),
                             qg, kg, sinks)
    oa_s = oa_r.reshape(DEC_BATCH, NKV, GRP, DEC_SEQ, HD).transpose(0, 3, 1, 2, 4)
    oa_s = oa_s.reshape(MS, 1024).astype(BF16)
    k_s = jnp.concatenate([ck[:, DEC_SEQ:], kno[:, :, :DEC_SEQ].transpose(0, 2, 1, 3)], axis=1)
    v_s = jnp.concatenate([cv[:, DEC_SEQ:], vn_raw], axis=1)

    x_dq = ps[:, P_DQKV:P_DQKV + CONV_CH].reshape(DEC_BATCH, DEC_SEQ, CONV_CH)
    xp8 = jnp.concatenate([jnp.zeros((DEC_BATCH, 1, CONV_CH), F32), dconv0, x_dq], axis=1)
    pad8 = ((0, 0), (0, 8 - DEC_SEQ), (0, 0))
    dz8 = jnp.pad(ps[:, P_DZ:P_LX].reshape(DEC_BATCH, DEC_SEQ, 1024), pad8)
    dab8 = jnp.pad(psab.reshape(DEC_BATCH, DEC_SEQ, 128), pad8)
    ob_r, d_s = _delta_sample(xp8, dz8, dab8, s0, dn_cw, alog_row, dtb_row, dn_ng)
    ob_s = ob_r[:, :DEC_SEQ].reshape(MS, 1024).astype(BF16)
    dc_s = x_dq[:, DEC_SEQ - (CONV_W - 1):]

    x_lx = ps[:, P_LX:P_Q].reshape(DEC_BATCH, DEC_SEQ, LW)
    oc_tm = _lru_sample(x_lx.transpose(1, 0, 2), lconv0.transpose(1, 0, 2), h0, l_cw, l_cb,
                        l_wr, l_br, l_wi, l_bi, l_lam)
    oc_s = oc_tm.transpose(1, 0, 2).reshape(MS, LW).astype(BF16)
    h_s = oc_tm[DEC_SEQ - 1]
    hc_s = x_lx[:, DEC_SEQ - (CONV_W - 1):]

    mg_s = _merge(oa_s, ob_s, oc_s, wb, ps, MS)
    x1s = _outproj(mg_s, w_out, xs, mod_s, MS)
    h2s, te_s, tw_s = _router(x1s, g2, mod_s, wr, br, MS)

    h2_all = jnp.concatenate([h2p, h2s], axis=0)
    te_all = jnp.concatenate([te_p, te_s], axis=0)
    dest3, out_rows = _moe(l, h2_all, te_all, w["w_gate_up"], w["b_gate_up"],
                           w["w_down"], w["b_down"])
    np_t = MP // TOK_TILE
    xp_new = _combine(dest3[:np_t], out_rows, tw_p, x1p, mod_p)
    xs_new = _combine(dest3[np_t:], out_rows, tw_s, x1s, mod_s)

    st_p = (k_p.reshape(BATCH, WINDOW, NKV, HD), v_p.reshape(BATCH, WINDOW, NKV, HD), d_p, dc_p,
            h_p.reshape(BATCH, LW), hc_p)
    st_s = (k_s, v_s, d_s, dc_s, h_s, hc_s)
    return xp_new, xs_new, st_p, st_s


def kernel(x_prompt, x_sample, cache_k, cache_v, state_delta, state_delta_conv, state_lru, state_lru_conv, c_prompt, c_sample, w_ada, b_ada, norm1_g, norm2_g, w_in, q_norm_g, k_norm_g, sinks, dn_conv_w, dn_a_log, dn_dt_bias, dn_norm_g, lru_conv_w, lru_conv_b, lru_w_r, lru_b_r, lru_w_i, lru_b_i, lru_lambda, w_branch, w_out, w_router, b_router, w_gate_up, b_gate_up, w_down, b_down):
    w = dict(w_in=w_in, w_branch=w_branch, w_out=w_out, w_router=w_router, b_router=b_router,
             norm1_g=norm1_g, norm2_g=norm2_g, q_norm_g=q_norm_g, k_norm_g=k_norm_g, sinks=sinks,
             dn_conv_w=dn_conv_w, dn_a_log=dn_a_log, dn_dt_bias=dn_dt_bias, dn_norm_g=dn_norm_g,
             lru_conv_w=lru_conv_w, lru_conv_b=lru_conv_b, lru_w_r=lru_w_r, lru_b_r=lru_b_r,
             lru_w_i=lru_w_i, lru_b_i=lru_b_i, lru_lambda=lru_lambda, w_gate_up=w_gate_up,
             b_gate_up=b_gate_up, w_down=w_down, b_down=b_down)
    n_c = BATCH + DEC_BATCH
    c_all = jnp.concatenate([c_prompt, c_sample, jnp.zeros((40 - n_c, D), F32)], axis=0)
    mod_all = _ada_mod(c_all, w_ada, b_ada)

    xp = x_prompt.reshape(MP, D)
    xs = x_sample.reshape(MS, D)
    new_p, new_s = [], []
    for l in range(DEPTH):
        mod_p = mod_all[l, :BATCH].reshape(BATCH, 6, 1, D).transpose(1, 0, 2, 3)
        mod_s = jnp.repeat(mod_all[l, BATCH:n_c].reshape(DEC_BATCH, 6, D), DEC_SEQ, axis=0)
        mod_s = mod_s.transpose(1, 0, 2).reshape(6, 1, MS, D)
        st = (cache_k[l], cache_v[l], state_delta[l], state_delta_conv[l], state_lru[l],
              state_lru_conv[l])
        xp, xs, st_p, st_s = _layer(l, xp, xs, st, w, mod_p, mod_s)
        new_p.append(st_p)
        new_s.append(st_s)
    k_p, v_p, d_p, dc_p, h_p, hc_p = (jnp.stack(z) for z in zip(*new_p))
    k_s, v_s, d_s, dc_s, h_s, hc_s = (jnp.stack(z) for z in zip(*new_s))
    return (xp.reshape(BATCH, SEQ, D), xs.reshape(DEC_BATCH, DEC_SEQ, D),
            k_p, v_p, d_p, dc_p, h_p, hc_p, k_s, v_s, d_s, dc_s, h_s, hc_s)
```

```python
import functools

import jax
import jax.numpy as jnp
from jax import lax
from jax.experimental import pallas as pl
from jax.experimental.pallas import tpu as pltpu

F32 = jnp.float32
BF16 = jnp.bfloat16
I32 = jnp.int32

D = 2048
BATCH = 2
SEQ = 4096
DEC_BATCH = 32
DEC_SEQ = 4
DEPTH = 2
MP = BATCH * SEQ
MS = DEC_BATCH * DEC_SEQ
NTOK = MP + MS

WINDOW = 128
HD = 64
NH = 16
NKV = 4
GRP = NH // NKV
ATT_SCALE = HD ** -0.5

DK = 128
DNH = 8
CHUNK = 64
CONV_CH = 3 * DNH * DK
CONV_W = 4

LW = 1024
LBLK = 128
LNB = LW // LBLK
LRU_C = 8.0

NE = 32
TOPK = 4
DFF = 2048
SW_ALPHA = 1.702
SW_LIMIT = 7.0
EPS = 1e-6

P_DQKV = 0
P_G = 3072
P_DZ = 9216
P_LX = 10240
P_Q = 11264
P_K = 12288
P_V = 12544
P_W = 12800

RB = 256
NB_MAX = -(-(NTOK * TOPK) // RB) + NE
ROWS = NB_MAX * RB
TN_E = 1024
NJ = DFF // TN_E
TOK_TILE = 128
N_TT = NTOK // TOK_TILE

VMEM_BIG = 56 * 1024 * 1024


def _sds(shape, dtype):
    return jax.ShapeDtypeStruct(shape, dtype)


def _cparams(n_axes, vmem=None):
    return pltpu.CompilerParams(dimension_semantics=("arbitrary",) * n_axes, vmem_limit_bytes=vmem)


def _dot(a, b):
    return jnp.dot(a, b, preferred_element_type=F32)


def _dot_nt(a, b):
    return lax.dot_general(a, b, (((1,), (1,)), ((), ())), preferred_element_type=F32)


def _dot_tn(a, b):
    return lax.dot_general(a, b, (((0,), (0,)), ((), ())), preferred_element_type=F32)


def _rms(x, g):
    return x * lax.rsqrt(jnp.mean(x * x, axis=-1, keepdims=True) + EPS) * g


def _sigmoid(x):
    return 1.0 / (1.0 + jnp.exp(-x))


def _silu(x):
    return x * _sigmoid(x)


def _softplus(x):
    return jnp.maximum(x, 0.0) + jnp.log1p(jnp.exp(-jnp.abs(x)))


def _ada_kernel(c_ref, w_ref, b_ref, o_ref):
    a = _silu(c_ref[...]).astype(BF16)
    o_ref[0] = _dot(a, w_ref[0].astype(BF16)) + b_ref[0]


def _ada_mod(c_all, w_ada, b_ada):
    rows = c_all.shape[0]
    tn = 1024
    return pl.pallas_call(
        _ada_kernel,
        grid=(DEPTH, 6 * D // tn),
        in_specs=[
            pl.BlockSpec((rows, D), lambda l, n: (0, 0)),
            pl.BlockSpec((1, D, tn), lambda l, n: (l, 0, n)),
            pl.BlockSpec((1, 1, tn), lambda l, n: (l, 0, n)),
        ],
        out_specs=pl.BlockSpec((1, rows, tn), lambda l, n: (l, 0, n)),
        out_shape=_sds((DEPTH, rows, 6 * D), F32),
        compiler_params=_cparams(2, VMEM_BIG),
        name="ada_mod",
    )(c_all, w_ada, b_ada.reshape(DEPTH, 1, 6 * D))


def _adaln_kernel(x_ref, g_ref, sh_ref, sc_ref, o_ref):
    y = _rms(x_ref[...], g_ref[...])
    o_ref[...] = (y * (1.0 + sc_ref[0, 0]) + sh_ref[0, 0]).astype(o_ref.dtype)


def _mod_spec(mod, j, tiles, tn=D, with_n=False):
    g, rb = mod.shape[1], mod.shape[2]
    tpg = tiles // g
    if with_n:
        return pl.BlockSpec((1, 1, rb, tn), lambda n, m: (j, m // tpg, 0, n))
    return pl.BlockSpec((1, 1, rb, tn), lambda m: (j, m // tpg, 0, 0))


def _adaln(x, g, mod, j_sh, j_sc, tm):
    m_rows = x.shape[0]
    tiles = m_rows // tm
    return pl.pallas_call(
        _adaln_kernel,
        grid=(tiles,),
        in_specs=[
            pl.BlockSpec((tm, D), lambda m: (m, 0)),
            pl.BlockSpec((1, D), lambda m: (0, 0)),
            _mod_spec(mod, j_sh, tiles),
            _mod_spec(mod, j_sc, tiles),
        ],
        out_specs=pl.BlockSpec((tm, D), lambda m: (m, 0)),
        out_shape=_sds((m_rows, D), BF16),
        compiler_params=_cparams(1),
        name="adaln1",
    )(x, g, mod, mod)


def _mm_kernel(x_ref, w_ref, o_ref):
    o_ref[...] = _dot(x_ref[...], w_ref[...]).astype(o_ref.dtype)


def _matmul(x, w, out_dtype, tm, tn, name):
    m_rows, k = x.shape
    n_cols = w.shape[1]
    return pl.pallas_call(
        _mm_kernel,
        grid=(n_cols // tn, m_rows // tm),
        in_specs=[
            pl.BlockSpec((tm, k), lambda n, m: (m, 0)),
            pl.BlockSpec((k, tn), lambda n, m: (0, n)),
        ],
        out_specs=pl.BlockSpec((tm, tn), lambda n, m: (m, n)),
        out_shape=_sds((m_rows, n_cols), out_dtype),
        compiler_params=_cparams(2, VMEM_BIG),
        name=name,
    )(x, w)


def _attn_prompt_kernel(sinks_ref, q_ref, kc_ref, vc_ref, kp_ref, vp_ref, qg_ref, kg_ref,
                        o_ref, ko_ref, vo_ref):
    nblk = SEQ // WINDOW
    first = (pl.program_id(0) % nblk) == 0
    row = lax.broadcasted_iota(I32, (WINDOW, 2 * WINDOW), 0)
    col = lax.broadcasted_iota(I32, (WINDOW, 2 * WINDOW), 1)
    lo = jnp.where(first, WINDOW, 0)
    mask = (col > row) & (col <= row + WINDOW) & (col >= lo)
    qg = qg_ref[...]
    kg = kg_ref[...]
    outs = []
    k_out = []
    for kh in range(NKV):
        ks = slice(kh * HD, (kh + 1) * HD)
        kc_n = _rms(kc_ref[:, ks], kg)
        kp_n = _rms(kp_ref[:, ks], kg)
        k_out.append(kc_n)
        kk = jnp.concatenate([kp_n, kc_n], axis=0).astype(BF16)
        vv = jnp.concatenate([vp_ref[:, ks], vc_ref[:, ks]], axis=0).astype(BF16)
        for g in range(GRP):
            h = kh * GRP + g
            qh = _rms(q_ref[:, h * HD:(h + 1) * HD], qg).astype(BF16)
            s = _dot_nt(qh, kk) * ATT_SCALE
            s = jnp.where(mask, s, -jnp.inf)
            sink = sinks_ref[h]
            m = jnp.maximum(jnp.max(s, axis=-1, keepdims=True), sink)
            p = jnp.exp(s - m)
            den = jnp.sum(p, axis=-1, keepdims=True) + jnp.exp(sink - m)
            outs.append(_dot(p.astype(BF16), vv) / den)
    o_ref[...] = jnp.concatenate(outs, axis=-1).astype(o_ref.dtype)
    ko_ref[0] = jnp.concatenate(k_out, axis=-1)
    vo_ref[0] = vc_ref[...]


def _attn_prompt(p_act, qg, kg, sinks):
    nblk = SEQ // WINDOW
    cq, ck, cv = P_Q // 1024, P_K // 256, P_V // 256
    return pl.pallas_call(
        _attn_prompt_kernel,
        grid=(MP // WINDOW,),
        in_specs=[
            pl.BlockSpec(memory_space=pltpu.SMEM),
            pl.BlockSpec((WINDOW, 1024), lambda g: (g, cq)),
            pl.BlockSpec((WINDOW, 256), lambda g: (g, ck)),
            pl.BlockSpec((WINDOW, 256), lambda g: (g, cv)),
            pl.BlockSpec((WINDOW, 256), lambda g: (jnp.maximum(g - 1, 0), ck)),
            pl.BlockSpec((WINDOW, 256), lambda g: (jnp.maximum(g - 1, 0), cv)),
            pl.BlockSpec((1, HD), lambda g: (0, 0)),
            pl.BlockSpec((1, HD), lambda g: (0, 0)),
        ],
        out_specs=[
            pl.BlockSpec((WINDOW, 1024), lambda g: (g, 0)),
            pl.BlockSpec((1, WINDOW, 256), lambda g: (g // nblk, 0, 0)),
            pl.BlockSpec((1, WINDOW, 256), lambda g: (g // nblk, 0, 0)),
        ],
        out_shape=[_sds((MP, 1024), BF16), _sds((BATCH, WINDOW, 256), F32),
                   _sds((BATCH, WINDOW, 256), F32)],
        compiler_params=_cparams(1),
        name="attn_prompt",
    )(sinks, p_act, p_act, p_act, p_act, p_act, qg, kg)


def _attn_sample_kernel(sinks_ref, q_ref, kn_ref, vn_ref, ck_ref, cv_ref, qg_ref, kg_ref,
                        o_ref, kno_ref):
    rows = GRP * DEC_SEQ
    t = lax.broadcasted_iota(I32, (rows, 1), 0) % DEC_SEQ
    g_of_row = lax.broadcasted_iota(I32, (rows, 1), 0) // DEC_SEQ
    col_c = lax.broadcasted_iota(I32, (rows, WINDOW), 1)
    col_n = lax.broadcasted_iota(I32, (rows, 8), 1)
    qg = qg_ref[...]
    kg = kg_ref[...]
    for kh in range(NKV):
        q16 = _rms(q_ref[0, kh], qg).astype(BF16)
        kn = _rms(kn_ref[0, kh], kg)
        kno_ref[0, kh] = kn
        s_c = _dot_nt(q16, ck_ref[0, kh].astype(BF16)) * ATT_SCALE
        s_n = _dot_nt(q16, kn.astype(BF16)) * ATT_SCALE
        s_c = jnp.where(col_c > t, s_c, -jnp.inf)
        s_n = jnp.where(col_n <= t, s_n, -jnp.inf)
        sink = jnp.zeros((rows, 1), F32)
        for g in range(GRP):
            sink = jnp.where(g_of_row == g, sinks_ref[kh * GRP + g], sink)
        m = jnp.maximum(jnp.maximum(jnp.max(s_c, axis=-1, keepdims=True),
                                    jnp.max(s_n, axis=-1, keepdims=True)), sink)
        p_c = jnp.exp(s_c - m)
        p_n = jnp.exp(s_n - m)
        den = (jnp.sum(p_c, axis=-1, keepdims=True) + jnp.sum(p_n, axis=-1, keepdims=True)
               + jnp.exp(sink - m))
        o = _dot(p_c.astype(BF16), cv_ref[0, kh].astype(BF16)) + _dot(
            p_n.astype(BF16), vn_ref[0, kh].astype(BF16))
        o_ref[0, kh] = o / den


def _attn_sample(q_r, kn_r, vn_r, ck_r, cv_r, qg, kg, sinks):
    rows = GRP * DEC_SEQ
    return pl.pallas_call(
        _attn_sample_kernel,
        grid=(DEC_BATCH,),
        in_specs=[
            pl.BlockSpec(memory_space=pltpu.SMEM),
            pl.BlockSpec((1, NKV, rows, HD), lambda b: (b, 0, 0, 0)),
            pl.BlockSpec((1, NKV, 8, HD), lambda b: (b, 0, 0, 0)),
            pl.BlockSpec((1, NKV, 8, HD), lambda b: (b, 0, 0, 0)),
            pl.BlockSpec((1, NKV, WINDOW, HD), lambda b: (b, 0, 0, 0)),
            pl.BlockSpec((1, NKV, WINDOW, HD), lambda b: (b, 0, 0, 0)),
            pl.BlockSpec((1, HD), lambda b: (0, 0)),
            pl.BlockSpec((1, HD), lambda b: (0, 0)),
        ],
        out_specs=[
            pl.BlockSpec((1, NKV, rows, HD), lambda b: (b, 0, 0, 0)),
            pl.BlockSpec((1, NKV, 8, HD), lambda b: (b, 0, 0, 0)),
        ],
        out_shape=[_sds((DEC_BATCH, NKV, rows, HD), F32), _sds((DEC_BATCH, NKV, 8, HD), F32)],
        compiler_params=_cparams(1),
        name="attn_sample",
    )(sinks, q_r, kn_r, vn_r, ck_r, cv_r, qg, kg)


def _cumsum_rows(x, c):
    row = lax.broadcasted_iota(I32, x.shape, 0)
    s = 1
    while s < c:
        x = x + jnp.where(row >= s, pltpu.roll(x, s, 0), 0.0)
        s *= 2
    return x


def _delta_chunk(c, nstack, conv_slice, g_full, beta_full, z_slice, norm_g, s_get, s_put, o_put):
    n = nstack * c
    lg = c.bit_length() - 1
    r = lax.broadcasted_iota(I32, (n, n), 0)
    cc = lax.broadcasted_iota(I32, (n, n), 1)
    same = (r >> lg) == (cc >> lg)
    incl = same & (r >= cc)
    strict = same & (r > cc)
    eye_b = r == cc
    eye = eye_b.astype(F32)
    gc = _cumsum_rows(g_full, c)
    for st in range(DNH // nstack):
        heads = range(st * nstack, (st + 1) * nstack)
        qs, ks, vs, gcs, bs = [], [], [], [], []
        for h in heads:
            qh = conv_slice(slice(h * DK, (h + 1) * DK))
            kh = conv_slice(slice(DNH * DK + h * DK, DNH * DK + (h + 1) * DK))
            vs.append(conv_slice(slice(2 * DNH * DK + h * DK, 2 * DNH * DK + (h + 1) * DK)))
            qs.append(qh * lax.rsqrt(jnp.sum(qh * qh, axis=-1, keepdims=True) + EPS) * (DK ** -0.5))
            ks.append(kh * lax.rsqrt(jnp.sum(kh * kh, axis=-1, keepdims=True) + EPS))
            gcs.append(gc[:, h:h + 1])
            bs.append(beta_full[:, DNH + h:DNH + h + 1])
        q = jnp.concatenate(qs, axis=0)
        k = jnp.concatenate(ks, axis=0)
        v = jnp.concatenate(vs, axis=0)
        gcol = jnp.concatenate(gcs, axis=0)
        bcol = jnp.concatenate(bs, axis=0)
        grow = jnp.sum(jnp.where(eye_b, gcol, 0.0), axis=0, keepdims=True)
        decay = jnp.exp(jnp.where(incl, gcol - grow, -jnp.inf))
        egc = jnp.exp(gcol)
        kb = k * bcol
        k16 = k.astype(BF16)
        a_mat = jnp.where(strict, _dot_nt(kb.astype(BF16), k16) * decay, 0.0)
        blk = 1
        t_inv = eye
        while blk < c:
            sh = blk.bit_length()
            pair = ((r >> sh) == (cc >> sh)) & ((r & blk) != 0) & ((cc & blk) == 0)
            off = jnp.where(pair, a_mat, 0.0)
            if blk == 1:
                t_inv = t_inv - off
            else:
                t16 = t_inv.astype(BF16)
                t_inv = t_inv - _dot(t16, _dot(off.astype(BF16), t16).astype(BF16))
            blk *= 2
        rhs = jnp.concatenate([v * bcol, kb * egc], axis=-1).astype(BF16)
        sol = _dot(t_inv.astype(BF16), rhs)
        qk16 = (_dot_nt(q.astype(BF16), k16) * decay).astype(BF16)
        q_dec = q * egc
        us, s_olds, g_lasts = [], [], []
        for i, h in enumerate(heads):
            hs = slice(i * c, (i + 1) * c)
            s_old = s_get(h)
            us.append(sol[hs, :DK] - _dot(sol[hs, DK:].astype(BF16), s_old.astype(BF16)))
            s_olds.append(s_old)
            g_lasts.append(gcol[(i + 1) * c - 1:(i + 1) * c, :])
        u16 = jnp.concatenate(us, axis=0).astype(BF16)
        o_in = _dot(qk16, u16)
        for i, h in enumerate(heads):
            hs = slice(i * c, (i + 1) * c)
            s16 = s_olds[i].astype(BF16)
            o = _dot(q_dec[hs].astype(BF16), s16) + o_in[hs]
            k_dec = k[hs] * jnp.exp(g_lasts[i] - gcol[hs])
            s_put(h, s_olds[i] * jnp.exp(g_lasts[i]) + _dot_tn(k_dec.astype(BF16), u16[hs]))
            zz = z_slice(slice(h * DK, (h + 1) * DK))
            o_put(h, _rms(o, norm_g) * _silu(zz))


def _delta_prompt_kernel(x_ref, prev_ref, dz_ref, dab_ref, cw_ref, alog_ref, dtb_ref, ng_ref,
                         o_ref, s_ref, dc_ref):
    c = pl.program_id(0)

    @pl.when(c == 0)
    def _():
        s_ref[...] = jnp.zeros_like(s_ref)

    row8 = lax.broadcasted_iota(I32, (8, DK), 0)
    keep_prev = c > 0
    for b in range(BATCH):
        def conv_slice(cs, b=b):
            x = x_ref[b, :, cs]
            prev = jnp.where(keep_prev, prev_ref[b, :, cs], 0.0)
            y = x * cw_ref[CONV_W - 1:CONV_W, cs]
            for s in range(1, CONV_W):
                xr = pltpu.roll(x, s, 0)
                top = jnp.where(row8 < s, pltpu.roll(prev, s, 0), xr[0:8])
                sh = jnp.concatenate([top, xr[8:]], axis=0)
                y = y + sh * cw_ref[CONV_W - 1 - s:CONV_W - s, cs]
            return _silu(y)

        dab = dab_ref[b]
        g_full = -jnp.exp(alog_ref[...]) * _softplus(dab + dtb_ref[...])
        beta_full = _sigmoid(dab)

        def s_get(h, b=b):
            return s_ref[b, h]

        def s_put(h, v, b=b):
            s_ref[b, h] = v

        def o_put(h, v, b=b):
            o_ref[b, :, h * DK:(h + 1) * DK] = v.astype(o_ref.dtype)

        _delta_chunk(CHUNK, 4, conv_slice, g_full, beta_full, lambda cs, b=b: dz_ref[b, :, cs],
                     ng_ref[...], s_get, s_put, o_put)

    @pl.when(c == pl.num_programs(0) - 1)
    def _():
        dc_ref[...] = x_ref[:, CHUNK - (CONV_W - 1):CHUNK, :]


def _delta_prompt(p_act, p_ab, conv_w, alog_row, dtb_row, norm_g):
    nck = SEQ // CHUNK
    p3 = p_act.reshape(BATCH, SEQ, p_act.shape[1])
    ab3 = p_ab.reshape(BATCH, SEQ, 128)
    return pl.pallas_call(
        _delta_prompt_kernel,
        grid=(nck,),
        in_specs=[
            pl.BlockSpec((BATCH, CHUNK, CONV_CH), lambda c: (0, c, P_DQKV // CONV_CH)),
            pl.BlockSpec((BATCH, 8, CONV_CH),
                         lambda c: (0, jnp.maximum(c * (CHUNK // 8) - 1, 0), P_DQKV // CONV_CH)),
            pl.BlockSpec((BATCH, CHUNK, 1024), lambda c: (0, c, P_DZ // 1024)),
            pl.BlockSpec((BATCH, CHUNK, 128), lambda c: (0, c, 0)),
            pl.BlockSpec((CONV_W, CONV_CH), lambda c: (0, 0)),
            pl.BlockSpec((1, 128), lambda c: (0, 0)),
            pl.BlockSpec((1, 128), lambda c: (0, 0)),
            pl.BlockSpec((1, DK), lambda c: (0, 0)),
        ],
        out_specs=[
            pl.BlockSpec((BATCH, CHUNK, 1024), lambda c: (0, c, 0)),
            pl.BlockSpec((BATCH, DNH, DK, DK), lambda c: (0, 0, 0, 0)),
            pl.BlockSpec((BATCH, CONV_W - 1, CONV_CH), lambda c: (0, 0, 0)),
        ],
        out_shape=[_sds((BATCH, SEQ, 1024), BF16), _sds((BATCH, DNH, DK, DK), F32),
                   _sds((BATCH, CONV_W - 1, CONV_CH), F32)],
        compiler_params=_cparams(1),
        name="delta_prompt",
    )(p3, p3, p3, ab3, conv_w, alog_row, dtb_row, norm_g)


def _delta_sample_kernel(xp_ref, dz_ref, dab_ref, s0_ref, cw_ref, alog_ref, dtb_ref, ng_ref,
                         o_ref, s_ref):
    row = lax.broadcasted_iota(I32, (8, DK), 0)
    live = row < DEC_SEQ

    def conv_slice(cs):
        xp = xp_ref[0, :, cs]
        y = jnp.zeros((8, DK), F32)
        for j in range(CONV_W):
            y = y + pltpu.roll(xp, 8 - 1 - j, 0) * cw_ref[j:j + 1, cs]
        return jnp.where(live, _silu(y), 0.0)

    dab = dab_ref[0]
    g_full = jnp.where(live, -jnp.exp(alog_ref[...]) * _softplus(dab + dtb_ref[...]), 0.0)
    beta_full = jnp.where(live, _sigmoid(dab), 0.0)

    def s_put(h, v):
        s_ref[0, h] = v

    def o_put(h, v):
        o_ref[0, :, h * DK:(h + 1) * DK] = v

    _delta_chunk(8, DNH, conv_slice, g_full, beta_full, lambda cs: dz_ref[0, :, cs], ng_ref[...],
                 lambda h: s0_ref[0, h], s_put, o_put)


def _delta_sample(xp8, dz8, dab8, s0, conv_w, alog_row, dtb_row, norm_g):
    return pl.pallas_call(
        _delta_sample_kernel,
        grid=(DEC_BATCH,),
        in_specs=[
            pl.BlockSpec((1, 8, CONV_CH), lambda b: (b, 0, 0)),
            pl.BlockSpec((1, 8, 1024), lambda b: (b, 0, 0)),
            pl.BlockSpec((1, 8, 128), lambda b: (b, 0, 0)),
            pl.BlockSpec((1, DNH, DK, DK), lambda b: (b, 0, 0, 0)),
            pl.BlockSpec((CONV_W, CONV_CH), lambda b: (0, 0)),
            pl.BlockSpec((1, 128), lambda b: (0, 0)),
            pl.BlockSpec((1, 128), lambda b: (0, 0)),
            pl.BlockSpec((1, DK), lambda b: (0, 0)),
        ],
        out_specs=[
            pl.BlockSpec((1, 8, 1024), lambda b: (b, 0, 0)),
            pl.BlockSpec((1, DNH, DK, DK), lambda b: (b, 0, 0, 0)),
        ],
        out_shape=[_sds((DEC_BATCH, 8, 1024), F32), _sds((DEC_BATCH, DNH, DK, DK), F32)],
        compiler_params=_cparams(1),
        name="delta_sample",
    )(xp8, dz8, dab8, s0, conv_w, alog_row, dtb_row, norm_g)


def _lru_gates(xc, wr, br, wi, bi, sp):
    x16 = xc.astype(BF16)
    r = _sigmoid(_dot(x16, wr) + br)
    i = _sigmoid(_dot(x16, wi) + bi)
    log_a = -LRU_C * r * sp
    a = jnp.exp(log_a)
    th = jnp.tanh(log_a)
    u = jnp.sqrt(-2.0 * th / (1.0 - th)) * (i * xc)
    return a, u


LRU_TT = 256


def _lru_prompt_kernel(x_ref, prev_ref, cw_ref, cb_ref, wr_ref, br_ref, wi_ref, bi_ref, lam_ref,
                       o_ref, h_ref, lc_ref):
    t = pl.program_id(1)
    row8 = lax.broadcasted_iota(I32, (8, LBLK), 0)
    row = lax.broadcasted_iota(I32, (LRU_TT, LBLK), 0)
    keep = t > 0
    for n in range(LNB):
        cs = slice(n * LBLK, (n + 1) * LBLK)
        x = x_ref[:, cs]
        prev = jnp.where(keep, prev_ref[:, cs], 0.0)
        y = x * cw_ref[CONV_W - 1:CONV_W, cs]
        for s in range(1, CONV_W):
            xr = pltpu.roll(x, s, 0)
            top = jnp.where(row8 < s, pltpu.roll(prev, s, 0), xr[0:8])
            y = y + jnp.concatenate([top, xr[8:]], axis=0) * cw_ref[CONV_W - 1 - s:CONV_W - s, cs]
        xc = y + cb_ref[:, cs]
        sp = _softplus(-lam_ref[:, cs])
        a, u = _lru_gates(xc, wr_ref[n], br_ref[:, cs], wi_ref[n], bi_ref[:, cs], sp)
        h0 = jnp.where(keep, h_ref[0, :, cs], 0.0)
        u = u + jnp.where(row == 0, a * h0, 0.0)
        s = 1
        while s < LRU_TT:
            valid = row >= s
            u_s = pltpu.roll(u, s, 0)
            a_s = pltpu.roll(a, s, 0)
            u = jnp.where(valid, a * u_s + u, u)
            a = jnp.where(valid, a * a_s, a)
            s *= 2
        o_ref[:, cs] = u.astype(o_ref.dtype)
        h_ref[0, :, cs] = u[LRU_TT - 1:LRU_TT, :]

    @pl.when(t == pl.num_programs(1) - 1)
    def _():
        lc_ref[0] = x_ref[LRU_TT - (CONV_W - 1):LRU_TT, :]


def _lru_prompt(p_act, cw, cb, wr, br, wi, bi, lam):
    ntt = SEQ // LRU_TT
    cl = P_LX // LW
    vec = pl.BlockSpec((1, LW), lambda b, t: (0, 0))
    mat = pl.BlockSpec((LNB, LBLK, LBLK), lambda b, t: (0, 0, 0))
    return pl.pallas_call(
        _lru_prompt_kernel,
        grid=(BATCH, ntt),
        in_specs=[
            pl.BlockSpec((LRU_TT, LW), lambda b, t: (b * ntt + t, cl)),
            pl.BlockSpec((8, LW),
                         lambda b, t: (jnp.maximum(b * (SEQ // 8) + t * (LRU_TT // 8) - 1, 0), cl)),
            pl.BlockSpec((CONV_W, LW), lambda b, t: (0, 0)),
            vec, mat, vec, mat, vec, vec,
        ],
        out_specs=[
            pl.BlockSpec((LRU_TT, LW), lambda b, t: (b * ntt + t, 0)),
            pl.BlockSpec((1, 1, LW), lambda b, t: (b, 0, 0)),
            pl.BlockSpec((1, CONV_W - 1, LW), lambda b, t: (b, 0, 0)),
        ],
        out_shape=[_sds((MP, LW), BF16), _sds((BATCH, 1, LW), F32),
                   _sds((BATCH, CONV_W - 1, LW), F32)],
        compiler_params=_cparams(2),
        name="lru_prompt",
    )(p_act, p_act, cw, cb, wr, br, wi, bi, lam)


def _lru_sample_kernel(x_ref, buf_ref, h0_ref, cw_ref, cb_ref, wr_ref, br_ref, wi_ref, bi_ref,
                       lam_ref, o_ref):
    for n in range(LNB):
        cs = slice(n * LBLK, (n + 1) * LBLK)
        xp = [buf_ref[j, :, cs] for j in range(CONV_W - 1)] + [x_ref[t, :, cs] for t in range(DEC_SEQ)]
        xcs = []
        for t in range(DEC_SEQ):
            y = xp[t] * cw_ref[0:1, cs]
            for j in range(1, CONV_W):
                y = y + xp[t + j] * cw_ref[j:j + 1, cs]
            xcs.append(y + cb_ref[:, cs])
        xc = jnp.concatenate(xcs, axis=0)
        sp = _softplus(-lam_ref[:, cs])
        a, u = _lru_gates(xc, wr_ref[n], br_ref[:, cs], wi_ref[n], bi_ref[:, cs], sp)
        h = h0_ref[:, cs]
        for t in range(DEC_SEQ):
            rs = slice(t * DEC_BATCH, (t + 1) * DEC_BATCH)
            h = a[rs] * h + u[rs]
            o_ref[t, :, cs] = h


def _lru_sample(x_tm, buf_tm, h0, cw, cb, wr, br, wi, bi, lam):
    return pl.pallas_call(
        _lru_sample_kernel,
        out_shape=_sds((DEC_SEQ, DEC_BATCH, LW), F32),
        name="lru_sample",
    )(x_tm, buf_tm, h0, cw, cb, wr, br, wi, bi, lam)


def _merge_kernel(oa_ref, ob_ref, oc_ref, w_ref, ga_ref, gb_ref, gc_ref, o_ref):
    acc = _sigmoid(ga_ref[...]) * _dot(oa_ref[...], w_ref[0])
    acc = acc + _sigmoid(gb_ref[...]) * _dot(ob_ref[...], w_ref[1])
    acc = acc + _sigmoid(gc_ref[...]) * _dot(oc_ref[...], w_ref[2])
    o_ref[...] = acc.astype(o_ref.dtype)


def _merge(o_a, o_b, o_c, wb, p_act, tm):
    m_rows = o_a.shape[0]
    tn = 512
    g0 = P_G // tn
    gs = D // tn
    br = pl.BlockSpec((tm, 1024), lambda n, m: (m, 0))
    return pl.pallas_call(
        _merge_kernel,
        grid=(D // tn, m_rows // tm),
        in_specs=[
            br, br, br,
            pl.BlockSpec((3, 1024, tn), lambda n, m: (0, 0, n)),
            pl.BlockSpec((tm, tn), lambda n, m: (m, g0 + n)),
            pl.BlockSpec((tm, tn), lambda n, m: (m, g0 + gs + n)),
            pl.BlockSpec((tm, tn), lambda n, m: (m, g0 + 2 * gs + n)),
        ],
        out_specs=pl.BlockSpec((tm, tn), lambda n, m: (m, n)),
        out_shape=_sds((m_rows, D), BF16),
        compiler_params=_cparams(2, VMEM_BIG),
        name="merge",
    )(o_a, o_b, o_c, wb, p_act, p_act, p_act)


def _outproj_kernel(a_ref, w_ref, x_ref, gt_ref, o_ref):
    o_ref[...] = x_ref[...] + gt_ref[0, 0] * _dot(a_ref[...], w_ref[...])


def _outproj(merged, w_out, x, mod, tm):
    m_rows = x.shape[0]
    tn = 512
    tiles = m_rows // tm
    return pl.pallas_call(
        _outproj_kernel,
        grid=(D // tn, tiles),
        in_specs=[
            pl.BlockSpec((tm, D), lambda n, m: (m, 0)),
            pl.BlockSpec((D, tn), lambda n, m: (0, n)),
            pl.BlockSpec((tm, tn), lambda n, m: (m, n)),
            _mod_spec(mod, 2, tiles, tn, with_n=True),
        ],
        out_specs=pl.BlockSpec((tm, tn), lambda n, m: (m, n)),
        out_shape=_sds((m_rows, D), F32),
        compiler_params=_cparams(2, VMEM_BIG),
        name="outproj",
    )(merged, w_out, x, mod)


def _router_kernel(xp_ref, xs_ref, g_ref, shp_ref, scp_ref, shs_ref, scs_ref, wr_ref, br_ref,
                   h_ref, te_ref, tw_ref):
    tm = TOK_TILE
    is_s = pl.program_id(0) == N_TT - 1
    x = jnp.where(is_s, xs_ref[...], xp_ref[...])
    sc = jnp.where(is_s, scs_ref[0, 0], scp_ref[0, 0])
    sh = jnp.where(is_s, shs_ref[0, 0], shp_ref[0, 0])
    h = _rms(x, g_ref[...]) * (1.0 + sc) + sh
    h_ref[...] = h
    lane = lax.broadcasted_iota(I32, (tm, 128), 1)
    logits = _dot(h.astype(BF16), wr_ref[...]) + br_ref[...]
    logits = jnp.where(lane < NE, logits, -jnp.inf)
    lane4 = lax.broadcasted_iota(I32, (tm, TOPK), 1)
    te = jnp.zeros((tm, TOPK), I32)
    tl = jnp.zeros((tm, TOPK), F32)
    for k in range(TOPK):
        m = jnp.max(logits, axis=-1, keepdims=True)
        idx = jnp.min(jnp.where(logits == m, lane, 128), axis=-1, keepdims=True)
        te = jnp.where(lane4 == k, idx, te)
        tl = jnp.where(lane4 == k, m, tl)
        logits = jnp.where(lane == idx, -jnp.inf, logits)
    e = jnp.exp(tl - tl[:, 0:1])
    te_ref[...] = te
    tw_ref[...] = e / jnp.sum(e, axis=-1, keepdims=True)


def _router(x1p, x1s, g, mod_p, mod_s, wr, br):
    np_t = MP // TOK_TILE
    tpg = np_t // BATCH
    last_p = np_t - 1

    def mp(j):
        return pl.BlockSpec((1, 1, 1, D), lambda m: (j, jnp.minimum(m, last_p) // tpg, 0, 0))

    def ms(j):
        return pl.BlockSpec((1, 1, MS, D), lambda m: (j, 0, 0, 0))

    return pl.pallas_call(
        _router_kernel,
        grid=(N_TT,),
        in_specs=[
            pl.BlockSpec((TOK_TILE, D), lambda m: (jnp.minimum(m, last_p), 0)),
            pl.BlockSpec((MS, D), lambda m: (0, 0)),
            pl.BlockSpec((1, D), lambda m: (0, 0)),
            mp(3), mp(4), ms(3), ms(4),
            pl.BlockSpec((D, 128), lambda m: (0, 0)),
            pl.BlockSpec((1, 128), lambda m: (0, 0)),
        ],
        out_specs=[
            pl.BlockSpec((TOK_TILE, D), lambda m: (m, 0)),
            pl.BlockSpec((TOK_TILE, TOPK), lambda m: (m, 0)),
            pl.BlockSpec((TOK_TILE, TOPK), lambda m: (m, 0)),
        ],
        out_shape=[_sds((NTOK, D), F32), _sds((NTOK, TOPK), I32), _sds((NTOK, TOPK), F32)],
        compiler_params=_cparams(1),
        name="router",
    )(x1p, x1s, g, mod_p, mod_p, mod_s, mod_s, wr, br)


def _rank_kernel(te_ref, rk_ref, cnt_ref):
    @pl.when(pl.program_id(0) == 0)
    def _():
        cnt_ref[...] = jnp.zeros_like(cnt_ref)

    tt = TOK_TILE
    lane = lax.broadcasted_iota(I32, (tt, 128), 1)
    te = te_ref[...]
    hot = jnp.zeros((tt, 128), F32)
    for k in range(TOPK):
        hot = hot + (lane == te[:, k:k + 1]).astype(F32)
    r = lax.broadcasted_iota(I32, (tt, tt), 0)
    c = lax.broadcasted_iota(I32, (tt, tt), 1)
    before = (r > c).astype(BF16)
    tot = _dot(before, hot.astype(BF16)) + cnt_ref[...]
    lane4 = lax.broadcasted_iota(I32, (tt, TOPK), 1)
    rk = jnp.zeros((tt, TOPK), F32)
    for k in range(TOPK):
        rk_k = jnp.sum(jnp.where(lane == te[:, k:k + 1], tot, 0.0), axis=-1, keepdims=True)
        rk = jnp.where(lane4 == k, rk_k, rk)
    rk_ref[...] = rk.astype(I32)
    cnt_ref[...] = cnt_ref[...] + jnp.sum(hot, axis=0, keepdims=True)


def _rank(te):
    return pl.pallas_call(
        _rank_kernel,
        grid=(N_TT,),
        in_specs=[pl.BlockSpec((TOK_TILE, TOPK), lambda i: (i, 0))],
        out_specs=[pl.BlockSpec((TOK_TILE, TOPK), lambda i: (i, 0)),
                   pl.BlockSpec((1, 128), lambda i: (0, 0))],
        out_shape=[_sds((NTOK, TOPK), I32), _sds((1, 128), F32)],
        compiler_params=_cparams(1),
        name="rank",
    )(te)


def _scatter_kernel(dest_ref, last_ref, h_ref, xs_out, zbuf, zsem, sem):
    @pl.when(pl.program_id(0) == 0)
    def _():
        zbuf[...] = jnp.zeros_like(zbuf)

        def zero_block(row):
            return pltpu.make_async_copy(zbuf, xs_out.at[pl.ds(row, RB)], zsem)

        def extra_row(i):
            return pl.multiple_of(last_ref[NE] + i * RB, RB)

        n_extra = last_ref[NE + 1]
        for e in range(NE):
            zero_block(pl.multiple_of(last_ref[e], RB)).start()
        lax.fori_loop(0, n_extra, lambda i, c: (zero_block(extra_row(i)).start(), c)[1], 0)
        for e in range(NE):
            zero_block(pl.multiple_of(last_ref[e], RB)).wait()
        lax.fori_loop(0, n_extra, lambda i, c: (zero_block(extra_row(i)).wait(), c)[1], 0)

    def body(i, carry):
        for k in range(TOPK):
            d = dest_ref[0, 0, i * TOPK + k]
            pltpu.make_async_copy(h_ref.at[pl.ds(i, 1)], xs_out.at[pl.ds(d, 1)], sem).start(
                priority=k % 2)
        return carry

    lax.fori_loop(0, TOK_TILE, body, 0)
    for k in range(TOPK):
        pltpu.make_async_copy(h_ref, xs_out.at[pl.ds(0, TOK_TILE)], sem).wait()


def _scatter_rows(dest3, last_rows, h2):
    return pl.pallas_call(
        _scatter_kernel,
        grid=(N_TT,),
        in_specs=[
            pl.BlockSpec((1, 1, TOK_TILE * TOPK), lambda i: (i, 0, 0), memory_space=pltpu.SMEM),
            pl.BlockSpec(memory_space=pltpu.SMEM),
            pl.BlockSpec((TOK_TILE, D), lambda i: (i, 0)),
        ],
        out_specs=pl.BlockSpec(memory_space=pl.ANY),
        out_shape=_sds((ROWS, D), F32),
        scratch_shapes=[pltpu.VMEM((RB, D), F32), pltpu.SemaphoreType.DMA(()),
                        pltpu.SemaphoreType.DMA(())],
        compiler_params=_cparams(1),
        name="scatter_rows",
    )(dest3, last_rows, h2)


def _zero_unowned(zsrc, dst_at, used_ref, zsem):
    n = NB_MAX - used_ref[0]

    def cp(i):
        return pltpu.make_async_copy(zsrc, dst_at(pl.multiple_of((used_ref[0] + i) * RB, RB)), zsem)

    lax.fori_loop(0, n, lambda i, c: (cp(i).start(), c)[1], 0)
    lax.fori_loop(0, n, lambda i, c: (cp(i).wait(), c)[1], 0)


def _gmm_up_kernel(bstart, nblk, used, x_hbm, wg_ref, wl_ref, bg_ref, bl_ref, act_hbm,
                   wg_s, wl_s, xbuf, obuf, xsem, osem, zsem):
    s = pl.program_id(0)
    e = s // NJ
    j = s % NJ
    nb = nblk[e]
    b0 = bstart[e]

    @pl.when(s == 0)
    def _():
        obuf[0] = jnp.zeros(obuf.shape[1:], obuf.dtype)
        for jj in range(NJ):
            _zero_unowned(obuf.at[0], lambda row, jj=jj: act_hbm.at[jj, pl.ds(row, RB)], used, zsem)

    def x_copy(i, slot):
        row = pl.multiple_of((b0 + i) * RB, RB)
        return pltpu.make_async_copy(x_hbm.at[pl.ds(row, RB)], xbuf.at[slot], xsem.at[slot])

    def o_copy(i, slot):
        row = pl.multiple_of((b0 + i) * RB, RB)
        return pltpu.make_async_copy(obuf.at[slot], act_hbm.at[j, pl.ds(row, RB)], osem.at[slot])

    @pl.when(nb > 0)
    def _():
        wg_s[...] = wg_ref[0, 0].astype(BF16)
        wl_s[...] = wl_ref[0, 0].astype(BF16)
        x_copy(0, 0).start()

        def body(i, carry):
            slot = i % 2
            x_copy(i, slot).wait()

            @pl.when(i + 1 < nb)
            def _():
                x_copy(i + 1, 1 - slot).start()

            @pl.when(i >= 2)
            def _():
                o_copy(i - 2, slot).wait()

            x = xbuf[slot].astype(BF16)
            glu = _dot(x, wg_s[...]) + bg_ref[0, 0]
            lin = _dot(x, wl_s[...]) + bl_ref[0, 0]
            glu = jnp.minimum(glu, SW_LIMIT)
            lin = jnp.clip(lin, -SW_LIMIT, SW_LIMIT)
            obuf[slot] = (glu * _sigmoid(SW_ALPHA * glu) * (lin + 1.0)).astype(obuf.dtype)
            o_copy(i, slot).start()
            return carry

        lax.fori_loop(0, nb, body, 0)

        @pl.when(nb >= 2)
        def _():
            o_copy(nb - 2, nb % 2).wait()

        o_copy(nb - 1, (nb - 1) % 2).wait()


def _gmm_up(tables, xs, w_gu, b_gu, layer):
    grid_spec = pltpu.PrefetchScalarGridSpec(
        num_scalar_prefetch=3,
        grid=(NE * NJ,),
        in_specs=[
            pl.BlockSpec(memory_space=pl.ANY),
            pl.BlockSpec((1, 1, D, TN_E), lambda s, *_: (layer, s // NJ, 0, s % NJ)),
            pl.BlockSpec((1, 1, D, TN_E), lambda s, *_: (layer, s // NJ, 0, NJ + s % NJ)),
            pl.BlockSpec((1, 1, 1, TN_E), lambda s, *_: (layer, s // NJ, 0, s % NJ)),
            pl.BlockSpec((1, 1, 1, TN_E), lambda s, *_: (layer, s // NJ, 0, NJ + s % NJ)),
        ],
        out_specs=pl.BlockSpec(memory_space=pl.ANY),
        scratch_shapes=[pltpu.VMEM((D, TN_E), BF16), pltpu.VMEM((D, TN_E), BF16),
                        pltpu.VMEM((2, RB, D), F32), pltpu.VMEM((2, RB, TN_E), BF16),
                        pltpu.SemaphoreType.DMA((2,)), pltpu.SemaphoreType.DMA((2,)),
                        pltpu.SemaphoreType.DMA(())],
    )
    return pl.pallas_call(
        _gmm_up_kernel,
        grid_spec=grid_spec,
        out_shape=_sds((NJ, ROWS, TN_E), BF16),
        compiler_params=_cparams(1, VMEM_BIG),
        name="gmm_up",
    )(*tables, xs, w_gu, w_gu, b_gu, b_gu)


def _gmm_down_kernel(bstart, nblk, used, x_hbm, w_ref, b_ref, out_hbm, w_s, xbuf, obuf,
                     xsem, osem, zsem):
    e = pl.program_id(0)
    nb = nblk[e]
    b0 = bstart[e]

    @pl.when(e == 0)
    def _():
        obuf[0] = jnp.zeros(obuf.shape[1:], obuf.dtype)
        _zero_unowned(obuf.at[0], lambda row: out_hbm.at[pl.ds(row, RB)], used, zsem)

    def x_copy(i, slot, jj):
        row = pl.multiple_of((b0 + i) * RB, RB)
        return pltpu.make_async_copy(x_hbm.at[jj, pl.ds(row, RB)], xbuf.at[slot, jj],
                                     xsem.at[slot])

    def o_copy(i, slot):
        row = pl.multiple_of((b0 + i) * RB, RB)
        return pltpu.make_async_copy(obuf.at[slot], out_hbm.at[pl.ds(row, RB)], osem.at[slot])

    @pl.when(nb > 0)
    def _():
        w_s[...] = w_ref[0, 0].astype(BF16)
        for jj in range(NJ):
            x_copy(0, 0, jj).start()

        def body(i, carry):
            slot = i % 2
            for jj in range(NJ):
                x_copy(i, slot, jj).wait()

            @pl.when(i + 1 < nb)
            def _():
                for jj in range(NJ):
                    x_copy(i + 1, 1 - slot, jj).start()

            @pl.when(i >= 2)
            def _():
                o_copy(i - 2, slot).wait()

            acc = b_ref[0, 0] + _dot(xbuf[slot, 0], w_s[0:TN_E, :])
            for jj in range(1, NJ):
                acc = acc + _dot(xbuf[slot, jj], w_s[jj * TN_E:(jj + 1) * TN_E, :])
            obuf[slot] = acc
            o_copy(i, slot).start()
            return carry

        lax.fori_loop(0, nb, body, 0)

        @pl.when(nb >= 2)
        def _():
            o_copy(nb - 2, nb % 2).wait()

        o_copy(nb - 1, (nb - 1) % 2).wait()


def _gmm_down(tables, act, w_down, b_down, layer):
    grid_spec = pltpu.PrefetchScalarGridSpec(
        num_scalar_prefetch=3,
        grid=(NE,),
        in_specs=[
            pl.BlockSpec(memory_space=pl.ANY),
            pl.BlockSpec((1, 1, DFF, D), lambda e, *_: (layer, e, 0, 0)),
            pl.BlockSpec((1, 1, 1, D), lambda e, *_: (layer, e, 0, 0)),
        ],
        out_specs=pl.BlockSpec(memory_space=pl.ANY),
        scratch_shapes=[pltpu.VMEM((DFF, D), BF16), pltpu.VMEM((2, NJ, RB, TN_E), BF16),
                        pltpu.VMEM((2, RB, D), F32),
                        pltpu.SemaphoreType.DMA((2,)), pltpu.SemaphoreType.DMA((2,)),
                        pltpu.SemaphoreType.DMA(())],
    )
    return pl.pallas_call(
        _gmm_down_kernel,
        grid_spec=grid_spec,
        out_shape=_sds((ROWS, D), F32),
        compiler_params=_cparams(1, VMEM_BIG),
        name="gmm_down",
    )(*tables, act, w_down, b_down)


def _combine_kernel(dest_ref, rows_hbm, tw_ref, x_ref, gt_ref, o_ref, buf, sem):
    def body(i, carry):
        for k in range(TOPK):
            d = dest_ref[0, 0, i * TOPK + k]
            pltpu.make_async_copy(rows_hbm.at[pl.ds(d, 1)], buf.at[k, pl.ds(i, 1)], sem).start(
                priority=k % 2)
        return carry

    lax.fori_loop(0, TOK_TILE, body, 0)
    for k in range(TOPK):
        pltpu.make_async_copy(rows_hbm.at[pl.ds(0, TOK_TILE)], buf.at[k], sem).wait()
    tw = tw_ref[...]
    y = tw[:, 0:1] * buf[0]
    for k in range(1, TOPK):
        y = y + tw[:, k:k + 1] * buf[k]
    o_ref[...] = x_ref[...] + gt_ref[0, 0] * y


def _combine(dest3, out_rows, tw, x1, mod, tile0):
    m_rows = x1.shape[0]
    tiles = m_rows // TOK_TILE
    return pl.pallas_call(
        _combine_kernel,
        grid=(tiles,),
        in_specs=[
            pl.BlockSpec((1, 1, TOK_TILE * TOPK), lambda m: (tile0 + m, 0, 0),
                         memory_space=pltpu.SMEM),
            pl.BlockSpec(memory_space=pl.ANY),
            pl.BlockSpec((TOK_TILE, TOPK), lambda m: (tile0 + m, 0)),
            pl.BlockSpec((TOK_TILE, D), lambda m: (m, 0)),
            _mod_spec(mod, 5, tiles),
        ],
        out_specs=pl.BlockSpec((TOK_TILE, D), lambda m: (m, 0)),
        out_shape=_sds((m_rows, D), F32),
        scratch_shapes=[pltpu.VMEM((TOPK, TOK_TILE, D), F32), pltpu.SemaphoreType.DMA(())],
        compiler_params=_cparams(1),
        name="combine",
    )(dest3, out_rows, tw, x1, mod)


def _routing_tables(te, rank, cnt):
    counts = cnt[0, :NE].astype(I32)
    nblk = (counts + RB - 1) // RB
    bstart = jnp.cumsum(nblk) - nblk
    used = jnp.sum(nblk)
    te_d = te.reshape(N_TT, TOK_TILE * TOPK)
    start_d = jnp.zeros_like(te_d)
    for e in range(NE):
        start_d = jnp.where(te_d == e, bstart[e] * RB, start_d)
    dest3 = (start_d + rank.reshape(N_TT, TOK_TILE * TOPK)).reshape(N_TT, 1, TOK_TILE * TOPK)
    empty = nblk == 0
    n_empty = jnp.sum(empty.astype(I32))
    last_blk = jnp.where(empty, used + jnp.cumsum(empty.astype(I32)) - 1, bstart + nblk - 1)
    extra0 = used + n_empty
    last_rows = jnp.concatenate([last_blk * RB, jnp.stack([extra0 * RB, NB_MAX - extra0])])
    tables = (bstart.astype(I32), nblk.astype(I32), used.reshape(1).astype(I32))
    return dest3, tables, last_rows.astype(I32)


def _moe(layer, h2_all, te_all, w_gu, b_gu, w_down, b_down):
    rank, cnt = _rank(te_all)
    dest3, steps, last_rows = _routing_tables(te_all, rank, cnt)
    xs = _scatter_rows(dest3, last_rows, h2_all)
    act = _gmm_up(steps, xs, w_gu, b_gu.reshape(DEPTH, NE, 1, 2 * DFF), layer)
    out_rows = _gmm_down(steps, act, w_down, b_down.reshape(DEPTH, NE, 1, D), layer)
    return dest3, out_rows


def _layer(l, xp, xs, st, w, mod_p, mod_s):
    cast = lambda a: a.astype(BF16)
    w_in = w["w_in"][l]
    w_main = cast(jnp.concatenate(
        [w_in[:, 1536:4608], w_in[:, 6672:12816], w_in[:, 4624:5648], w_in[:, 5648:6672],
         w_in[:, 0:1024], w_in[:, 1024:1280], w_in[:, 1280:1536]], axis=1))
    w_ab = cast(jnp.pad(w_in[:, 4608:4624], ((0, 0), (0, 112))))
    wb = cast(w["w_branch"][l])
    w_out = cast(w["w_out"][l])
    wr = cast(jnp.pad(w["w_router"][l], ((0, 0), (0, 128 - NE))))
    br = jnp.pad(w["b_router"][l], (0, 128 - NE)).reshape(1, 128)
    g1 = w["norm1_g"][l].reshape(1, D)
    g2 = w["norm2_g"][l].reshape(1, D)
    qg = w["q_norm_g"][l].reshape(1, HD)
    kg = w["k_norm_g"][l].reshape(1, HD)
    sinks = w["sinks"][l]
    dn_cw = w["dn_conv_w"][l]
    alog_row = jnp.pad(w["dn_a_log"][l], (0, 128 - DNH)).reshape(1, 128)
    dtb_row = jnp.pad(w["dn_dt_bias"][l], (0, 128 - DNH)).reshape(1, 128)
    dn_ng = w["dn_norm_g"][l].reshape(1, DK)
    l_cw = w["lru_conv_w"][l]
    l_cb = w["lru_conv_b"][l].reshape(1, LW)
    l_wr = cast(w["lru_w_r"][l])
    l_wi = cast(w["lru_w_i"][l])
    l_br = w["lru_b_r"][l].reshape(1, LW)
    l_bi = w["lru_b_i"][l].reshape(1, LW)
    l_lam = w["lru_lambda"][l].reshape(1, LW)

    h1p = _adaln(xp, g1, mod_p, 0, 1, 256)
    pp = _matmul(h1p, w_main, F32, 1024, 512, "proj")
    pab = _matmul(h1p, w_ab, F32, 1024, 128, "proj_ab")
    oa_p, k_p, v_p = _attn_prompt(pp, qg, kg, sinks)
    ob_p, d_p, dc_p = _delta_prompt(pp, pab, dn_cw, alog_row, dtb_row, dn_ng)
    ob_p = ob_p.reshape(MP, 1024)
    oc_p, h_p, hc_p = _lru_prompt(pp, l_cw, l_cb, l_wr, l_br, l_wi, l_bi, l_lam)
    mg_p = _merge(oa_p, ob_p, oc_p, wb, pp, 1024)
    x1p = _outproj(mg_p, w_out, xp, mod_p, 1024)

    ck, cv, s0, dconv0, h0, lconv0 = st
    h1s = _adaln(xs, g1, mod_s, 0, 1, MS)
    ps = _matmul(h1s, w_main, F32, MS, 512, "proj")
    psab = _matmul(h1s, w_ab, F32, MS, 128, "proj_ab")
    q_r = ps[:, P_Q:P_K].reshape(DEC_BATCH, DEC_SEQ, NKV, GRP, HD).transpose(0, 2, 3, 1, 4)
    q_r = q_r.reshape(DEC_BATCH, NKV, GRP * DEC_SEQ, HD)
    pad_t = ((0, 0), (0, 0), (0, 8 - DEC_SEQ), (0, 0))
    kn_r = jnp.pad(ps[:, P_K:P_V].reshape(DEC_BATCH, DEC_SEQ, NKV, HD).transpose(0, 2, 1, 3), pad_t)
    vn_raw = ps[:, P_V:P_W].reshape(DEC_BATCH, DEC_SEQ, NKV, HD)
    vn_r = jnp.pad(vn_raw.transpose(0, 2, 1, 3), pad_t)
    oa_r, kno = _attn_sample(q_r, kn_r, vn_r, ck.transpose(0, 2, 1, 3), cv.transpose(0, 2, 1, 3),
                             qg, kg, sinks)
    oa_s = oa_r.reshape(DEC_BATCH, NKV, GRP, DEC_SEQ, HD).transpose(0, 3, 1, 2, 4)
    oa_s = oa_s.reshape(MS, 1024).astype(BF16)
    k_s = jnp.concatenate([ck[:, DEC_SEQ:], kno[:, :, :DEC_SEQ].transpose(0, 2, 1, 3)], axis=1)
    v_s = jnp.concatenate([cv[:, DEC_SEQ:], vn_raw], axis=1)

    x_dq = ps[:, P_DQKV:P_DQKV + CONV_CH].reshape(DEC_BATCH, DEC_SEQ, CONV_CH)
    xp8 = jnp.concatenate([jnp.zeros((DEC_BATCH, 1, CONV_CH), F32), dconv0, x_dq], axis=1)
    pad8 = ((0, 0), (0, 8 - DEC_SEQ), (0, 0))
    dz8 = jnp.pad(ps[:, P_DZ:P_LX].reshape(DEC_BATCH, DEC_SEQ, 1024), pad8)
    dab8 = jnp.pad(psab.reshape(DEC_BATCH, DEC_SEQ, 128), pad8)
    ob_r, d_s = _delta_sample(xp8, dz8, dab8, s0, dn_cw, alog_row, dtb_row, dn_ng)
    ob_s = ob_r[:, :DEC_SEQ].reshape(MS, 1024).astype(BF16)
    dc_s = x_dq[:, DEC_SEQ - (CONV_W - 1):]

    x_lx = ps[:, P_LX:P_Q].reshape(DEC_BATCH, DEC_SEQ, LW)
    oc_tm = _lru_sample(x_lx.transpose(1, 0, 2), lconv0.transpose(1, 0, 2), h0, l_cw, l_cb,
                        l_wr, l_br, l_wi, l_bi, l_lam)
    oc_s = oc_tm.transpose(1, 0, 2).reshape(MS, LW).astype(BF16)
    h_s = oc_tm[DEC_SEQ - 1]
    hc_s = x_lx[:, DEC_SEQ - (CONV_W - 1):]

    mg_s = _merge(oa_s, ob_s, oc_s, wb, ps, MS)
    x1s = _outproj(mg_s, w_out, xs, mod_s, MS)

    h2_all, te_all, tw_all = _router(x1p, x1s, g2, mod_p, mod_s, wr, br)
    dest3, out_rows = _moe(l, h2_all, te_all, w["w_gate_up"], w["b_gate_up"],
                           w["w_down"], w["b_down"])
    np_t = MP // TOK_TILE
    xp_new = _combine(dest3, out_rows, tw_all, x1p, mod_p, 0)
    xs_new = _combine(dest3, out_rows, tw_all, x1s, mod_s, np_t)

    st_p = (k_p.reshape(BATCH, WINDOW, NKV, HD), v_p.reshape(BATCH, WINDOW, NKV, HD), d_p, dc_p,
            h_p.reshape(BATCH, LW), hc_p)
    st_s = (k_s, v_s, d_s, dc_s, h_s, hc_s)
    return xp_new, xs_new, st_p, st_s


def kernel(x_prompt, x_sample, cache_k, cache_v, state_delta, state_delta_conv, state_lru, state_lru_conv, c_prompt, c_sample, w_ada, b_ada, norm1_g, norm2_g, w_in, q_norm_g, k_norm_g, sinks, dn_conv_w, dn_a_log, dn_dt_bias, dn_norm_g, lru_conv_w, lru_conv_b, lru_w_r, lru_b_r, lru_w_i, lru_b_i, lru_lambda, w_branch, w_out, w_router, b_router, w_gate_up, b_gate_up, w_down, b_down):
    w = dict(w_in=w_in, w_branch=w_branch, w_out=w_out, w_router=w_router, b_router=b_router,
             norm1_g=norm1_g, norm2_g=norm2_g, q_norm_g=q_norm_g, k_norm_g=k_norm_g, sinks=sinks,
             dn_conv_w=dn_conv_w, dn_a_log=dn_a_log, dn_dt_bias=dn_dt_bias, dn_norm_g=dn_norm_g,
             lru_conv_w=lru_conv_w, lru_conv_b=lru_conv_b, lru_w_r=lru_w_r, lru_b_r=lru_b_r,
             lru_w_i=lru_w_i, lru_b_i=lru_b_i, lru_lambda=lru_lambda, w_gate_up=w_gate_up,
             b_gate_up=b_gate_up, w_down=w_down, b_down=b_down)
    n_c = BATCH + DEC_BATCH
    c_all = jnp.concatenate([c_prompt, c_sample, jnp.zeros((40 - n_c, D), F32)], axis=0)
    mod_all = _ada_mod(c_all, w_ada, b_ada)

    xp = x_prompt.reshape(MP, D)
    xs = x_sample.reshape(MS, D)
    new_p, new_s = [], []
    for l in range(DEPTH):
        mod_p = mod_all[l, :BATCH].reshape(BATCH, 6, 1, D).transpose(1, 0, 2, 3)
        mod_s = jnp.repeat(mod_all[l, BATCH:n_c].reshape(DEC_BATCH, 6, D), DEC_SEQ, axis=0)
        mod_s = mod_s.transpose(1, 0, 2).reshape(6, 1, MS, D)
        st = (cache_k[l], cache_v[l], state_delta[l], state_delta_conv[l], state_lru[l],
              state_lru_conv[l])
        xp, xs, st_p, st_s = _layer(l, xp, xs, st, w, mod_p, mod_s)
        new_p.append(st_p)
        new_s.append(st_s)
    k_p, v_p, d_p, dc_p, h_p, hc_p = (jnp.stack(z) for z in zip(*new_p))
    k_s, v_s, d_s, dc_s, h_s, hc_s = (jnp.stack(z) for z in zip(*new_s))
    return (xp.reshape(BATCH, SEQ, D), xs.reshape(DEC_BATCH, DEC_SEQ, D),
            k_p, v_p, d_p, dc_p, h_p, hc_p, k_s, v_s, d_s, dc_s, h_s, hc_s)
```

```python
import functools

import jax
import jax.numpy as jnp
from jax import lax
from jax.experimental import pallas as pl
from jax.experimental.pallas import tpu as pltpu

F32 = jnp.float32
BF16 = jnp.bfloat16
I32 = jnp.int32

D = 2048
BATCH = 2
SEQ = 4096
DEC_BATCH = 32
DEC_SEQ = 4
DEPTH = 2
MP = BATCH * SEQ
MS = DEC_BATCH * DEC_SEQ
NTOK = MP + MS

WINDOW = 128
HD = 64
NH = 16
NKV = 4
GRP = NH // NKV
ATT_SCALE = HD ** -0.5

DK = 128
DNH = 8
CHUNK = 64
CONV_CH = 3 * DNH * DK
CONV_W = 4

LW = 1024
LBLK = 128
LNB = LW // LBLK
LRU_C = 8.0

NE = 32
TOPK = 4
DFF = 2048
SW_ALPHA = 1.702
SW_LIMIT = 7.0
EPS = 1e-6

P_DQKV = 0
P_G = 3072
P_DZ = 9216
P_LX = 10240
P_Q = 11264
P_K = 12288
P_V = 12544
P_W = 12800

RB = 256
NB_MAX = -(-(NTOK * TOPK) // RB) + NE
ROWS = NB_MAX * RB
TN_E = 1024
NJ = DFF // TN_E
TOK_TILE = 128
N_TT = NTOK // TOK_TILE

VMEM_BIG = 56 * 1024 * 1024


def _sds(shape, dtype):
    return jax.ShapeDtypeStruct(shape, dtype)


def _cparams(n_axes, vmem=None):
    return pltpu.CompilerParams(dimension_semantics=("arbitrary",) * n_axes, vmem_limit_bytes=vmem)


def _dot(a, b):
    return jnp.dot(a, b, preferred_element_type=F32)


def _dot_nt(a, b):
    return lax.dot_general(a, b, (((1,), (1,)), ((), ())), preferred_element_type=F32)


def _dot_tn(a, b):
    return lax.dot_general(a, b, (((0,), (0,)), ((), ())), preferred_element_type=F32)


def _rms(x, g):
    return x * lax.rsqrt(jnp.mean(x * x, axis=-1, keepdims=True) + EPS) * g


def _sigmoid(x):
    return 1.0 / (1.0 + jnp.exp(-x))


def _silu(x):
    return x * _sigmoid(x)


def _softplus(x):
    return jnp.maximum(x, 0.0) + jnp.log1p(jnp.exp(-jnp.abs(x)))


def _ada_kernel(c_ref, w_ref, b_ref, o_ref):
    a = _silu(c_ref[...]).astype(BF16)
    o_ref[0] = _dot(a, w_ref[0].astype(BF16)) + b_ref[0]


def _ada_mod(c_all, w_ada, b_ada):
    rows = c_all.shape[0]
    tn = 1024
    return pl.pallas_call(
        _ada_kernel,
        grid=(DEPTH, 6 * D // tn),
        in_specs=[
            pl.BlockSpec((rows, D), lambda l, n: (0, 0)),
            pl.BlockSpec((1, D, tn), lambda l, n: (l, 0, n)),
            pl.BlockSpec((1, 1, tn), lambda l, n: (l, 0, n)),
        ],
        out_specs=pl.BlockSpec((1, rows, tn), lambda l, n: (l, 0, n)),
        out_shape=_sds((DEPTH, rows, 6 * D), F32),
        compiler_params=_cparams(2, VMEM_BIG),
        name="ada_mod",
    )(c_all, w_ada, b_ada.reshape(DEPTH, 1, 6 * D))


def _adaln_kernel(x_ref, g_ref, sh_ref, sc_ref, o_ref):
    y = _rms(x_ref[...], g_ref[...])
    o_ref[...] = (y * (1.0 + sc_ref[0, 0]) + sh_ref[0, 0]).astype(o_ref.dtype)


def _mod_spec(mod, j, tiles, tn=D, with_n=False):
    g, rb = mod.shape[1], mod.shape[2]
    tpg = tiles // g
    if with_n:
        return pl.BlockSpec((1, 1, rb, tn), lambda n, m: (j, m // tpg, 0, n))
    return pl.BlockSpec((1, 1, rb, tn), lambda m: (j, m // tpg, 0, 0))


def _adaln(x, g, mod, j_sh, j_sc, tm):
    m_rows = x.shape[0]
    tiles = m_rows // tm
    return pl.pallas_call(
        _adaln_kernel,
        grid=(tiles,),
        in_specs=[
            pl.BlockSpec((tm, D), lambda m: (m, 0)),
            pl.BlockSpec((1, D), lambda m: (0, 0)),
            _mod_spec(mod, j_sh, tiles),
            _mod_spec(mod, j_sc, tiles),
        ],
        out_specs=pl.BlockSpec((tm, D), lambda m: (m, 0)),
        out_shape=_sds((m_rows, D), BF16),
        compiler_params=_cparams(1),
        name="adaln1",
    )(x, g, mod, mod)


def _mm_kernel(x_ref, w_ref, o_ref):
    o_ref[...] = _dot(x_ref[...], w_ref[...]).astype(o_ref.dtype)


def _matmul(x, w, out_dtype, tm, tn, name):
    m_rows, k = x.shape
    n_cols = w.shape[1]
    return pl.pallas_call(
        _mm_kernel,
        grid=(n_cols // tn, m_rows // tm),
        in_specs=[
            pl.BlockSpec((tm, k), lambda n, m: (m, 0)),
            pl.BlockSpec((k, tn), lambda n, m: (0, n)),
        ],
        out_specs=pl.BlockSpec((tm, tn), lambda n, m: (m, n)),
        out_shape=_sds((m_rows, n_cols), out_dtype),
        compiler_params=_cparams(2, VMEM_BIG),
        name=name,
    )(x, w)


def _attn_prompt_kernel(sinks_ref, q_ref, kc_ref, vc_ref, kp_ref, vp_ref, qg_ref, kg_ref,
                        o_ref, ko_ref, vo_ref):
    nblk = SEQ // WINDOW
    first = (pl.program_id(0) % nblk) == 0
    row = lax.broadcasted_iota(I32, (WINDOW, 2 * WINDOW), 0)
    col = lax.broadcasted_iota(I32, (WINDOW, 2 * WINDOW), 1)
    lo = jnp.where(first, WINDOW, 0)
    mask = (col > row) & (col <= row + WINDOW) & (col >= lo)
    qg = qg_ref[...]
    kg = kg_ref[...]
    outs = []
    k_out = []
    for kh in range(NKV):
        ks = slice(kh * HD, (kh + 1) * HD)
        kc_n = _rms(kc_ref[:, ks], kg)
        kp_n = _rms(kp_ref[:, ks], kg)
        k_out.append(kc_n)
        kk = jnp.concatenate([kp_n, kc_n], axis=0).astype(BF16)
        vv = jnp.concatenate([vp_ref[:, ks], vc_ref[:, ks]], axis=0).astype(BF16)
        for g in range(GRP):
            h = kh * GRP + g
            qh = _rms(q_ref[:, h * HD:(h + 1) * HD], qg).astype(BF16)
            s = _dot_nt(qh, kk) * ATT_SCALE
            s = jnp.where(mask, s, -jnp.inf)
            sink = sinks_ref[h]
            m = jnp.maximum(jnp.max(s, axis=-1, keepdims=True), sink)
            p = jnp.exp(s - m)
            den = jnp.sum(p, axis=-1, keepdims=True) + jnp.exp(sink - m)
            outs.append(_dot(p.astype(BF16), vv) / den)
    o_ref[...] = jnp.concatenate(outs, axis=-1).astype(o_ref.dtype)
    ko_ref[0] = jnp.concatenate(k_out, axis=-1)
    vo_ref[0] = vc_ref[...]


def _attn_prompt(p_act, qg, kg, sinks):
    nblk = SEQ // WINDOW
    cq, ck, cv = P_Q // 1024, P_K // 256, P_V // 256
    return pl.pallas_call(
        _attn_prompt_kernel,
        grid=(MP // WINDOW,),
        in_specs=[
            pl.BlockSpec(memory_space=pltpu.SMEM),
            pl.BlockSpec((WINDOW, 1024), lambda g: (g, cq)),
            pl.BlockSpec((WINDOW, 256), lambda g: (g, ck)),
            pl.BlockSpec((WINDOW, 256), lambda g: (g, cv)),
            pl.BlockSpec((WINDOW, 256), lambda g: (jnp.maximum(g - 1, 0), ck)),
            pl.BlockSpec((WINDOW, 256), lambda g: (jnp.maximum(g - 1, 0), cv)),
            pl.BlockSpec((1, HD), lambda g: (0, 0)),
            pl.BlockSpec((1, HD), lambda g: (0, 0)),
        ],
        out_specs=[
            pl.BlockSpec((WINDOW, 1024), lambda g: (g, 0)),
            pl.BlockSpec((1, WINDOW, 256), lambda g: (g // nblk, 0, 0)),
            pl.BlockSpec((1, WINDOW, 256), lambda g: (g // nblk, 0, 0)),
        ],
        out_shape=[_sds((MP, 1024), BF16), _sds((BATCH, WINDOW, 256), F32),
                   _sds((BATCH, WINDOW, 256), F32)],
        compiler_params=_cparams(1),
        name="attn_prompt",
    )(sinks, p_act, p_act, p_act, p_act, p_act, qg, kg)


def _attn_sample_kernel(sinks_ref, q_ref, kn_ref, vn_ref, ck_ref, cv_ref, qg_ref, kg_ref,
                        o_ref, kno_ref):
    rows = GRP * DEC_SEQ
    t = lax.broadcasted_iota(I32, (rows, 1), 0) % DEC_SEQ
    g_of_row = lax.broadcasted_iota(I32, (rows, 1), 0) // DEC_SEQ
    col_c = lax.broadcasted_iota(I32, (rows, WINDOW), 1)
    col_n = lax.broadcasted_iota(I32, (rows, 8), 1)
    qg = qg_ref[...]
    kg = kg_ref[...]
    for kh in range(NKV):
        q16 = _rms(q_ref[0, kh], qg).astype(BF16)
        kn = _rms(kn_ref[0, kh], kg)
        kno_ref[0, kh] = kn
        s_c = _dot_nt(q16, ck_ref[0, kh].astype(BF16)) * ATT_SCALE
        s_n = _dot_nt(q16, kn.astype(BF16)) * ATT_SCALE
        s_c = jnp.where(col_c > t, s_c, -jnp.inf)
        s_n = jnp.where(col_n <= t, s_n, -jnp.inf)
        sink = jnp.zeros((rows, 1), F32)
        for g in range(GRP):
            sink = jnp.where(g_of_row == g, sinks_ref[kh * GRP + g], sink)
        m = jnp.maximum(jnp.maximum(jnp.max(s_c, axis=-1, keepdims=True),
                                    jnp.max(s_n, axis=-1, keepdims=True)), sink)
        p_c = jnp.exp(s_c - m)
        p_n = jnp.exp(s_n - m)
        den = (jnp.sum(p_c, axis=-1, keepdims=True) + jnp.sum(p_n, axis=-1, keepdims=True)
               + jnp.exp(sink - m))
        o = _dot(p_c.astype(BF16), cv_ref[0, kh].astype(BF16)) + _dot(
            p_n.astype(BF16), vn_ref[0, kh].astype(BF16))
        o_ref[0, kh] = o / den


def _attn_sample(q_r, kn_r, vn_r, ck_r, cv_r, qg, kg, sinks):
    rows = GRP * DEC_SEQ
    return pl.pallas_call(
        _attn_sample_kernel,
        grid=(DEC_BATCH,),
        in_specs=[
            pl.BlockSpec(memory_space=pltpu.SMEM),
            pl.BlockSpec((1, NKV, rows, HD), lambda b: (b, 0, 0, 0)),
            pl.BlockSpec((1, NKV, 8, HD), lambda b: (b, 0, 0, 0)),
            pl.BlockSpec((1, NKV, 8, HD), lambda b: (b, 0, 0, 0)),
            pl.BlockSpec((1, NKV, WINDOW, HD), lambda b: (b, 0, 0, 0)),
            pl.BlockSpec((1, NKV, WINDOW, HD), lambda b: (b, 0, 0, 0)),
            pl.BlockSpec((1, HD), lambda b: (0, 0)),
            pl.BlockSpec((1, HD), lambda b: (0, 0)),
        ],
        out_specs=[
            pl.BlockSpec((1, NKV, rows, HD), lambda b: (b, 0, 0, 0)),
            pl.BlockSpec((1, NKV, 8, HD), lambda b: (b, 0, 0, 0)),
        ],
        out_shape=[_sds((DEC_BATCH, NKV, rows, HD), F32), _sds((DEC_BATCH, NKV, 8, HD), F32)],
        compiler_params=_cparams(1),
        name="attn_sample",
    )(sinks, q_r, kn_r, vn_r, ck_r, cv_r, qg, kg)


def _cumsum_rows(x, c):
    row = lax.broadcasted_iota(I32, x.shape, 0)
    s = 1
    while s < c:
        x = x + jnp.where(row >= s, pltpu.roll(x, s, 0), 0.0)
        s *= 2
    return x


def _delta_chunk(c, nstack, conv_slice, g_full, beta_full, z_slice, norm_g, s_get, s_put, o_put):
    n = nstack * c
    lg = c.bit_length() - 1
    r = lax.broadcasted_iota(I32, (n, n), 0)
    cc = lax.broadcasted_iota(I32, (n, n), 1)
    same = (r >> lg) == (cc >> lg)
    incl = same & (r >= cc)
    strict = same & (r > cc)
    eye_b = r == cc
    eye = eye_b.astype(F32)
    gc = _cumsum_rows(g_full, c)
    for st in range(DNH // nstack):
        heads = range(st * nstack, (st + 1) * nstack)
        qs, ks, vs, gcs, bs = [], [], [], [], []
        for h in heads:
            qh = conv_slice(slice(h * DK, (h + 1) * DK))
            kh = conv_slice(slice(DNH * DK + h * DK, DNH * DK + (h + 1) * DK))
            vs.append(conv_slice(slice(2 * DNH * DK + h * DK, 2 * DNH * DK + (h + 1) * DK)))
            qs.append(qh * lax.rsqrt(jnp.sum(qh * qh, axis=-1, keepdims=True) + EPS) * (DK ** -0.5))
            ks.append(kh * lax.rsqrt(jnp.sum(kh * kh, axis=-1, keepdims=True) + EPS))
            gcs.append(gc[:, h:h + 1])
            bs.append(beta_full[:, DNH + h:DNH + h + 1])
        q = jnp.concatenate(qs, axis=0)
        k = jnp.concatenate(ks, axis=0)
        v = jnp.concatenate(vs, axis=0)
        gcol = jnp.concatenate(gcs, axis=0)
        bcol = jnp.concatenate(bs, axis=0)
        grow = jnp.sum(jnp.where(eye_b, gcol, 0.0), axis=0, keepdims=True)
        decay = jnp.exp(jnp.where(incl, gcol - grow, -jnp.inf))
        egc = jnp.exp(gcol)
        kb = k * bcol
        k16 = k.astype(BF16)
        a_mat = jnp.where(strict, _dot_nt(kb.astype(BF16), k16) * decay, 0.0)
        blk = 1
        t_inv = eye
        while blk < c:
            sh = blk.bit_length()
            pair = ((r >> sh) == (cc >> sh)) & ((r & blk) != 0) & ((cc & blk) == 0)
            off = jnp.where(pair, a_mat, 0.0)
            if blk == 1:
                t_inv = t_inv - off
            else:
                t16 = t_inv.astype(BF16)
                t_inv = t_inv - _dot(t16, _dot(off.astype(BF16), t16).astype(BF16))
            blk *= 2
        rhs = jnp.concatenate([v * bcol, kb * egc], axis=-1).astype(BF16)
        sol = _dot(t_inv.astype(BF16), rhs)
        qk16 = (_dot_nt(q.astype(BF16), k16) * decay).astype(BF16)
        q_dec = q * egc
        us, s_olds, g_lasts = [], [], []
        for i, h in enumerate(heads):
            hs = slice(i * c, (i + 1) * c)
            s_old = s_get(h)
            us.append(sol[hs, :DK] - _dot(sol[hs, DK:].astype(BF16), s_old.astype(BF16)))
            s_olds.append(s_old)
            g_lasts.append(gcol[(i + 1) * c - 1:(i + 1) * c, :])
        u16 = jnp.concatenate(us, axis=0).astype(BF16)
        o_in = _dot(qk16, u16)
        for i, h in enumerate(heads):
            hs = slice(i * c, (i + 1) * c)
            s16 = s_olds[i].astype(BF16)
            o = _dot(q_dec[hs].astype(BF16), s16) + o_in[hs]
            k_dec = k[hs] * jnp.exp(g_lasts[i] - gcol[hs])
            s_put(h, s_olds[i] * jnp.exp(g_lasts[i]) + _dot_tn(k_dec.astype(BF16), u16[hs]))
            zz = z_slice(slice(h * DK, (h + 1) * DK))
            o_put(h, _rms(o, norm_g) * _silu(zz))


def _delta_prompt_kernel(x_ref, prev_ref, dz_ref, dab_ref, cw_ref, alog_ref, dtb_ref, ng_ref,
                         o_ref, s_ref, dc_ref):
    c = pl.program_id(0)

    @pl.when(c == 0)
    def _():
        s_ref[...] = jnp.zeros_like(s_ref)

    row8 = lax.broadcasted_iota(I32, (8, DK), 0)
    keep_prev = c > 0
    for b in range(BATCH):
        def conv_slice(cs, b=b):
            x = x_ref[b, :, cs]
            prev = jnp.where(keep_prev, prev_ref[b, :, cs], 0.0)
            y = x * cw_ref[CONV_W - 1:CONV_W, cs]
            for s in range(1, CONV_W):
                xr = pltpu.roll(x, s, 0)
                top = jnp.where(row8 < s, pltpu.roll(prev, s, 0), xr[0:8])
                sh = jnp.concatenate([top, xr[8:]], axis=0)
                y = y + sh * cw_ref[CONV_W - 1 - s:CONV_W - s, cs]
            return _silu(y)

        dab = dab_ref[b]
        g_full = -jnp.exp(alog_ref[...]) * _softplus(dab + dtb_ref[...])
        beta_full = _sigmoid(dab)

        def s_get(h, b=b):
            return s_ref[b, h]

        def s_put(h, v, b=b):
            s_ref[b, h] = v

        def o_put(h, v, b=b):
            o_ref[b, :, h * DK:(h + 1) * DK] = v.astype(o_ref.dtype)

        _delta_chunk(CHUNK, 4, conv_slice, g_full, beta_full, lambda cs, b=b: dz_ref[b, :, cs],
                     ng_ref[...], s_get, s_put, o_put)

    @pl.when(c == pl.num_programs(0) - 1)
    def _():
        dc_ref[...] = x_ref[:, CHUNK - (CONV_W - 1):CHUNK, :]


def _delta_prompt(p_act, p_ab, conv_w, alog_row, dtb_row, norm_g):
    nck = SEQ // CHUNK
    p3 = p_act.reshape(BATCH, SEQ, p_act.shape[1])
    ab3 = p_ab.reshape(BATCH, SEQ, 128)
    return pl.pallas_call(
        _delta_prompt_kernel,
        grid=(nck,),
        in_specs=[
            pl.BlockSpec((BATCH, CHUNK, CONV_CH), lambda c: (0, c, P_DQKV // CONV_CH)),
            pl.BlockSpec((BATCH, 8, CONV_CH),
                         lambda c: (0, jnp.maximum(c * (CHUNK // 8) - 1, 0), P_DQKV // CONV_CH)),
            pl.BlockSpec((BATCH, CHUNK, 1024), lambda c: (0, c, P_DZ // 1024)),
            pl.BlockSpec((BATCH, CHUNK, 128), lambda c: (0, c, 0)),
            pl.BlockSpec((CONV_W, CONV_CH), lambda c: (0, 0)),
            pl.BlockSpec((1, 128), lambda c: (0, 0)),
            pl.BlockSpec((1, 128), lambda c: (0, 0)),
            pl.BlockSpec((1, DK), lambda c: (0, 0)),
        ],
        out_specs=[
            pl.BlockSpec((BATCH, CHUNK, 1024), lambda c: (0, c, 0)),
            pl.BlockSpec((BATCH, DNH, DK, DK), lambda c: (0, 0, 0, 0)),
            pl.BlockSpec((BATCH, CONV_W - 1, CONV_CH), lambda c: (0, 0, 0)),
        ],
        out_shape=[_sds((BATCH, SEQ, 1024), BF16), _sds((BATCH, DNH, DK, DK), F32),
                   _sds((BATCH, CONV_W - 1, CONV_CH), F32)],
        compiler_params=_cparams(1),
        name="delta_prompt",
    )(p3, p3, p3, ab3, conv_w, alog_row, dtb_row, norm_g)


def _delta_sample_kernel(xp_ref, dz_ref, dab_ref, s0_ref, cw_ref, alog_ref, dtb_ref, ng_ref,
                         o_ref, s_ref):
    row = lax.broadcasted_iota(I32, (8, DK), 0)
    live = row < DEC_SEQ

    def conv_slice(cs):
        xp = xp_ref[0, :, cs]
        y = jnp.zeros((8, DK), F32)
        for j in range(CONV_W):
            y = y + pltpu.roll(xp, 8 - 1 - j, 0) * cw_ref[j:j + 1, cs]
        return jnp.where(live, _silu(y), 0.0)

    dab = dab_ref[0]
    g_full = jnp.where(live, -jnp.exp(alog_ref[...]) * _softplus(dab + dtb_ref[...]), 0.0)
    beta_full = jnp.where(live, _sigmoid(dab), 0.0)

    def s_put(h, v):
        s_ref[0, h] = v

    def o_put(h, v):
        o_ref[0, :, h * DK:(h + 1) * DK] = v

    _delta_chunk(8, DNH, conv_slice, g_full, beta_full, lambda cs: dz_ref[0, :, cs], ng_ref[...],
                 lambda h: s0_ref[0, h], s_put, o_put)


def _delta_sample(xp8, dz8, dab8, s0, conv_w, alog_row, dtb_row, norm_g):
    return pl.pallas_call(
        _delta_sample_kernel,
        grid=(DEC_BATCH,),
        in_specs=[
            pl.BlockSpec((1, 8, CONV_CH), lambda b: (b, 0, 0)),
            pl.BlockSpec((1, 8, 1024), lambda b: (b, 0, 0)),
            pl.BlockSpec((1, 8, 128), lambda b: (b, 0, 0)),
            pl.BlockSpec((1, DNH, DK, DK), lambda b: (b, 0, 0, 0)),
            pl.BlockSpec((CONV_W, CONV_CH), lambda b: (0, 0)),
            pl.BlockSpec((1, 128), lambda b: (0, 0)),
            pl.BlockSpec((1, 128), lambda b: (0, 0)),
            pl.BlockSpec((1, DK), lambda b: (0, 0)),
        ],
        out_specs=[
            pl.BlockSpec((1, 8, 1024), lambda b: (b, 0, 0)),
            pl.BlockSpec((1, DNH, DK, DK), lambda b: (b, 0, 0, 0)),
        ],
        out_shape=[_sds((DEC_BATCH, 8, 1024), F32), _sds((DEC_BATCH, DNH, DK, DK), F32)],
        compiler_params=_cparams(1),
        name="delta_sample",
    )(xp8, dz8, dab8, s0, conv_w, alog_row, dtb_row, norm_g)


def _lru_gates(xc, wr, br, wi, bi, sp):
    x16 = xc.astype(BF16)
    r = _sigmoid(_dot(x16, wr) + br)
    i = _sigmoid(_dot(x16, wi) + bi)
    log_a = -LRU_C * r * sp
    a = jnp.exp(log_a)
    th = jnp.tanh(log_a)
    u = jnp.sqrt(-2.0 * th / (1.0 - th)) * (i * xc)
    return a, u


LRU_TT = 256


def _lru_prompt_kernel(x_ref, prev_ref, cw_ref, cb_ref, wr_ref, br_ref, wi_ref, bi_ref, lam_ref,
                       o_ref, h_ref, lc_ref):
    t = pl.program_id(1)
    row8 = lax.broadcasted_iota(I32, (8, LBLK), 0)
    row = lax.broadcasted_iota(I32, (LRU_TT, LBLK), 0)
    keep = t > 0
    for n in range(LNB):
        cs = slice(n * LBLK, (n + 1) * LBLK)
        x = x_ref[:, cs]
        prev = jnp.where(keep, prev_ref[:, cs], 0.0)
        y = x * cw_ref[CONV_W - 1:CONV_W, cs]
        for s in range(1, CONV_W):
            xr = pltpu.roll(x, s, 0)
            top = jnp.where(row8 < s, pltpu.roll(prev, s, 0), xr[0:8])
            y = y + jnp.concatenate([top, xr[8:]], axis=0) * cw_ref[CONV_W - 1 - s:CONV_W - s, cs]
        xc = y + cb_ref[:, cs]
        sp = _softplus(-lam_ref[:, cs])
        a, u = _lru_gates(xc, wr_ref[n], br_ref[:, cs], wi_ref[n], bi_ref[:, cs], sp)
        h0 = jnp.where(keep, h_ref[0, :, cs], 0.0)
        u = u + jnp.where(row == 0, a * h0, 0.0)
        s = 1
        while s < LRU_TT:
            valid = row >= s
            u_s = pltpu.roll(u, s, 0)
            a_s = pltpu.roll(a, s, 0)
            u = jnp.where(valid, a * u_s + u, u)
            a = jnp.where(valid, a * a_s, a)
            s *= 2
        o_ref[:, cs] = u.astype(o_ref.dtype)
        h_ref[0, :, cs] = u[LRU_TT - 1:LRU_TT, :]

    @pl.when(t == pl.num_programs(1) - 1)
    def _():
        lc_ref[0] = x_ref[LRU_TT - (CONV_W - 1):LRU_TT, :]


def _lru_prompt(p_act, cw, cb, wr, br, wi, bi, lam):
    ntt = SEQ // LRU_TT
    cl = P_LX // LW
    vec = pl.BlockSpec((1, LW), lambda b, t: (0, 0))
    mat = pl.BlockSpec((LNB, LBLK, LBLK), lambda b, t: (0, 0, 0))
    return pl.pallas_call(
        _lru_prompt_kernel,
        grid=(BATCH, ntt),
        in_specs=[
            pl.BlockSpec((LRU_TT, LW), lambda b, t: (b * ntt + t, cl)),
            pl.BlockSpec((8, LW),
                         lambda b, t: (jnp.maximum(b * (SEQ // 8) + t * (LRU_TT // 8) - 1, 0), cl)),
            pl.BlockSpec((CONV_W, LW), lambda b, t: (0, 0)),
            vec, mat, vec, mat, vec, vec,
        ],
        out_specs=[
            pl.BlockSpec((LRU_TT, LW), lambda b, t: (b * ntt + t, 0)),
            pl.BlockSpec((1, 1, LW), lambda b, t: (b, 0, 0)),
            pl.BlockSpec((1, CONV_W - 1, LW), lambda b, t: (b, 0, 0)),
        ],
        out_shape=[_sds((MP, LW), BF16), _sds((BATCH, 1, LW), F32),
                   _sds((BATCH, CONV_W - 1, LW), F32)],
        compiler_params=_cparams(2),
        name="lru_prompt",
    )(p_act, p_act, cw, cb, wr, br, wi, bi, lam)


def _lru_sample_kernel(x_ref, buf_ref, h0_ref, cw_ref, cb_ref, wr_ref, br_ref, wi_ref, bi_ref,
                       lam_ref, o_ref):
    for n in range(LNB):
        cs = slice(n * LBLK, (n + 1) * LBLK)
        xp = [buf_ref[j, :, cs] for j in range(CONV_W - 1)] + [x_ref[t, :, cs] for t in range(DEC_SEQ)]
        xcs = []
        for t in range(DEC_SEQ):
            y = xp[t] * cw_ref[0:1, cs]
            for j in range(1, CONV_W):
                y = y + xp[t + j] * cw_ref[j:j + 1, cs]
            xcs.append(y + cb_ref[:, cs])
        xc = jnp.concatenate(xcs, axis=0)
        sp = _softplus(-lam_ref[:, cs])
        a, u = _lru_gates(xc, wr_ref[n], br_ref[:, cs], wi_ref[n], bi_ref[:, cs], sp)
        h = h0_ref[:, cs]
        for t in range(DEC_SEQ):
            rs = slice(t * DEC_BATCH, (t + 1) * DEC_BATCH)
            h = a[rs] * h + u[rs]
            o_ref[t, :, cs] = h


def _lru_sample(x_tm, buf_tm, h0, cw, cb, wr, br, wi, bi, lam):
    return pl.pallas_call(
        _lru_sample_kernel,
        out_shape=_sds((DEC_SEQ, DEC_BATCH, LW), F32),
        name="lru_sample",
    )(x_tm, buf_tm, h0, cw, cb, wr, br, wi, bi, lam)


def _merge_kernel(oa_ref, ob_ref, oc_ref, w_ref, ga_ref, gb_ref, gc_ref, o_ref):
    acc = _sigmoid(ga_ref[...]) * _dot(oa_ref[...], w_ref[0])
    acc = acc + _sigmoid(gb_ref[...]) * _dot(ob_ref[...], w_ref[1])
    acc = acc + _sigmoid(gc_ref[...]) * _dot(oc_ref[...], w_ref[2])
    o_ref[...] = acc.astype(o_ref.dtype)


def _merge(o_a, o_b, o_c, wb, p_act, tm):
    m_rows = o_a.shape[0]
    tn = 512
    g0 = P_G // tn
    gs = D // tn
    br = pl.BlockSpec((tm, 1024), lambda n, m: (m, 0))
    return pl.pallas_call(
        _merge_kernel,
        grid=(D // tn, m_rows // tm),
        in_specs=[
            br, br, br,
            pl.BlockSpec((3, 1024, tn), lambda n, m: (0, 0, n)),
            pl.BlockSpec((tm, tn), lambda n, m: (m, g0 + n)),
            pl.BlockSpec((tm, tn), lambda n, m: (m, g0 + gs + n)),
            pl.BlockSpec((tm, tn), lambda n, m: (m, g0 + 2 * gs + n)),
        ],
        out_specs=pl.BlockSpec((tm, tn), lambda n, m: (m, n)),
        out_shape=_sds((m_rows, D), BF16),
        compiler_params=_cparams(2, VMEM_BIG),
        name="merge",
    )(o_a, o_b, o_c, wb, p_act, p_act, p_act)


def _outproj_kernel(a_ref, w_ref, x_ref, gt_ref, o_ref):
    o_ref[...] = x_ref[...] + gt_ref[0, 0] * _dot(a_ref[...], w_ref[...])


def _outproj(merged, w_out, x, mod, tm):
    m_rows = x.shape[0]
    tn = 512
    tiles = m_rows // tm
    return pl.pallas_call(
        _outproj_kernel,
        grid=(D // tn, tiles),
        in_specs=[
            pl.BlockSpec((tm, D), lambda n, m: (m, 0)),
            pl.BlockSpec((D, tn), lambda n, m: (0, n)),
            pl.BlockSpec((tm, tn), lambda n, m: (m, n)),
            _mod_spec(mod, 2, tiles, tn, with_n=True),
        ],
        out_specs=pl.BlockSpec((tm, tn), lambda n, m: (m, n)),
        out_shape=_sds((m_rows, D), F32),
        compiler_params=_cparams(2, VMEM_BIG),
        name="outproj",
    )(merged, w_out, x, mod)


def _router_kernel(xp_ref, xs_ref, g_ref, shp_ref, scp_ref, shs_ref, scs_ref, wr_ref, br_ref,
                   h_ref, te_ref, tw_ref):
    tm = TOK_TILE
    is_s = pl.program_id(0) == N_TT - 1
    x = jnp.where(is_s, xs_ref[...], xp_ref[...])
    sc = jnp.where(is_s, scs_ref[0, 0], scp_ref[0, 0])
    sh = jnp.where(is_s, shs_ref[0, 0], shp_ref[0, 0])
    h = _rms(x, g_ref[...]) * (1.0 + sc) + sh
    h_ref[...] = h
    lane = lax.broadcasted_iota(I32, (tm, 128), 1)
    logits = _dot(h.astype(BF16), wr_ref[...]) + br_ref[...]
    logits = jnp.where(lane < NE, logits, -jnp.inf)
    lane4 = lax.broadcasted_iota(I32, (tm, TOPK), 1)
    te = jnp.zeros((tm, TOPK), I32)
    tl = jnp.zeros((tm, TOPK), F32)
    for k in range(TOPK):
        m = jnp.max(logits, axis=-1, keepdims=True)
        idx = jnp.min(jnp.where(logits == m, lane, 128), axis=-1, keepdims=True)
        te = jnp.where(lane4 == k, idx, te)
        tl = jnp.where(lane4 == k, m, tl)
        logits = jnp.where(lane == idx, -jnp.inf, logits)
    e = jnp.exp(tl - tl[:, 0:1])
    te_ref[...] = te
    tw_ref[...] = e / jnp.sum(e, axis=-1, keepdims=True)


def _router(x1p, x1s, g, mod_p, mod_s, wr, br):
    np_t = MP // TOK_TILE
    tpg = np_t // BATCH
    last_p = np_t - 1

    def mp(j):
        return pl.BlockSpec((1, 1, 1, D), lambda m: (j, jnp.minimum(m, last_p) // tpg, 0, 0))

    def ms(j):
        return pl.BlockSpec((1, 1, MS, D), lambda m: (j, 0, 0, 0))

    return pl.pallas_call(
        _router_kernel,
        grid=(N_TT,),
        in_specs=[
            pl.BlockSpec((TOK_TILE, D), lambda m: (jnp.minimum(m, last_p), 0)),
            pl.BlockSpec((MS, D), lambda m: (0, 0)),
            pl.BlockSpec((1, D), lambda m: (0, 0)),
            mp(3), mp(4), ms(3), ms(4),
            pl.BlockSpec((D, 128), lambda m: (0, 0)),
            pl.BlockSpec((1, 128), lambda m: (0, 0)),
        ],
        out_specs=[
            pl.BlockSpec((TOK_TILE, D), lambda m: (m, 0)),
            pl.BlockSpec((TOK_TILE, TOPK), lambda m: (m, 0)),
            pl.BlockSpec((TOK_TILE, TOPK), lambda m: (m, 0)),
        ],
        out_shape=[_sds((NTOK, D), F32), _sds((NTOK, TOPK), I32), _sds((NTOK, TOPK), F32)],
        compiler_params=_cparams(1),
        name="router",
    )(x1p, x1s, g, mod_p, mod_p, mod_s, mod_s, wr, br)


def _rank_kernel(te_ref, rk_ref, cnt_ref):
    @pl.when(pl.program_id(0) == 0)
    def _():
        cnt_ref[...] = jnp.zeros_like(cnt_ref)

    tt = TOK_TILE
    lane = lax.broadcasted_iota(I32, (tt, 128), 1)
    te = te_ref[...]
    hot = jnp.zeros((tt, 128), F32)
    for k in range(TOPK):
        hot = hot + (lane == te[:, k:k + 1]).astype(F32)
    r = lax.broadcasted_iota(I32, (tt, tt), 0)
    c = lax.broadcasted_iota(I32, (tt, tt), 1)
    before = (r > c).astype(BF16)
    tot = _dot(before, hot.astype(BF16)) + cnt_ref[...]
    lane4 = lax.broadcasted_iota(I32, (tt, TOPK), 1)
    rk = jnp.zeros((tt, TOPK), F32)
    for k in range(TOPK):
        rk_k = jnp.sum(jnp.where(lane == te[:, k:k + 1], tot, 0.0), axis=-1, keepdims=True)
        rk = jnp.where(lane4 == k, rk_k, rk)
    rk_ref[...] = rk.astype(I32)
    cnt_ref[...] = cnt_ref[...] + jnp.sum(hot, axis=0, keepdims=True)


def _rank(te):
    return pl.pallas_call(
        _rank_kernel,
        grid=(N_TT,),
        in_specs=[pl.BlockSpec((TOK_TILE, TOPK), lambda i: (i, 0))],
        out_specs=[pl.BlockSpec((TOK_TILE, TOPK), lambda i: (i, 0)),
                   pl.BlockSpec((1, 128), lambda i: (0, 0))],
        out_shape=[_sds((NTOK, TOPK), I32), _sds((1, 128), F32)],
        compiler_params=_cparams(1),
        name="rank",
    )(te)


def _scatter_kernel(dest_ref, last_ref, h_ref, xs_out, zbuf, zsem, sem):
    @pl.when(pl.program_id(0) == 0)
    def _():
        zbuf[...] = jnp.zeros_like(zbuf)

        def zero_block(row):
            return pltpu.make_async_copy(zbuf, xs_out.at[pl.ds(row, RB)], zsem)

        def extra_row(i):
            return pl.multiple_of(last_ref[NE] + i * RB, RB)

        n_extra = last_ref[NE + 1]
        for e in range(NE):
            zero_block(pl.multiple_of(last_ref[e], RB)).start()
        lax.fori_loop(0, n_extra, lambda i, c: (zero_block(extra_row(i)).start(), c)[1], 0)
        for e in range(NE):
            zero_block(pl.multiple_of(last_ref[e], RB)).wait()
        lax.fori_loop(0, n_extra, lambda i, c: (zero_block(extra_row(i)).wait(), c)[1], 0)

    def body(i, carry):
        for k in range(TOPK):
            d = dest_ref[0, 0, i * TOPK + k]
            pltpu.make_async_copy(h_ref.at[pl.ds(i, 1)], xs_out.at[pl.ds(d, 1)], sem).start(
                priority=k % 2)
        return carry

    lax.fori_loop(0, TOK_TILE, body, 0)
    for k in range(TOPK):
        pltpu.make_async_copy(h_ref, xs_out.at[pl.ds(0, TOK_TILE)], sem).wait()


def _scatter_rows(dest3, last_rows, h2):
    return pl.pallas_call(
        _scatter_kernel,
        grid=(N_TT,),
        in_specs=[
            pl.BlockSpec((1, 1, TOK_TILE * TOPK), lambda i: (i, 0, 0), memory_space=pltpu.SMEM),
            pl.BlockSpec(memory_space=pltpu.SMEM),
            pl.BlockSpec((TOK_TILE, D), lambda i: (i, 0)),
        ],
        out_specs=pl.BlockSpec(memory_space=pl.ANY),
        out_shape=_sds((ROWS, D), F32),
        scratch_shapes=[pltpu.VMEM((RB, D), F32), pltpu.SemaphoreType.DMA(()),
                        pltpu.SemaphoreType.DMA(())],
        compiler_params=_cparams(1),
        name="scatter_rows",
    )(dest3, last_rows, h2)


def _zero_unowned(zsrc, dst_at, used_ref, zsem):
    n = NB_MAX - used_ref[0]

    def cp(i):
        return pltpu.make_async_copy(zsrc, dst_at(used_ref[0] + i), zsem)

    lax.fori_loop(0, n, lambda i, c: (cp(i).start(), c)[1], 0)
    lax.fori_loop(0, n, lambda i, c: (cp(i).wait(), c)[1], 0)


def _row_of(blk):
    return pl.multiple_of(blk * RB, RB)


def _gmm_up_kernel(bstart, nblk, used, x_hbm, wg_ref, wl_ref, bg_ref, bl_ref, act_hbm,
                   wt_s, xbuf, obuf, xsem, osem, zsem):
    s = pl.program_id(0)
    e = s // NJ
    j = s % NJ
    nb = nblk[e]
    b0 = bstart[e]

    @pl.when(s == 0)
    def _():
        obuf[0] = jnp.zeros(obuf.shape[1:], obuf.dtype)
        for jj in range(NJ):
            _zero_unowned(obuf.at[0],
                          lambda blk, jj=jj: act_hbm.at[blk, pl.ds(jj * TN_E, TN_E)], used, zsem)

    def x_copy(i, slot):
        return pltpu.make_async_copy(x_hbm.at[pl.ds(_row_of(b0 + i), RB)], xbuf.at[slot],
                                     xsem.at[slot])

    def o_copy(i, slot):
        return pltpu.make_async_copy(
            obuf.at[slot], act_hbm.at[b0 + i, pl.ds(pl.multiple_of(j * TN_E, TN_E), TN_E)],
            osem.at[slot])

    @pl.when(nb > 0)
    def _():
        x_copy(0, 0).start()
        for c in range(D // 256):
            rs = slice(c * 256, (c + 1) * 256)
            wt_s[0:TN_E, rs] = wg_ref[0, 0, rs, :].T.astype(BF16)
            wt_s[TN_E:2 * TN_E, rs] = wl_ref[0, 0, rs, :].T.astype(BF16)

        def body(i, carry):
            slot = i % 2
            x_copy(i, slot).wait()

            @pl.when(i + 1 < nb)
            def _():
                x_copy(i + 1, 1 - slot).start()

            @pl.when(i >= 2)
            def _():
                o_copy(i - 2, slot).wait()

            yt = _dot_nt(wt_s[...], xbuf[slot].astype(BF16))
            glu = jnp.minimum(yt[0:TN_E] + bg_ref[0, 0], SW_LIMIT)
            lin = jnp.clip(yt[TN_E:2 * TN_E] + bl_ref[0, 0], -SW_LIMIT, SW_LIMIT)
            obuf[slot] = (glu * _sigmoid(SW_ALPHA * glu) * (lin + 1.0)).astype(obuf.dtype)
            o_copy(i, slot).start()
            return carry

        lax.fori_loop(0, nb, body, 0)

        @pl.when(nb >= 2)
        def _():
            o_copy(nb - 2, nb % 2).wait()

        o_copy(nb - 1, (nb - 1) % 2).wait()


def _gmm_up(tables, xs, w_gu, b_gu_col, layer):
    grid_spec = pltpu.PrefetchScalarGridSpec(
        num_scalar_prefetch=3,
        grid=(NE * NJ,),
        in_specs=[
            pl.BlockSpec(memory_space=pl.ANY),
            pl.BlockSpec((1, 1, D, TN_E), lambda s, *_: (layer, s // NJ, 0, s % NJ)),
            pl.BlockSpec((1, 1, D, TN_E), lambda s, *_: (layer, s // NJ, 0, NJ + s % NJ)),
            pl.BlockSpec((1, 1, TN_E, 1), lambda s, *_: (layer, s // NJ, s % NJ, 0)),
            pl.BlockSpec((1, 1, TN_E, 1), lambda s, *_: (layer, s // NJ, NJ + s % NJ, 0)),
        ],
        out_specs=pl.BlockSpec(memory_space=pl.ANY),
        scratch_shapes=[pltpu.VMEM((2 * TN_E, D), BF16),
                        pltpu.VMEM((2, RB, D), F32), pltpu.VMEM((2, TN_E, RB), BF16),
                        pltpu.SemaphoreType.DMA((2,)), pltpu.SemaphoreType.DMA((2,)),
                        pltpu.SemaphoreType.DMA(())],
    )
    return pl.pallas_call(
        _gmm_up_kernel,
        grid_spec=grid_spec,
        out_shape=_sds((NB_MAX, DFF, RB), BF16),
        compiler_params=_cparams(1, VMEM_BIG),
        name="gmm_up",
    )(*tables, xs, w_gu, w_gu, b_gu_col, b_gu_col)


def _gmm_down_kernel(bstart, nblk, used, xt_hbm, w_ref, b_ref, out_hbm, wt_s, xbuf, obuf,
                     xsem, osem, zsem):
    e = pl.program_id(0)
    nb = nblk[e]
    b0 = bstart[e]

    @pl.when(e == 0)
    def _():
        obuf[0] = jnp.zeros(obuf.shape[1:], obuf.dtype)
        _zero_unowned(obuf.at[0], lambda blk: out_hbm.at[pl.ds(_row_of(blk), RB)], used, zsem)

    def x_copy(i, slot):
        return pltpu.make_async_copy(xt_hbm.at[b0 + i], xbuf.at[slot], xsem.at[slot])

    def o_copy(i, slot):
        return pltpu.make_async_copy(obuf.at[slot], out_hbm.at[pl.ds(_row_of(b0 + i), RB)],
                                     osem.at[slot])

    @pl.when(nb > 0)
    def _():
        x_copy(0, 0).start()
        for c in range(DFF // 256):
            rs = slice(c * 256, (c + 1) * 256)
            wt_s[:, rs] = w_ref[0, 0, rs, :].T.astype(BF16)

        def body(i, carry):
            slot = i % 2
            x_copy(i, slot).wait()

            @pl.when(i + 1 < nb)
            def _():
                x_copy(i + 1, 1 - slot).start()

            @pl.when(i >= 2)
            def _():
                o_copy(i - 2, slot).wait()

            obuf[slot] = _dot(wt_s[...], xbuf[slot]).T + b_ref[0, 0]
            o_copy(i, slot).start()
            return carry

        lax.fori_loop(0, nb, body, 0)

        @pl.when(nb >= 2)
        def _():
            o_copy(nb - 2, nb % 2).wait()

        o_copy(nb - 1, (nb - 1) % 2).wait()


def _gmm_down(tables, act_t, w_down, b_down, layer):
    grid_spec = pltpu.PrefetchScalarGridSpec(
        num_scalar_prefetch=3,
        grid=(NE,),
        in_specs=[
            pl.BlockSpec(memory_space=pl.ANY),
            pl.BlockSpec((1, 1, DFF, D), lambda e, *_: (layer, e, 0, 0)),
            pl.BlockSpec((1, 1, 1, D), lambda e, *_: (layer, e, 0, 0)),
        ],
        out_specs=pl.BlockSpec(memory_space=pl.ANY),
        scratch_shapes=[pltpu.VMEM((D, DFF), BF16), pltpu.VMEM((2, DFF, RB), BF16),
                        pltpu.VMEM((2, RB, D), F32),
                        pltpu.SemaphoreType.DMA((2,)), pltpu.SemaphoreType.DMA((2,)),
                        pltpu.SemaphoreType.DMA(())],
    )
    return pl.pallas_call(
        _gmm_down_kernel,
        grid_spec=grid_spec,
        out_shape=_sds((ROWS, D), F32),
        compiler_params=_cparams(1, VMEM_BIG),
        name="gmm_down",
    )(*tables, act_t, w_down, b_down)


def _combine_kernel(dest_ref, rows_hbm, tw_ref, x_ref, gt_ref, o_ref, buf, sem):
    def body(i, carry):
        for k in range(TOPK):
            d = dest_ref[0, 0, i * TOPK + k]
            pltpu.make_async_copy(rows_hbm.at[pl.ds(d, 1)], buf.at[k, pl.ds(i, 1)], sem).start(
                priority=k % 2)
        return carry

    lax.fori_loop(0, TOK_TILE, body, 0)
    for k in range(TOPK):
        pltpu.make_async_copy(rows_hbm.at[pl.ds(0, TOK_TILE)], buf.at[k], sem).wait()
    tw = tw_ref[...]
    y = tw[:, 0:1] * buf[0]
    for k in range(1, TOPK):
        y = y + tw[:, k:k + 1] * buf[k]
    o_ref[...] = x_ref[...] + gt_ref[0, 0] * y


def _combine(dest3, out_rows, tw, x1, mod, tile0):
    m_rows = x1.shape[0]
    tiles = m_rows // TOK_TILE
    return pl.pallas_call(
        _combine_kernel,
        grid=(tiles,),
        in_specs=[
            pl.BlockSpec((1, 1, TOK_TILE * TOPK), lambda m: (tile0 + m, 0, 0),
                         memory_space=pltpu.SMEM),
            pl.BlockSpec(memory_space=pl.ANY),
            pl.BlockSpec((TOK_TILE, TOPK), lambda m: (tile0 + m, 0)),
            pl.BlockSpec((TOK_TILE, D), lambda m: (m, 0)),
            _mod_spec(mod, 5, tiles),
        ],
        out_specs=pl.BlockSpec((TOK_TILE, D), lambda m: (m, 0)),
        out_shape=_sds((m_rows, D), F32),
        scratch_shapes=[pltpu.VMEM((TOPK, TOK_TILE, D), F32), pltpu.SemaphoreType.DMA(())],
        compiler_params=_cparams(1),
        name="combine",
    )(dest3, out_rows, tw, x1, mod)


def _routing_tables(te, rank, cnt):
    counts = cnt[0, :NE].astype(I32)
    nblk = (counts + RB - 1) // RB
    bstart = jnp.cumsum(nblk) - nblk
    used = jnp.sum(nblk)
    te_d = te.reshape(N_TT, TOK_TILE * TOPK)
    start_d = jnp.zeros_like(te_d)
    for e in range(NE):
        start_d = jnp.where(te_d == e, bstart[e] * RB, start_d)
    dest3 = (start_d + rank.reshape(N_TT, TOK_TILE * TOPK)).reshape(N_TT, 1, TOK_TILE * TOPK)
    empty = nblk == 0
    n_empty = jnp.sum(empty.astype(I32))
    last_blk = jnp.where(empty, used + jnp.cumsum(empty.astype(I32)) - 1, bstart + nblk - 1)
    extra0 = used + n_empty
    last_rows = jnp.concatenate([last_blk * RB, jnp.stack([extra0 * RB, NB_MAX - extra0])])
    tables = (bstart.astype(I32), nblk.astype(I32), used.reshape(1).astype(I32))
    return dest3, tables, last_rows.astype(I32)


def _moe(layer, h2_all, te_all, w_gu, b_gu, w_down, b_down):
    rank, cnt = _rank(te_all)
    dest3, steps, last_rows = _routing_tables(te_all, rank, cnt)
    xs = _scatter_rows(dest3, last_rows, h2_all)
    act_t = _gmm_up(steps, xs, w_gu, b_gu.reshape(DEPTH, NE, 2 * DFF, 1), layer)
    out_rows = _gmm_down(steps, act_t, w_down, b_down.reshape(DEPTH, NE, 1, D), layer)
    return dest3, out_rows


def _layer(l, xp, xs, st, w, mod_p, mod_s):
    cast = lambda a: a.astype(BF16)
    w_in = w["w_in"][l]
    w_main = cast(jnp.concatenate(
        [w_in[:, 1536:4608], w_in[:, 6672:12816], w_in[:, 4624:5648], w_in[:, 5648:6672],
         w_in[:, 0:1024], w_in[:, 1024:1280], w_in[:, 1280:1536]], axis=1))
    w_ab = cast(jnp.pad(w_in[:, 4608:4624], ((0, 0), (0, 112))))
    wb = cast(w["w_branch"][l])
    w_out = cast(w["w_out"][l])
    wr = cast(jnp.pad(w["w_router"][l], ((0, 0), (0, 128 - NE))))
    br = jnp.pad(w["b_router"][l], (0, 128 - NE)).reshape(1, 128)
    g1 = w["norm1_g"][l].reshape(1, D)
    g2 = w["norm2_g"][l].reshape(1, D)
    qg = w["q_norm_g"][l].reshape(1, HD)
    kg = w["k_norm_g"][l].reshape(1, HD)
    sinks = w["sinks"][l]
    dn_cw = w["dn_conv_w"][l]
    alog_row = jnp.pad(w["dn_a_log"][l], (0, 128 - DNH)).reshape(1, 128)
    dtb_row = jnp.pad(w["dn_dt_bias"][l], (0, 128 - DNH)).reshape(1, 128)
    dn_ng = w["dn_norm_g"][l].reshape(1, DK)
    l_cw = w["lru_conv_w"][l]
    l_cb = w["lru_conv_b"][l].reshape(1, LW)
    l_wr = cast(w["lru_w_r"][l])
    l_wi = cast(w["lru_w_i"][l])
    l_br = w["lru_b_r"][l].reshape(1, LW)
    l_bi = w["lru_b_i"][l].reshape(1, LW)
    l_lam = w["lru_lambda"][l].reshape(1, LW)

    h1p = _adaln(xp, g1, mod_p, 0, 1, 256)
    pp = _matmul(h1p, w_main, F32, 2048, 512, "proj")
    pab = _matmul(h1p, w_ab, F32, 2048, 128, "proj_ab")
    oa_p, k_p, v_p = _attn_prompt(pp, qg, kg, sinks)
    ob_p, d_p, dc_p = _delta_prompt(pp, pab, dn_cw, alog_row, dtb_row, dn_ng)
    ob_p = ob_p.reshape(MP, 1024)
    oc_p, h_p, hc_p = _lru_prompt(pp, l_cw, l_cb, l_wr, l_br, l_wi, l_bi, l_lam)
    mg_p = _merge(oa_p, ob_p, oc_p, wb, pp, 1024)
    x1p = _outproj(mg_p, w_out, xp, mod_p, 1024)

    ck, cv, s0, dconv0, h0, lconv0 = st
    h1s = _adaln(xs, g1, mod_s, 0, 1, MS)
    ps = _matmul(h1s, w_main, F32, MS, 512, "proj")
    psab = _matmul(h1s, w_ab, F32, MS, 128, "proj_ab")
    q_r = ps[:, P_Q:P_K].reshape(DEC_BATCH, DEC_SEQ, NKV, GRP, HD).transpose(0, 2, 3, 1, 4)
    q_r = q_r.reshape(DEC_BATCH, NKV, GRP * DEC_SEQ, HD)
    pad_t = ((0, 0), (0, 0), (0, 8 - DEC_SEQ), (0, 0))
    kn_r = jnp.pad(ps[:, P_K:P_V].reshape(DEC_BATCH, DEC_SEQ, NKV, HD).transpose(0, 2, 1, 3), pad_t)
    vn_raw = ps[:, P_V:P_W].reshape(DEC_BATCH, DEC_SEQ, NKV, HD)
    vn_r = jnp.pad(vn_raw.transpose(0, 2, 1, 3), pad_t)
    oa_r, kno = _attn_sample(q_r, kn_r, vn_r, ck.transpose(0, 2, 1, 3), cv.transpose(0, 2, 1, 3),
                             qg, kg, sinks)
    oa_s = oa_r.reshape(DEC_BATCH, NKV, GRP, DEC_SEQ, HD).transpose(0, 3, 1, 2, 4)
    oa_s = oa_s.reshape(MS, 1024).astype(BF16)
    k_s = jnp.concatenate([ck[:, DEC_SEQ:], kno[:, :, :DEC_SEQ].transpose(0, 2, 1, 3)], axis=1)
    v_s = jnp.concatenate([cv[:, DEC_SEQ:], vn_raw], axis=1)

    x_dq = ps[:, P_DQKV:P_DQKV + CONV_CH].reshape(DEC_BATCH, DEC_SEQ, CONV_CH)
    xp8 = jnp.concatenate([jnp.zeros((DEC_BATCH, 1, CONV_CH), F32), dconv0, x_dq], axis=1)
    pad8 = ((0, 0), (0, 8 - DEC_SEQ), (0, 0))
    dz8 = jnp.pad(ps[:, P_DZ:P_LX].reshape(DEC_BATCH, DEC_SEQ, 1024), pad8)
    dab8 = jnp.pad(psab.reshape(DEC_BATCH, DEC_SEQ, 128), pad8)
    ob_r, d_s = _delta_sample(xp8, dz8, dab8, s0, dn_cw, alog_row, dtb_row, dn_ng)
    ob_s = ob_r[:, :DEC_SEQ].reshape(MS, 1024).astype(BF16)
    dc_s = x_dq[:, DEC_SEQ - (CONV_W - 1):]

    x_lx = ps[:, P_LX:P_Q].reshape(DEC_BATCH, DEC_SEQ, LW)
    oc_tm = _lru_sample(x_lx.transpose(1, 0, 2), lconv0.transpose(1, 0, 2), h0, l_cw, l_cb,
                        l_wr, l_br, l_wi, l_bi, l_lam)
    oc_s = oc_tm.transpose(1, 0, 2).reshape(MS, LW).astype(BF16)
    h_s = oc_tm[DEC_SEQ - 1]
    hc_s = x_lx[:, DEC_SEQ - (CONV_W - 1):]

    mg_s = _merge(oa_s, ob_s, oc_s, wb, ps, MS)
    x1s = _outproj(mg_s, w_out, xs, mod_s, MS)

    h2_all, te_all, tw_all = _router(x1p, x1s, g2, mod_p, mod_s, wr, br)
    dest3, out_rows = _moe(l, h2_all, te_all, w["w_gate_up"], w["b_gate_up"],
                           w["w_down"], w["b_down"])
    np_t = MP // TOK_TILE
    xp_new = _combine(dest3, out_rows, tw_all, x1p, mod_p, 0)
    xs_new = _combine(dest3, out_rows, tw_all, x1s, mod_s, np_t)

    st_p = (k_p.reshape(BATCH, WINDOW, NKV, HD), v_p.reshape(BATCH, WINDOW, NKV, HD), d_p, dc_p,
            h_p.reshape(BATCH, LW), hc_p)
    st_s = (k_s, v_s, d_s, dc_s, h_s, hc_s)
    return xp_new, xs_new, st_p, st_s


def kernel(x_prompt, x_sample, cache_k, cache_v, state_delta, state_delta_conv, state_lru, state_lru_conv, c_prompt, c_sample, w_ada, b_ada, norm1_g, norm2_g, w_in, q_norm_g, k_norm_g, sinks, dn_conv_w, dn_a_log, dn_dt_bias, dn_norm_g, lru_conv_w, lru_conv_b, lru_w_r, lru_b_r, lru_w_i, lru_b_i, lru_lambda, w_branch, w_out, w_router, b_router, w_gate_up, b_gate_up, w_down, b_down):
    w = dict(w_in=w_in, w_branch=w_branch, w_out=w_out, w_router=w_router, b_router=b_router,
             norm1_g=norm1_g, norm2_g=norm2_g, q_norm_g=q_norm_g, k_norm_g=k_norm_g, sinks=sinks,
             dn_conv_w=dn_conv_w, dn_a_log=dn_a_log, dn_dt_bias=dn_dt_bias, dn_norm_g=dn_norm_g,
             lru_conv_w=lru_conv_w, lru_conv_b=lru_conv_b, lru_w_r=lru_w_r, lru_b_r=lru_b_r,
             lru_w_i=lru_w_i, lru_b_i=lru_b_i, lru_lambda=lru_lambda, w_gate_up=w_gate_up,
             b_gate_up=b_gate_up, w_down=w_down, b_down=b_down)
    n_c = BATCH + DEC_BATCH
    c_all = jnp.concatenate([c_prompt, c_sample, jnp.zeros((40 - n_c, D), F32)], axis=0)
    mod_all = _ada_mod(c_all, w_ada, b_ada)

    xp = x_prompt.reshape(MP, D)
    xs = x_sample.reshape(MS, D)
    new_p, new_s = [], []
    for l in range(DEPTH):
        mod_p = mod_all[l, :BATCH].reshape(BATCH, 6, 1, D).transpose(1, 0, 2, 3)
        mod_s = jnp.repeat(mod_all[l, BATCH:n_c].reshape(DEC_BATCH, 6, D), DEC_SEQ, axis=0)
        mod_s = mod_s.transpose(1, 0, 2).reshape(6, 1, MS, D)
        st = (cache_k[l], cache_v[l], state_delta[l], state_delta_conv[l], state_lru[l],
              state_lru_conv[l])
        xp, xs, st_p, st_s = _layer(l, xp, xs, st, w, mod_p, mod_s)
        new_p.append(st_p)
        new_s.append(st_s)
    k_p, v_p, d_p, dc_p, h_p, hc_p = (jnp.stack(z) for z in zip(*new_p))
    k_s, v_s, d_s, dc_s, h_s, hc_s = (jnp.stack(z) for z in zip(*new_s))
    return (xp.reshape(BATCH, SEQ, D), xs.reshape(DEC_BATCH, DEC_SEQ, D),
            k_p, v_p, d_p, dc_p, h_p, hc_p, k_s, v_s, d_s, dc_s, h_s, hc_s)
```

```python
import functools

import jax
import jax.numpy as jnp
from jax import lax
from jax.experimental import pallas as pl
from jax.experimental.pallas import tpu as pltpu

F32 = jnp.float32
BF16 = jnp.bfloat16
I32 = jnp.int32

D = 2048
BATCH = 2
SEQ = 4096
DEC_BATCH = 32
DEC_SEQ = 4
DEPTH = 2
MP = BATCH * SEQ
MS = DEC_BATCH * DEC_SEQ
NTOK = MP + MS

WINDOW = 128
HD = 64
NH = 16
NKV = 4
GRP = NH // NKV
ATT_SCALE = HD ** -0.5

DK = 128
DNH = 8
CHUNK = 64
CONV_CH = 3 * DNH * DK
CONV_W = 4

LW = 1024
LBLK = 128
LNB = LW // LBLK
LRU_C = 8.0

NE = 32
TOPK = 4
DFF = 2048
SW_ALPHA = 1.702
SW_LIMIT = 7.0
EPS = 1e-6

P_DQKV = 0
P_G = 3072
P_DZ = 9216
P_LX = 10240
P_Q = 11264
P_K = 12288
P_V = 12544
P_W = 12800

RB = 256
NB_MAX = -(-(NTOK * TOPK) // RB) + NE
ROWS = NB_MAX * RB
TN_E = 1024
NJ = DFF // TN_E
TOK_TILE = 128
N_TT = NTOK // TOK_TILE

VMEM_BIG = 56 * 1024 * 1024


def _sds(shape, dtype):
    return jax.ShapeDtypeStruct(shape, dtype)


def _cparams(n_axes, vmem=None):
    return pltpu.CompilerParams(dimension_semantics=("arbitrary",) * n_axes, vmem_limit_bytes=vmem)


def _dot(a, b):
    return jnp.dot(a, b, preferred_element_type=F32)


def _dot_nt(a, b):
    return lax.dot_general(a, b, (((1,), (1,)), ((), ())), preferred_element_type=F32)


def _dot_tn(a, b):
    return lax.dot_general(a, b, (((0,), (0,)), ((), ())), preferred_element_type=F32)


def _rms(x, g):
    return x * lax.rsqrt(jnp.mean(x * x, axis=-1, keepdims=True) + EPS) * g


def _sigmoid(x):
    return 1.0 / (1.0 + jnp.exp(-x))


def _silu(x):
    return x * _sigmoid(x)


def _softplus(x):
    return jnp.maximum(x, 0.0) + jnp.log1p(jnp.exp(-jnp.abs(x)))


def _ada_kernel(c_ref, w_ref, b_ref, o_ref):
    a = _silu(c_ref[...]).astype(BF16)
    o_ref[0] = _dot(a, w_ref[0].astype(BF16)) + b_ref[0]


def _ada_mod(c_all, w_ada, b_ada):
    rows = c_all.shape[0]
    tn = 1024
    return pl.pallas_call(
        _ada_kernel,
        grid=(DEPTH, 6 * D // tn),
        in_specs=[
            pl.BlockSpec((rows, D), lambda l, n: (0, 0)),
            pl.BlockSpec((1, D, tn), lambda l, n: (l, 0, n)),
            pl.BlockSpec((1, 1, tn), lambda l, n: (l, 0, n)),
        ],
        out_specs=pl.BlockSpec((1, rows, tn), lambda l, n: (l, 0, n)),
        out_shape=_sds((DEPTH, rows, 6 * D), F32),
        compiler_params=_cparams(2, VMEM_BIG),
        name="ada_mod",
    )(c_all, w_ada, b_ada.reshape(DEPTH, 1, 6 * D))


def _adaln_kernel(x_ref, g_ref, sh_ref, sc_ref, o_ref):
    y = _rms(x_ref[...], g_ref[...])
    o_ref[...] = (y * (1.0 + sc_ref[0, 0]) + sh_ref[0, 0]).astype(o_ref.dtype)


def _mod_spec(mod, j, tiles, tn=D, with_n=False):
    g, rb = mod.shape[1], mod.shape[2]
    tpg = tiles // g
    if with_n:
        return pl.BlockSpec((1, 1, rb, tn), lambda n, m: (j, m // tpg, 0, n))
    return pl.BlockSpec((1, 1, rb, tn), lambda m: (j, m // tpg, 0, 0))


def _adaln(x, g, mod, j_sh, j_sc, tm):
    m_rows = x.shape[0]
    tiles = m_rows // tm
    return pl.pallas_call(
        _adaln_kernel,
        grid=(tiles,),
        in_specs=[
            pl.BlockSpec((tm, D), lambda m: (m, 0)),
            pl.BlockSpec((1, D), lambda m: (0, 0)),
            _mod_spec(mod, j_sh, tiles),
            _mod_spec(mod, j_sc, tiles),
        ],
        out_specs=pl.BlockSpec((tm, D), lambda m: (m, 0)),
        out_shape=_sds((m_rows, D), BF16),
        compiler_params=_cparams(1),
        name="adaln1",
    )(x, g, mod, mod)


def _mm_kernel(x_ref, w_ref, o_ref):
    o_ref[...] = _dot(x_ref[...], w_ref[...]).astype(o_ref.dtype)


def _matmul(x, w, out_dtype, tm, tn, name):
    m_rows, k = x.shape
    n_cols = w.shape[1]
    return pl.pallas_call(
        _mm_kernel,
        grid=(n_cols // tn, m_rows // tm),
        in_specs=[
            pl.BlockSpec((tm, k), lambda n, m: (m, 0)),
            pl.BlockSpec((k, tn), lambda n, m: (0, n)),
        ],
        out_specs=pl.BlockSpec((tm, tn), lambda n, m: (m, n)),
        out_shape=_sds((m_rows, n_cols), out_dtype),
        compiler_params=_cparams(2, VMEM_BIG),
        name=name,
    )(x, w)


def _attn_prompt_kernel(sinks_ref, q_ref, kc_ref, vc_ref, kp_ref, vp_ref, qg_ref, kg_ref,
                        o_ref, ko_ref, vo_ref):
    nblk = SEQ // WINDOW
    first = (pl.program_id(0) % nblk) == 0
    row = lax.broadcasted_iota(I32, (WINDOW, 2 * WINDOW), 0)
    col = lax.broadcasted_iota(I32, (WINDOW, 2 * WINDOW), 1)
    lo = jnp.where(first, WINDOW, 0)
    mask = (col > row) & (col <= row + WINDOW) & (col >= lo)
    qg = qg_ref[...]
    kg = kg_ref[...]
    outs = []
    k_out = []
    for kh in range(NKV):
        ks = slice(kh * HD, (kh + 1) * HD)
        kc_n = _rms(kc_ref[:, ks], kg)
        kp_n = _rms(kp_ref[:, ks], kg)
        k_out.append(kc_n)
        kk = jnp.concatenate([kp_n, kc_n], axis=0).astype(BF16)
        vv = jnp.concatenate([vp_ref[:, ks], vc_ref[:, ks]], axis=0).astype(BF16)
        for g in range(GRP):
            h = kh * GRP + g
            qh = _rms(q_ref[:, h * HD:(h + 1) * HD], qg).astype(BF16)
            s = _dot_nt(qh, kk) * ATT_SCALE
            s = jnp.where(mask, s, -jnp.inf)
            sink = sinks_ref[h]
            m = jnp.maximum(jnp.max(s, axis=-1, keepdims=True), sink)
            p = jnp.exp(s - m)
            den = jnp.sum(p, axis=-1, keepdims=True) + jnp.exp(sink - m)
            outs.append(_dot(p.astype(BF16), vv) / den)
    o_ref[...] = jnp.concatenate(outs, axis=-1).astype(o_ref.dtype)
    ko_ref[0] = jnp.concatenate(k_out, axis=-1)
    vo_ref[0] = vc_ref[...]


def _attn_prompt(p_act, qg, kg, sinks):
    nblk = SEQ // WINDOW
    cq, ck, cv = P_Q // 1024, P_K // 256, P_V // 256
    return pl.pallas_call(
        _attn_prompt_kernel,
        grid=(MP // WINDOW,),
        in_specs=[
            pl.BlockSpec(memory_space=pltpu.SMEM),
            pl.BlockSpec((WINDOW, 1024), lambda g: (g, cq)),
            pl.BlockSpec((WINDOW, 256), lambda g: (g, ck)),
            pl.BlockSpec((WINDOW, 256), lambda g: (g, cv)),
            pl.BlockSpec((WINDOW, 256), lambda g: (jnp.maximum(g - 1, 0), ck)),
            pl.BlockSpec((WINDOW, 256), lambda g: (jnp.maximum(g - 1, 0), cv)),
            pl.BlockSpec((1, HD), lambda g: (0, 0)),
            pl.BlockSpec((1, HD), lambda g: (0, 0)),
        ],
        out_specs=[
            pl.BlockSpec((WINDOW, 1024), lambda g: (g, 0)),
            pl.BlockSpec((1, WINDOW, 256), lambda g: (g // nblk, 0, 0)),
            pl.BlockSpec((1, WINDOW, 256), lambda g: (g // nblk, 0, 0)),
        ],
        out_shape=[_sds((MP, 1024), BF16), _sds((BATCH, WINDOW, 256), F32),
                   _sds((BATCH, WINDOW, 256), F32)],
        compiler_params=_cparams(1),
        name="attn_prompt",
    )(sinks, p_act, p_act, p_act, p_act, p_act, qg, kg)


def _attn_sample_kernel(sinks_ref, q_ref, kn_ref, vn_ref, ck_ref, cv_ref, qg_ref, kg_ref,
                        o_ref, kno_ref):
    rows = GRP * DEC_SEQ
    t = lax.broadcasted_iota(I32, (rows, 1), 0) % DEC_SEQ
    g_of_row = lax.broadcasted_iota(I32, (rows, 1), 0) // DEC_SEQ
    col_c = lax.broadcasted_iota(I32, (rows, WINDOW), 1)
    col_n = lax.broadcasted_iota(I32, (rows, 8), 1)
    qg = qg_ref[...]
    kg = kg_ref[...]
    for kh in range(NKV):
        q16 = _rms(q_ref[0, kh], qg).astype(BF16)
        kn = _rms(kn_ref[0, kh], kg)
        kno_ref[0, kh] = kn
        s_c = _dot_nt(q16, ck_ref[0, kh].astype(BF16)) * ATT_SCALE
        s_n = _dot_nt(q16, kn.astype(BF16)) * ATT_SCALE
        s_c = jnp.where(col_c > t, s_c, -jnp.inf)
        s_n = jnp.where(col_n <= t, s_n, -jnp.inf)
        sink = jnp.zeros((rows, 1), F32)
        for g in range(GRP):
            sink = jnp.where(g_of_row == g, sinks_ref[kh * GRP + g], sink)
        m = jnp.maximum(jnp.maximum(jnp.max(s_c, axis=-1, keepdims=True),
                                    jnp.max(s_n, axis=-1, keepdims=True)), sink)
        p_c = jnp.exp(s_c - m)
        p_n = jnp.exp(s_n - m)
        den = (jnp.sum(p_c, axis=-1, keepdims=True) + jnp.sum(p_n, axis=-1, keepdims=True)
               + jnp.exp(sink - m))
        o = _dot(p_c.astype(BF16), cv_ref[0, kh].astype(BF16)) + _dot(
            p_n.astype(BF16), vn_ref[0, kh].astype(BF16))
        o_ref[0, kh] = o / den


def _attn_sample(q_r, kn_r, vn_r, ck_r, cv_r, qg, kg, sinks):
    rows = GRP * DEC_SEQ
    return pl.pallas_call(
        _attn_sample_kernel,
        grid=(DEC_BATCH,),
        in_specs=[
            pl.BlockSpec(memory_space=pltpu.SMEM),
            pl.BlockSpec((1, NKV, rows, HD), lambda b: (b, 0, 0, 0)),
            pl.BlockSpec((1, NKV, 8, HD), lambda b: (b, 0, 0, 0)),
            pl.BlockSpec((1, NKV, 8, HD), lambda b: (b, 0, 0, 0)),
            pl.BlockSpec((1, NKV, WINDOW, HD), lambda b: (b, 0, 0, 0)),
            pl.BlockSpec((1, NKV, WINDOW, HD), lambda b: (b, 0, 0, 0)),
            pl.BlockSpec((1, HD), lambda b: (0, 0)),
            pl.BlockSpec((1, HD), lambda b: (0, 0)),
        ],
        out_specs=[
            pl.BlockSpec((1, NKV, rows, HD), lambda b: (b, 0, 0, 0)),
            pl.BlockSpec((1, NKV, 8, HD), lambda b: (b, 0, 0, 0)),
        ],
        out_shape=[_sds((DEC_BATCH, NKV, rows, HD), F32), _sds((DEC_BATCH, NKV, 8, HD), F32)],
        compiler_params=_cparams(1),
        name="attn_sample",
    )(sinks, q_r, kn_r, vn_r, ck_r, cv_r, qg, kg)


def _cumsum_rows(x, c):
    row = lax.broadcasted_iota(I32, x.shape, 0)
    s = 1
    while s < c:
        x = x + jnp.where(row >= s, pltpu.roll(x, s, 0), 0.0)
        s *= 2
    return x


def _delta_chunk(c, nstack, conv_slice, g_full, beta_full, z_slice, norm_g, s_get, s_put, o_put):
    n = nstack * c
    lg = c.bit_length() - 1
    r = lax.broadcasted_iota(I32, (n, n), 0)
    cc = lax.broadcasted_iota(I32, (n, n), 1)
    same = (r >> lg) == (cc >> lg)
    incl = same & (r >= cc)
    strict = same & (r > cc)
    eye_b = r == cc
    eye = eye_b.astype(F32)
    gc = _cumsum_rows(g_full, c)
    for st in range(DNH // nstack):
        heads = range(st * nstack, (st + 1) * nstack)
        qs, ks, vs, gcs, bs = [], [], [], [], []
        for h in heads:
            qh = conv_slice(slice(h * DK, (h + 1) * DK))
            kh = conv_slice(slice(DNH * DK + h * DK, DNH * DK + (h + 1) * DK))
            vs.append(conv_slice(slice(2 * DNH * DK + h * DK, 2 * DNH * DK + (h + 1) * DK)))
            qs.append(qh * lax.rsqrt(jnp.sum(qh * qh, axis=-1, keepdims=True) + EPS) * (DK ** -0.5))
            ks.append(kh * lax.rsqrt(jnp.sum(kh * kh, axis=-1, keepdims=True) + EPS))
            gcs.append(gc[:, h:h + 1])
            bs.append(beta_full[:, DNH + h:DNH + h + 1])
        q = jnp.concatenate(qs, axis=0)
        k = jnp.concatenate(ks, axis=0)
        v = jnp.concatenate(vs, axis=0)
        gcol = jnp.concatenate(gcs, axis=0)
        bcol = jnp.concatenate(bs, axis=0)
        grow = jnp.sum(jnp.where(eye_b, gcol, 0.0), axis=0, keepdims=True)
        decay = jnp.exp(jnp.where(incl, gcol - grow, -jnp.inf))
        egc = jnp.exp(gcol)
        kb = k * bcol
        k16 = k.astype(BF16)
        a_mat = jnp.where(strict, _dot_nt(kb.astype(BF16), k16) * decay, 0.0)
        blk = 1
        t_inv = eye
        while blk < c:
            sh = blk.bit_length()
            pair = ((r >> sh) == (cc >> sh)) & ((r & blk) != 0) & ((cc & blk) == 0)
            off = jnp.where(pair, a_mat, 0.0)
            if blk == 1:
                t_inv = t_inv - off
            else:
                t16 = t_inv.astype(BF16)
                t_inv = t_inv - _dot(t16, _dot(off.astype(BF16), t16).astype(BF16))
            blk *= 2
        rhs = jnp.concatenate([v * bcol, kb * egc], axis=-1).astype(BF16)
        sol = _dot(t_inv.astype(BF16), rhs)
        qk16 = (_dot_nt(q.astype(BF16), k16) * decay).astype(BF16)
        q_dec = q * egc
        us, s_olds, g_lasts = [], [], []
        for i, h in enumerate(heads):
            hs = slice(i * c, (i + 1) * c)
            s_old = s_get(h)
            us.append(sol[hs, :DK] - _dot(sol[hs, DK:].astype(BF16), s_old.astype(BF16)))
            s_olds.append(s_old)
            g_lasts.append(gcol[(i + 1) * c - 1:(i + 1) * c, :])
        u16 = jnp.concatenate(us, axis=0).astype(BF16)
        o_in = _dot(qk16, u16)
        for i, h in enumerate(heads):
            hs = slice(i * c, (i + 1) * c)
            s16 = s_olds[i].astype(BF16)
            o = _dot(q_dec[hs].astype(BF16), s16) + o_in[hs]
            k_dec = k[hs] * jnp.exp(g_lasts[i] - gcol[hs])
            s_put(h, s_olds[i] * jnp.exp(g_lasts[i]) + _dot_tn(k_dec.astype(BF16), u16[hs]))
            zz = z_slice(slice(h * DK, (h + 1) * DK))
            o_put(h, _rms(o, norm_g) * _silu(zz))


def _delta_prompt_kernel(x_ref, prev_ref, dz_ref, dab_ref, cw_ref, alog_ref, dtb_ref, ng_ref,
                         o_ref, s_ref, dc_ref):
    c = pl.program_id(0)

    @pl.when(c == 0)
    def _():
        s_ref[...] = jnp.zeros_like(s_ref)

    row8 = lax.broadcasted_iota(I32, (8, DK), 0)
    keep_prev = c > 0
    for b in range(BATCH):
        def conv_slice(cs, b=b):
            x = x_ref[b, :, cs]
            prev = jnp.where(keep_prev, prev_ref[b, :, cs], 0.0)
            y = x * cw_ref[CONV_W - 1:CONV_W, cs]
            for s in range(1, CONV_W):
                xr = pltpu.roll(x, s, 0)
                top = jnp.where(row8 < s, pltpu.roll(prev, s, 0), xr[0:8])
                sh = jnp.concatenate([top, xr[8:]], axis=0)
                y = y + sh * cw_ref[CONV_W - 1 - s:CONV_W - s, cs]
            return _silu(y)

        dab = dab_ref[b]
        g_full = -jnp.exp(alog_ref[...]) * _softplus(dab + dtb_ref[...])
        beta_full = _sigmoid(dab)

        def s_get(h, b=b):
            return s_ref[b, h]

        def s_put(h, v, b=b):
            s_ref[b, h] = v

        def o_put(h, v, b=b):
            o_ref[b, :, h * DK:(h + 1) * DK] = v.astype(o_ref.dtype)

        _delta_chunk(CHUNK, 4, conv_slice, g_full, beta_full, lambda cs, b=b: dz_ref[b, :, cs],
                     ng_ref[...], s_get, s_put, o_put)

    @pl.when(c == pl.num_programs(0) - 1)
    def _():
        dc_ref[...] = x_ref[:, CHUNK - (CONV_W - 1):CHUNK, :]


def _delta_prompt(p_act, p_ab, conv_w, alog_row, dtb_row, norm_g):
    nck = SEQ // CHUNK
    p3 = p_act.reshape(BATCH, SEQ, p_act.shape[1])
    ab3 = p_ab.reshape(BATCH, SEQ, 128)
    return pl.pallas_call(
        _delta_prompt_kernel,
        grid=(nck,),
        in_specs=[
            pl.BlockSpec((BATCH, CHUNK, CONV_CH), lambda c: (0, c, P_DQKV // CONV_CH)),
            pl.BlockSpec((BATCH, 8, CONV_CH),
                         lambda c: (0, jnp.maximum(c * (CHUNK // 8) - 1, 0), P_DQKV // CONV_CH)),
            pl.BlockSpec((BATCH, CHUNK, 1024), lambda c: (0, c, P_DZ // 1024)),
            pl.BlockSpec((BATCH, CHUNK, 128), lambda c: (0, c, 0)),
            pl.BlockSpec((CONV_W, CONV_CH), lambda c: (0, 0)),
            pl.BlockSpec((1, 128), lambda c: (0, 0)),
            pl.BlockSpec((1, 128), lambda c: (0, 0)),
            pl.BlockSpec((1, DK), lambda c: (0, 0)),
        ],
        out_specs=[
            pl.BlockSpec((BATCH, CHUNK, 1024), lambda c: (0, c, 0)),
            pl.BlockSpec((BATCH, DNH, DK, DK), lambda c: (0, 0, 0, 0)),
            pl.BlockSpec((BATCH, CONV_W - 1, CONV_CH), lambda c: (0, 0, 0)),
        ],
        out_shape=[_sds((BATCH, SEQ, 1024), BF16), _sds((BATCH, DNH, DK, DK), F32),
                   _sds((BATCH, CONV_W - 1, CONV_CH), F32)],
        compiler_params=_cparams(1),
        name="delta_prompt",
    )(p3, p3, p3, ab3, conv_w, alog_row, dtb_row, norm_g)


def _delta_sample_kernel(xp_ref, dz_ref, dab_ref, s0_ref, cw_ref, alog_ref, dtb_ref, ng_ref,
                         o_ref, s_ref):
    row = lax.broadcasted_iota(I32, (8, DK), 0)
    live = row < DEC_SEQ

    def conv_slice(cs):
        xp = xp_ref[0, :, cs]
        y = jnp.zeros((8, DK), F32)
        for j in range(CONV_W):
            y = y + pltpu.roll(xp, 8 - 1 - j, 0) * cw_ref[j:j + 1, cs]
        return jnp.where(live, _silu(y), 0.0)

    dab = dab_ref[0]
    g_full = jnp.where(live, -jnp.exp(alog_ref[...]) * _softplus(dab + dtb_ref[...]), 0.0)
    beta_full = jnp.where(live, _sigmoid(dab), 0.0)

    def s_put(h, v):
        s_ref[0, h] = v

    def o_put(h, v):
        o_ref[0, :, h * DK:(h + 1) * DK] = v

    _delta_chunk(8, DNH, conv_slice, g_full, beta_full, lambda cs: dz_ref[0, :, cs], ng_ref[...],
                 lambda h: s0_ref[0, h], s_put, o_put)


def _delta_sample(xp8, dz8, dab8, s0, conv_w, alog_row, dtb_row, norm_g):
    return pl.pallas_call(
        _delta_sample_kernel,
        grid=(DEC_BATCH,),
        in_specs=[
            pl.BlockSpec((1, 8, CONV_CH), lambda b: (b, 0, 0)),
            pl.BlockSpec((1, 8, 1024), lambda b: (b, 0, 0)),
            pl.BlockSpec((1, 8, 128), lambda b: (b, 0, 0)),
            pl.BlockSpec((1, DNH, DK, DK), lambda b: (b, 0, 0, 0)),
            pl.BlockSpec((CONV_W, CONV_CH), lambda b: (0, 0)),
            pl.BlockSpec((1, 128), lambda b: (0, 0)),
            pl.BlockSpec((1, 128), lambda b: (0, 0)),
            pl.BlockSpec((1, DK), lambda b: (0, 0)),
        ],
        out_specs=[
            pl.BlockSpec((1, 8, 1024), lambda b: (b, 0, 0)),
            pl.BlockSpec((1, DNH, DK, DK), lambda b: (b, 0, 0, 0)),
        ],
        out_shape=[_sds((DEC_BATCH, 8, 1024), F32), _sds((DEC_BATCH, DNH, DK, DK), F32)],
        compiler_params=_cparams(1),
        name="delta_sample",
    )(xp8, dz8, dab8, s0, conv_w, alog_row, dtb_row, norm_g)


def _lru_gates(xc, wr, br, wi, bi, sp):
    x16 = xc.astype(BF16)
    r = _sigmoid(_dot(x16, wr) + br)
    i = _sigmoid(_dot(x16, wi) + bi)
    log_a = -LRU_C * r * sp
    a = jnp.exp(log_a)
    th = jnp.tanh(log_a)
    u = jnp.sqrt(-2.0 * th / (1.0 - th)) * (i * xc)
    return a, u


LRU_TT = 256


def _lru_prompt_kernel(x_ref, prev_ref, cw_ref, cb_ref, wr_ref, br_ref, wi_ref, bi_ref, lam_ref,
                       o_ref, h_ref, lc_ref):
    t = pl.program_id(1)
    row8 = lax.broadcasted_iota(I32, (8, LBLK), 0)
    row = lax.broadcasted_iota(I32, (LRU_TT, LBLK), 0)
    keep = t > 0
    for n in range(LNB):
        cs = slice(n * LBLK, (n + 1) * LBLK)
        x = x_ref[:, cs]
        prev = jnp.where(keep, prev_ref[:, cs], 0.0)
        y = x * cw_ref[CONV_W - 1:CONV_W, cs]
        for s in range(1, CONV_W):
            xr = pltpu.roll(x, s, 0)
            top = jnp.where(row8 < s, pltpu.roll(prev, s, 0), xr[0:8])
            y = y + jnp.concatenate([top, xr[8:]], axis=0) * cw_ref[CONV_W - 1 - s:CONV_W - s, cs]
        xc = y + cb_ref[:, cs]
        sp = _softplus(-lam_ref[:, cs])
        a, u = _lru_gates(xc, wr_ref[n], br_ref[:, cs], wi_ref[n], bi_ref[:, cs], sp)
        h0 = jnp.where(keep, h_ref[0, :, cs], 0.0)
        u = u + jnp.where(row == 0, a * h0, 0.0)
        s = 1
        while s < LRU_TT:
            valid = row >= s
            u_s = pltpu.roll(u, s, 0)
            a_s = pltpu.roll(a, s, 0)
            u = jnp.where(valid, a * u_s + u, u)
            a = jnp.where(valid, a * a_s, a)
            s *= 2
        o_ref[:, cs] = u.astype(o_ref.dtype)
        h_ref[0, :, cs] = u[LRU_TT - 1:LRU_TT, :]

    @pl.when(t == pl.num_programs(1) - 1)
    def _():
        lc_ref[0] = x_ref[LRU_TT - (CONV_W - 1):LRU_TT, :]


def _lru_prompt(p_act, cw, cb, wr, br, wi, bi, lam):
    ntt = SEQ // LRU_TT
    cl = P_LX // LW
    vec = pl.BlockSpec((1, LW), lambda b, t: (0, 0))
    mat = pl.BlockSpec((LNB, LBLK, LBLK), lambda b, t: (0, 0, 0))
    return pl.pallas_call(
        _lru_prompt_kernel,
        grid=(BATCH, ntt),
        in_specs=[
            pl.BlockSpec((LRU_TT, LW), lambda b, t: (b * ntt + t, cl)),
            pl.BlockSpec((8, LW),
                         lambda b, t: (jnp.maximum(b * (SEQ // 8) + t * (LRU_TT // 8) - 1, 0), cl)),
            pl.BlockSpec((CONV_W, LW), lambda b, t: (0, 0)),
            vec, mat, vec, mat, vec, vec,
        ],
        out_specs=[
            pl.BlockSpec((LRU_TT, LW), lambda b, t: (b * ntt + t, 0)),
            pl.BlockSpec((1, 1, LW), lambda b, t: (b, 0, 0)),
            pl.BlockSpec((1, CONV_W - 1, LW), lambda b, t: (b, 0, 0)),
        ],
        out_shape=[_sds((MP, LW), BF16), _sds((BATCH, 1, LW), F32),
                   _sds((BATCH, CONV_W - 1, LW), F32)],
        compiler_params=_cparams(2),
        name="lru_prompt",
    )(p_act, p_act, cw, cb, wr, br, wi, bi, lam)


def _lru_sample_kernel(x_ref, buf_ref, h0_ref, cw_ref, cb_ref, wr_ref, br_ref, wi_ref, bi_ref,
                       lam_ref, o_ref):
    for n in range(LNB):
        cs = slice(n * LBLK, (n + 1) * LBLK)
        xp = [buf_ref[j, :, cs] for j in range(CONV_W - 1)] + [x_ref[t, :, cs] for t in range(DEC_SEQ)]
        xcs = []
        for t in range(DEC_SEQ):
            y = xp[t] * cw_ref[0:1, cs]
            for j in range(1, CONV_W):
                y = y + xp[t + j] * cw_ref[j:j + 1, cs]
            xcs.append(y + cb_ref[:, cs])
        xc = jnp.concatenate(xcs, axis=0)
        sp = _softplus(-lam_ref[:, cs])
        a, u = _lru_gates(xc, wr_ref[n], br_ref[:, cs], wi_ref[n], bi_ref[:, cs], sp)
        h = h0_ref[:, cs]
        for t in range(DEC_SEQ):
            rs = slice(t * DEC_BATCH, (t + 1) * DEC_BATCH)
            h = a[rs] * h + u[rs]
            o_ref[t, :, cs] = h


def _lru_sample(x_tm, buf_tm, h0, cw, cb, wr, br, wi, bi, lam):
    return pl.pallas_call(
        _lru_sample_kernel,
        out_shape=_sds((DEC_SEQ, DEC_BATCH, LW), F32),
        name="lru_sample",
    )(x_tm, buf_tm, h0, cw, cb, wr, br, wi, bi, lam)


def _merge_kernel(oa_ref, ob_ref, oc_ref, w_ref, ga_ref, gb_ref, gc_ref, o_ref):
    acc = _sigmoid(ga_ref[...]) * _dot(oa_ref[...], w_ref[0])
    acc = acc + _sigmoid(gb_ref[...]) * _dot(ob_ref[...], w_ref[1])
    acc = acc + _sigmoid(gc_ref[...]) * _dot(oc_ref[...], w_ref[2])
    o_ref[...] = acc.astype(o_ref.dtype)


def _merge(o_a, o_b, o_c, wb, p_act, tm):
    m_rows = o_a.shape[0]
    tn = 512
    g0 = P_G // tn
    gs = D // tn
    br = pl.BlockSpec((tm, 1024), lambda n, m: (m, 0))
    return pl.pallas_call(
        _merge_kernel,
        grid=(D // tn, m_rows // tm),
        in_specs=[
            br, br, br,
            pl.BlockSpec((3, 1024, tn), lambda n, m: (0, 0, n)),
            pl.BlockSpec((tm, tn), lambda n, m: (m, g0 + n)),
            pl.BlockSpec((tm, tn), lambda n, m: (m, g0 + gs + n)),
            pl.BlockSpec((tm, tn), lambda n, m: (m, g0 + 2 * gs + n)),
        ],
        out_specs=pl.BlockSpec((tm, tn), lambda n, m: (m, n)),
        out_shape=_sds((m_rows, D), BF16),
        compiler_params=_cparams(2, VMEM_BIG),
        name="merge",
    )(o_a, o_b, o_c, wb, p_act, p_act, p_act)


def _outproj_kernel(a_ref, w_ref, x_ref, gt_ref, o_ref):
    o_ref[...] = x_ref[...] + gt_ref[0, 0] * _dot(a_ref[...], w_ref[...])


def _outproj(merged, w_out, x, mod, tm):
    m_rows = x.shape[0]
    tn = 1024
    tiles = m_rows // tm
    return pl.pallas_call(
        _outproj_kernel,
        grid=(D // tn, tiles),
        in_specs=[
            pl.BlockSpec((tm, D), lambda n, m: (m, 0)),
            pl.BlockSpec((D, tn), lambda n, m: (0, n)),
            pl.BlockSpec((tm, tn), lambda n, m: (m, n)),
            _mod_spec(mod, 2, tiles, tn, with_n=True),
        ],
        out_specs=pl.BlockSpec((tm, tn), lambda n, m: (m, n)),
        out_shape=_sds((m_rows, D), F32),
        compiler_params=_cparams(2, VMEM_BIG),
        name="outproj",
    )(merged, w_out, x, mod)


def _router_kernel(xp_ref, xs_ref, g_ref, shp_ref, scp_ref, shs_ref, scs_ref, wr_ref, br_ref,
                   h_ref, te_ref, tw_ref):
    tm = TOK_TILE
    is_s = pl.program_id(0) == N_TT - 1
    x = jnp.where(is_s, xs_ref[...], xp_ref[...])
    sc = jnp.where(is_s, scs_ref[0, 0], scp_ref[0, 0])
    sh = jnp.where(is_s, shs_ref[0, 0], shp_ref[0, 0])
    h = _rms(x, g_ref[...]) * (1.0 + sc) + sh
    h_ref[...] = h
    lane = lax.broadcasted_iota(I32, (tm, 128), 1)
    logits = _dot(h.astype(BF16), wr_ref[...]) + br_ref[...]
    logits = jnp.where(lane < NE, logits, -jnp.inf)
    lane4 = lax.broadcasted_iota(I32, (tm, TOPK), 1)
    te = jnp.zeros((tm, TOPK), I32)
    tl = jnp.zeros((tm, TOPK), F32)
    for k in range(TOPK):
        m = jnp.max(logits, axis=-1, keepdims=True)
        idx = jnp.min(jnp.where(logits == m, lane, 128), axis=-1, keepdims=True)
        te = jnp.where(lane4 == k, idx, te)
        tl = jnp.where(lane4 == k, m, tl)
        logits = jnp.where(lane == idx, -jnp.inf, logits)
    e = jnp.exp(tl - tl[:, 0:1])
    te_ref[...] = te
    tw_ref[...] = e / jnp.sum(e, axis=-1, keepdims=True)


def _router(x1p, x1s, g, mod_p, mod_s, wr, br):
    np_t = MP // TOK_TILE
    tpg = np_t // BATCH
    last_p = np_t - 1

    def mp(j):
        return pl.BlockSpec((1, 1, 1, D), lambda m: (j, jnp.minimum(m, last_p) // tpg, 0, 0))

    def ms(j):
        return pl.BlockSpec((1, 1, MS, D), lambda m: (j, 0, 0, 0))

    return pl.pallas_call(
        _router_kernel,
        grid=(N_TT,),
        in_specs=[
            pl.BlockSpec((TOK_TILE, D), lambda m: (jnp.minimum(m, last_p), 0)),
            pl.BlockSpec((MS, D), lambda m: (0, 0)),
            pl.BlockSpec((1, D), lambda m: (0, 0)),
            mp(3), mp(4), ms(3), ms(4),
            pl.BlockSpec((D, 128), lambda m: (0, 0)),
            pl.BlockSpec((1, 128), lambda m: (0, 0)),
        ],
        out_specs=[
            pl.BlockSpec((TOK_TILE, D), lambda m: (m, 0)),
            pl.BlockSpec((TOK_TILE, TOPK), lambda m: (m, 0)),
            pl.BlockSpec((TOK_TILE, TOPK), lambda m: (m, 0)),
        ],
        out_shape=[_sds((NTOK, D), F32), _sds((NTOK, TOPK), I32), _sds((NTOK, TOPK), F32)],
        compiler_params=_cparams(1),
        name="router",
    )(x1p, x1s, g, mod_p, mod_p, mod_s, mod_s, wr, br)


def _rank_kernel(te_ref, rk_ref, cnt_ref):
    @pl.when(pl.program_id(0) == 0)
    def _():
        cnt_ref[...] = jnp.zeros_like(cnt_ref)

    tt = TOK_TILE
    lane = lax.broadcasted_iota(I32, (tt, 128), 1)
    te = te_ref[...]
    hot = jnp.zeros((tt, 128), F32)
    for k in range(TOPK):
        hot = hot + (lane == te[:, k:k + 1]).astype(F32)
    r = lax.broadcasted_iota(I32, (tt, tt), 0)
    c = lax.broadcasted_iota(I32, (tt, tt), 1)
    before = (r > c).astype(BF16)
    tot = _dot(before, hot.astype(BF16)) + cnt_ref[...]
    lane4 = lax.broadcasted_iota(I32, (tt, TOPK), 1)
    rk = jnp.zeros((tt, TOPK), F32)
    for k in range(TOPK):
        rk_k = jnp.sum(jnp.where(lane == te[:, k:k + 1], tot, 0.0), axis=-1, keepdims=True)
        rk = jnp.where(lane4 == k, rk_k, rk)
    rk_ref[...] = rk.astype(I32)
    cnt_ref[...] = cnt_ref[...] + jnp.sum(hot, axis=0, keepdims=True)


def _rank(te):
    return pl.pallas_call(
        _rank_kernel,
        grid=(N_TT,),
        in_specs=[pl.BlockSpec((TOK_TILE, TOPK), lambda i: (i, 0))],
        out_specs=[pl.BlockSpec((TOK_TILE, TOPK), lambda i: (i, 0)),
                   pl.BlockSpec((1, 128), lambda i: (0, 0))],
        out_shape=[_sds((NTOK, TOPK), I32), _sds((1, 128), F32)],
        compiler_params=_cparams(1),
        name="rank",
    )(te)


def _scatter_kernel(dest_ref, last_ref, h_ref, xs_out, zbuf, zsem, sem):
    @pl.when(pl.program_id(0) == 0)
    def _():
        zbuf[...] = jnp.zeros_like(zbuf)

        def zero_block(row):
            return pltpu.make_async_copy(zbuf, xs_out.at[pl.ds(row, RB)], zsem)

        def extra_row(i):
            return pl.multiple_of(last_ref[NE] + i * RB, RB)

        n_extra = last_ref[NE + 1]
        for e in range(NE):
            zero_block(pl.multiple_of(last_ref[e], RB)).start()
        lax.fori_loop(0, n_extra, lambda i, c: (zero_block(extra_row(i)).start(), c)[1], 0)
        for e in range(NE):
            zero_block(pl.multiple_of(last_ref[e], RB)).wait()
        lax.fori_loop(0, n_extra, lambda i, c: (zero_block(extra_row(i)).wait(), c)[1], 0)

    def body(i, carry):
        for k in range(TOPK):
            d = dest_ref[0, 0, i * TOPK + k]
            pltpu.make_async_copy(h_ref.at[pl.ds(i, 1)], xs_out.at[pl.ds(d, 1)], sem).start(
                priority=k % 2)
        return carry

    lax.fori_loop(0, TOK_TILE, body, 0)
    for k in range(TOPK):
        pltpu.make_async_copy(h_ref, xs_out.at[pl.ds(0, TOK_TILE)], sem).wait()


def _scatter_rows(dest3, last_rows, h2):
    return pl.pallas_call(
        _scatter_kernel,
        grid=(N_TT,),
        in_specs=[
            pl.BlockSpec((1, 1, TOK_TILE * TOPK), lambda i: (i, 0, 0), memory_space=pltpu.SMEM),
            pl.BlockSpec(memory_space=pltpu.SMEM),
            pl.BlockSpec((TOK_TILE, D), lambda i: (i, 0)),
        ],
        out_specs=pl.BlockSpec(memory_space=pl.ANY),
        out_shape=_sds((ROWS, D), F32),
        scratch_shapes=[pltpu.VMEM((RB, D), F32), pltpu.SemaphoreType.DMA(()),
                        pltpu.SemaphoreType.DMA(())],
        compiler_params=_cparams(1),
        name="scatter_rows",
    )(dest3, last_rows, h2)


def _zero_unowned(zsrc, dst_at, used_ref, zsem):
    n = NB_MAX - used_ref[0]

    def cp(i):
        return pltpu.make_async_copy(zsrc, dst_at(used_ref[0] + i), zsem)

    lax.fori_loop(0, n, lambda i, c: (cp(i).start(), c)[1], 0)
    lax.fori_loop(0, n, lambda i, c: (cp(i).wait(), c)[1], 0)


def _row_of(blk):
    return pl.multiple_of(blk * RB, RB)


ROW_DMA_PRIORITY = 1


def _gmm_up_kernel(bstart, nblk, used, x_hbm, wg_ref, wl_ref, bg_ref, bl_ref, act_hbm,
                   wg_s, wl_s, xbuf, obuf, xsem, osem, zsem):
    s = pl.program_id(0)
    e = s // NJ
    j = s % NJ
    nb = nblk[e]
    b0 = bstart[e]

    @pl.when(s == 0)
    def _():
        obuf[0] = jnp.zeros(obuf.shape[1:], obuf.dtype)
        for jj in range(NJ):
            _zero_unowned(obuf.at[0],
                          lambda blk, jj=jj: act_hbm.at[jj, pl.ds(_row_of(blk), RB)], used, zsem)

    def x_copy(i, slot):
        return pltpu.make_async_copy(x_hbm.at[pl.ds(_row_of(b0 + i), RB)], xbuf.at[slot],
                                     xsem.at[slot])

    def o_copy(i, slot):
        return pltpu.make_async_copy(obuf.at[slot], act_hbm.at[j, pl.ds(_row_of(b0 + i), RB)],
                                     osem.at[slot])

    @pl.when(nb > 0)
    def _():
        x_copy(0, 0).start(priority=ROW_DMA_PRIORITY)
        wg_s[...] = wg_ref[0, 0].astype(BF16)
        wl_s[...] = wl_ref[0, 0].astype(BF16)

        def body(i, carry):
            slot = i % 2
            x_copy(i, slot).wait()

            @pl.when(i + 1 < nb)
            def _():
                x_copy(i + 1, 1 - slot).start(priority=ROW_DMA_PRIORITY)

            @pl.when(i >= 2)
            def _():
                o_copy(i - 2, slot).wait()

            x = xbuf[slot].astype(BF16)
            glu = _dot(x, wg_s[...]) + bg_ref[0, 0]
            lin = _dot(x, wl_s[...]) + bl_ref[0, 0]
            glu = jnp.minimum(glu, SW_LIMIT)
            lin = jnp.clip(lin, -SW_LIMIT, SW_LIMIT)
            obuf[slot] = (glu * _sigmoid(SW_ALPHA * glu) * (lin + 1.0)).astype(obuf.dtype)
            o_copy(i, slot).start(priority=ROW_DMA_PRIORITY)
            return carry

        lax.fori_loop(0, nb, body, 0)

        @pl.when(nb >= 2)
        def _():
            o_copy(nb - 2, nb % 2).wait()

        o_copy(nb - 1, (nb - 1) % 2).wait()


def _gmm_up(tables, xs, w_gu, b_gu, layer):
    grid_spec = pltpu.PrefetchScalarGridSpec(
        num_scalar_prefetch=3,
        grid=(NE * NJ,),
        in_specs=[
            pl.BlockSpec(memory_space=pl.ANY),
            pl.BlockSpec((1, 1, D, TN_E), lambda s, *_: (layer, s // NJ, 0, s % NJ)),
            pl.BlockSpec((1, 1, D, TN_E), lambda s, *_: (layer, s // NJ, 0, NJ + s % NJ)),
            pl.BlockSpec((1, 1, 1, TN_E), lambda s, *_: (layer, s // NJ, 0, s % NJ)),
            pl.BlockSpec((1, 1, 1, TN_E), lambda s, *_: (layer, s // NJ, 0, NJ + s % NJ)),
        ],
        out_specs=pl.BlockSpec(memory_space=pl.ANY),
        scratch_shapes=[pltpu.VMEM((D, TN_E), BF16), pltpu.VMEM((D, TN_E), BF16),
                        pltpu.VMEM((2, RB, D), F32), pltpu.VMEM((2, RB, TN_E), BF16),
                        pltpu.SemaphoreType.DMA((2,)), pltpu.SemaphoreType.DMA((2,)),
                        pltpu.SemaphoreType.DMA(())],
    )
    return pl.pallas_call(
        _gmm_up_kernel,
        grid_spec=grid_spec,
        out_shape=_sds((NJ, ROWS, TN_E), BF16),
        compiler_params=_cparams(1, VMEM_BIG),
        name="gmm_up",
    )(*tables, xs, w_gu, w_gu, b_gu, b_gu)


def _gmm_down_kernel(bstart, nblk, used, x_hbm, w_ref, b_ref, out_hbm, w_s, xbuf, obuf,
                     xsem, osem, zsem):
    e = pl.program_id(0)
    nb = nblk[e]
    b0 = bstart[e]

    @pl.when(e == 0)
    def _():
        obuf[0] = jnp.zeros(obuf.shape[1:], obuf.dtype)
        _zero_unowned(obuf.at[0], lambda blk: out_hbm.at[pl.ds(_row_of(blk), RB)], used, zsem)

    def x_copy(i, slot, jj):
        return pltpu.make_async_copy(x_hbm.at[jj, pl.ds(_row_of(b0 + i), RB)], xbuf.at[slot, jj],
                                     xsem.at[slot])

    def o_copy(i, slot):
        return pltpu.make_async_copy(obuf.at[slot], out_hbm.at[pl.ds(_row_of(b0 + i), RB)],
                                     osem.at[slot])

    @pl.when(nb > 0)
    def _():
        for jj in range(NJ):
            x_copy(0, 0, jj).start(priority=ROW_DMA_PRIORITY)
        w_s[...] = w_ref[0, 0].astype(BF16)

        def body(i, carry):
            slot = i % 2
            for jj in range(NJ):
                x_copy(i, slot, jj).wait()

            @pl.when(i + 1 < nb)
            def _():
                for jj in range(NJ):
                    x_copy(i + 1, 1 - slot, jj).start(priority=ROW_DMA_PRIORITY)

            @pl.when(i >= 2)
            def _():
                o_copy(i - 2, slot).wait()

            acc = b_ref[0, 0] + _dot(xbuf[slot, 0], w_s[0:TN_E, :])
            for jj in range(1, NJ):
                acc = acc + _dot(xbuf[slot, jj], w_s[jj * TN_E:(jj + 1) * TN_E, :])
            obuf[slot] = acc
            o_copy(i, slot).start(priority=ROW_DMA_PRIORITY)
            return carry

        lax.fori_loop(0, nb, body, 0)

        @pl.when(nb >= 2)
        def _():
            o_copy(nb - 2, nb % 2).wait()

        o_copy(nb - 1, (nb - 1) % 2).wait()


def _gmm_down(tables, act, w_down, b_down, layer):
    grid_spec = pltpu.PrefetchScalarGridSpec(
        num_scalar_prefetch=3,
        grid=(NE,),
        in_specs=[
            pl.BlockSpec(memory_space=pl.ANY),
            pl.BlockSpec((1, 1, DFF, D), lambda e, *_: (layer, e, 0, 0)),
            pl.BlockSpec((1, 1, 1, D), lambda e, *_: (layer, e, 0, 0)),
        ],
        out_specs=pl.BlockSpec(memory_space=pl.ANY),
        scratch_shapes=[pltpu.VMEM((DFF, D), BF16), pltpu.VMEM((2, NJ, RB, TN_E), BF16),
                        pltpu.VMEM((2, RB, D), F32),
                        pltpu.SemaphoreType.DMA((2,)), pltpu.SemaphoreType.DMA((2,)),
                        pltpu.SemaphoreType.DMA(())],
    )
    return pl.pallas_call(
        _gmm_down_kernel,
        grid_spec=grid_spec,
        out_shape=_sds((ROWS, D), F32),
        compiler_params=_cparams(1, VMEM_BIG),
        name="gmm_down",
    )(*tables, act, w_down, b_down)


def _combine_kernel(dest_ref, rows_hbm, tw_ref, x_ref, gt_ref, o_ref, buf, sem):
    def body(i, carry):
        for k in range(TOPK):
            d = dest_ref[0, 0, i * TOPK + k]
            pltpu.make_async_copy(rows_hbm.at[pl.ds(d, 1)], buf.at[k, pl.ds(i, 1)], sem).start(
                priority=k % 2)
        return carry

    lax.fori_loop(0, TOK_TILE, body, 0)
    for k in range(TOPK):
        pltpu.make_async_copy(rows_hbm.at[pl.ds(0, TOK_TILE)], buf.at[k], sem).wait()
    tw = tw_ref[...]
    y = tw[:, 0:1] * buf[0]
    for k in range(1, TOPK):
        y = y + tw[:, k:k + 1] * buf[k]
    o_ref[...] = x_ref[...] + gt_ref[0, 0] * y


def _combine(dest3, out_rows, tw, x1, mod, tile0):
    m_rows = x1.shape[0]
    tiles = m_rows // TOK_TILE
    return pl.pallas_call(
        _combine_kernel,
        grid=(tiles,),
        in_specs=[
            pl.BlockSpec((1, 1, TOK_TILE * TOPK), lambda m: (tile0 + m, 0, 0),
                         memory_space=pltpu.SMEM),
            pl.BlockSpec(memory_space=pl.ANY),
            pl.BlockSpec((TOK_TILE, TOPK), lambda m: (tile0 + m, 0)),
            pl.BlockSpec((TOK_TILE, D), lambda m: (m, 0)),
            _mod_spec(mod, 5, tiles),
        ],
        out_specs=pl.BlockSpec((TOK_TILE, D), lambda m: (m, 0)),
        out_shape=_sds((m_rows, D), F32),
        scratch_shapes=[pltpu.VMEM((TOPK, TOK_TILE, D), F32), pltpu.SemaphoreType.DMA(())],
        compiler_params=_cparams(1),
        name="combine",
    )(dest3, out_rows, tw, x1, mod)


def _routing_tables(te, rank, cnt):
    counts = cnt[0, :NE].astype(I32)
    nblk = (counts + RB - 1) // RB
    bstart = jnp.cumsum(nblk) - nblk
    used = jnp.sum(nblk)
    te_d = te.reshape(N_TT, TOK_TILE * TOPK)
    start_d = jnp.zeros_like(te_d)
    for e in range(NE):
        start_d = jnp.where(te_d == e, bstart[e] * RB, start_d)
    dest3 = (start_d + rank.reshape(N_TT, TOK_TILE * TOPK)).reshape(N_TT, 1, TOK_TILE * TOPK)
    empty = nblk == 0
    n_empty = jnp.sum(empty.astype(I32))
    last_blk = jnp.where(empty, used + jnp.cumsum(empty.astype(I32)) - 1, bstart + nblk - 1)
    extra0 = used + n_empty
    last_rows = jnp.concatenate([last_blk * RB, jnp.stack([extra0 * RB, NB_MAX - extra0])])
    tables = (bstart.astype(I32), nblk.astype(I32), used.reshape(1).astype(I32))
    return dest3, tables, last_rows.astype(I32)


def _moe(layer, h2_all, te_all, w_gu, b_gu, w_down, b_down):
    rank, cnt = _rank(te_all)
    dest3, steps, last_rows = _routing_tables(te_all, rank, cnt)
    xs = _scatter_rows(dest3, last_rows, h2_all)
    act = _gmm_up(steps, xs, w_gu, b_gu.reshape(DEPTH, NE, 1, 2 * DFF), layer)
    out_rows = _gmm_down(steps, act, w_down, b_down.reshape(DEPTH, NE, 1, D), layer)
    return dest3, out_rows


def _layer(l, xp, xs, st, w, mod_p, mod_s):
    cast = lambda a: a.astype(BF16)
    w_in = w["w_in"][l]
    w_main = cast(jnp.concatenate(
        [w_in[:, 1536:4608], w_in[:, 6672:12816], w_in[:, 4624:5648], w_in[:, 5648:6672],
         w_in[:, 0:1024], w_in[:, 1024:1280], w_in[:, 1280:1536]], axis=1))
    w_ab = cast(jnp.pad(w_in[:, 4608:4624], ((0, 0), (0, 112))))
    wb = cast(w["w_branch"][l])
    w_out = cast(w["w_out"][l])
    wr = cast(jnp.pad(w["w_router"][l], ((0, 0), (0, 128 - NE))))
    br = jnp.pad(w["b_router"][l], (0, 128 - NE)).reshape(1, 128)
    g1 = w["norm1_g"][l].reshape(1, D)
    g2 = w["norm2_g"][l].reshape(1, D)
    qg = w["q_norm_g"][l].reshape(1, HD)
    kg = w["k_norm_g"][l].reshape(1, HD)
    sinks = w["sinks"][l]
    dn_cw = w["dn_conv_w"][l]
    alog_row = jnp.pad(w["dn_a_log"][l], (0, 128 - DNH)).reshape(1, 128)
    dtb_row = jnp.pad(w["dn_dt_bias"][l], (0, 128 - DNH)).reshape(1, 128)
    dn_ng = w["dn_norm_g"][l].reshape(1, DK)
    l_cw = w["lru_conv_w"][l]
    l_cb = w["lru_conv_b"][l].reshape(1, LW)
    l_wr = cast(w["lru_w_r"][l])
    l_wi = cast(w["lru_w_i"][l])
    l_br = w["lru_b_r"][l].reshape(1, LW)
    l_bi = w["lru_b_i"][l].reshape(1, LW)
    l_lam = w["lru_lambda"][l].reshape(1, LW)

    h1p = _adaln(xp, g1, mod_p, 0, 1, 256)
    pp = _matmul(h1p, w_main, F32, 1024, 1280, "proj")
    pab = _matmul(h1p, w_ab, F32, 2048, 128, "proj_ab")
    oa_p, k_p, v_p = _attn_prompt(pp, qg, kg, sinks)
    ob_p, d_p, dc_p = _delta_prompt(pp, pab, dn_cw, alog_row, dtb_row, dn_ng)
    ob_p = ob_p.reshape(MP, 1024)
    oc_p, h_p, hc_p = _lru_prompt(pp, l_cw, l_cb, l_wr, l_br, l_wi, l_bi, l_lam)
    mg_p = _merge(oa_p, ob_p, oc_p, wb, pp, 1024)
    x1p = _outproj(mg_p, w_out, xp, mod_p, 1024)

    ck, cv, s0, dconv0, h0, lconv0 = st
    h1s = _adaln(xs, g1, mod_s, 0, 1, MS)
    ps = _matmul(h1s, w_main, F32, MS, 512, "proj")
    psab = _matmul(h1s, w_ab, F32, MS, 128, "proj_ab")
    q_r = ps[:, P_Q:P_K].reshape(DEC_BATCH, DEC_SEQ, NKV, GRP, HD).transpose(0, 2, 3, 1, 4)
    q_r = q_r.reshape(DEC_BATCH, NKV, GRP * DEC_SEQ, HD)
    pad_t = ((0, 0), (0, 0), (0, 8 - DEC_SEQ), (0, 0))
    kn_r = jnp.pad(ps[:, P_K:P_V].reshape(DEC_BATCH, DEC_SEQ, NKV, HD).transpose(0, 2, 1, 3), pad_t)
    vn_raw = ps[:, P_V:P_W].reshape(DEC_BATCH, DEC_SEQ, NKV, HD)
    vn_r = jnp.pad(vn_raw.transpose(0, 2, 1, 3), pad_t)
    oa_r, kno = _attn_sample(q_r, kn_r, vn_r, ck.transpose(0, 2, 1, 3), cv.transpose(0, 2, 1, 3),
                             qg, kg, sinks)
    oa_s = oa_r.reshape(DEC_BATCH, NKV, GRP, DEC_SEQ, HD).transpose(0, 3, 1, 2, 4)
    oa_s = oa_s.reshape(MS, 1024).astype(BF16)
    k_s = jnp.concatenate([ck[:, DEC_SEQ:], kno[:, :, :DEC_SEQ].transpose(0, 2, 1, 3)], axis=1)
    v_s = jnp.concatenate([cv[:, DEC_SEQ:], vn_raw], axis=1)

    x_dq = ps[:, P_DQKV:P_DQKV + CONV_CH].reshape(DEC_BATCH, DEC_SEQ, CONV_CH)
    xp8 = jnp.concatenate([jnp.zeros((DEC_BATCH, 1, CONV_CH), F32), dconv0, x_dq], axis=1)
    pad8 = ((0, 0), (0, 8 - DEC_SEQ), (0, 0))
    dz8 = jnp.pad(ps[:, P_DZ:P_LX].reshape(DEC_BATCH, DEC_SEQ, 1024), pad8)
    dab8 = jnp.pad(psab.reshape(DEC_BATCH, DEC_SEQ, 128), pad8)
    ob_r, d_s = _delta_sample(xp8, dz8, dab8, s0, dn_cw, alog_row, dtb_row, dn_ng)
    ob_s = ob_r[:, :DEC_SEQ].reshape(MS, 1024).astype(BF16)
    dc_s = x_dq[:, DEC_SEQ - (CONV_W - 1):]

    x_lx = ps[:, P_LX:P_Q].reshape(DEC_BATCH, DEC_SEQ, LW)
    oc_tm = _lru_sample(x_lx.transpose(1, 0, 2), lconv0.transpose(1, 0, 2), h0, l_cw, l_cb,
                        l_wr, l_br, l_wi, l_bi, l_lam)
    oc_s = oc_tm.transpose(1, 0, 2).reshape(MS, LW).astype(BF16)
    h_s = oc_tm[DEC_SEQ - 1]
    hc_s = x_lx[:, DEC_SEQ - (CONV_W - 1):]

    mg_s = _merge(oa_s, ob_s, oc_s, wb, ps, MS)
    x1s = _outproj(mg_s, w_out, xs, mod_s, MS)

    h2_all, te_all, tw_all = _router(x1p, x1s, g2, mod_p, mod_s, wr, br)
    dest3, out_rows = _moe(l, h2_all, te_all, w["w_gate_up"], w["b_gate_up"],
                           w["w_down"], w["b_down"])
    np_t = MP // TOK_TILE
    xp_new = _combine(dest3, out_rows, tw_all, x1p, mod_p, 0)
    xs_new = _combine(dest3, out_rows, tw_all, x1s, mod_s, np_t)

    st_p = (k_p.reshape(BATCH, WINDOW, NKV, HD), v_p.reshape(BATCH, WINDOW, NKV, HD), d_p, dc_p,
            h_p.reshape(BATCH, LW), hc_p)
    st_s = (k_s, v_s, d_s, dc_s, h_s, hc_s)
    return xp_new, xs_new, st_p, st_s


def kernel(x_prompt, x_sample, cache_k, cache_v, state_delta, state_delta_conv, state_lru, state_lru_conv, c_prompt, c_sample, w_ada, b_ada, norm1_g, norm2_g, w_in, q_norm_g, k_norm_g, sinks, dn_conv_w, dn_a_log, dn_dt_bias, dn_norm_g, lru_conv_w, lru_conv_b, lru_w_r, lru_b_r, lru_w_i, lru_b_i, lru_lambda, w_branch, w_out, w_router, b_router, w_gate_up, b_gate_up, w_down, b_down):
    w = dict(w_in=w_in, w_branch=w_branch, w_out=w_out, w_router=w_router, b_router=b_router,
             norm1_g=norm1_g, norm2_g=norm2_g, q_norm_g=q_norm_g, k_norm_g=k_norm_g, sinks=sinks,
             dn_conv_w=dn_conv_w, dn_a_log=dn_a_log, dn_dt_bias=dn_dt_bias, dn_norm_g=dn_norm_g,
             lru_conv_w=lru_conv_w, lru_conv_b=lru_conv_b, lru_w_r=lru_w_r, lru_b_r=lru_b_r,
             lru_w_i=lru_w_i, lru_b_i=lru_b_i, lru_lambda=lru_lambda, w_gate_up=w_gate_up,
             b_gate_up=b_gate_up, w_down=w_down, b_down=b_down)
    n_c = BATCH + DEC_BATCH
    c_all = jnp.concatenate([c_prompt, c_sample, jnp.zeros((40 - n_c, D), F32)], axis=0)
    mod_all = _ada_mod(c_all, w_ada, b_ada)

    xp = x_prompt.reshape(MP, D)
    xs = x_sample.reshape(MS, D)
    new_p, new_s = [], []
    for l in range(DEPTH):
        mod_p = mod_all[l, :BATCH].reshape(BATCH, 6, 1, D).transpose(1, 0, 2, 3)
        mod_s = jnp.repeat(mod_all[l, BATCH:n_c].reshape(DEC_BATCH, 6, D), DEC_SEQ, axis=0)
        mod_s = mod_s.transpose(1, 0, 2).reshape(6, 1, MS, D)
        st = (cache_k[l], cache_v[l], state_delta[l], state_delta_conv[l], state_lru[l],
              state_lru_conv[l])
        xp, xs, st_p, st_s = _layer(l, xp, xs, st, w, mod_p, mod_s)
        new_p.append(st_p)
        new_s.append(st_s)
    k_p, v_p, d_p, dc_p, h_p, hc_p = (jnp.stack(z) for z in zip(*new_p))
    k_s, v_s, d_s, dc_s, h_s, hc_s = (jnp.stack(z) for z in zip(*new_s))
    return (xp.reshape(BATCH, SEQ, D), xs.reshape(DEC_BATCH, DEC_SEQ, D),
            k_p, v_p, d_p, dc_p, h_p, hc_p, k_s, v_s, d_s, dc_s, h_s, hc_s)
```

```python
import functools

import jax
import jax.numpy as jnp
from jax import lax
from jax.experimental import pallas as pl
from jax.experimental.pallas import tpu as pltpu

F32 = jnp.float32
BF16 = jnp.bfloat16
I32 = jnp.int32

D = 2048
BATCH = 2
SEQ = 4096
DEC_BATCH = 32
DEC_SEQ = 4
DEPTH = 2
MP = BATCH * SEQ
MS = DEC_BATCH * DEC_SEQ
NTOK = MP + MS

WINDOW = 128
HD = 64
NH = 16
NKV = 4
GRP = NH // NKV
ATT_SCALE = HD ** -0.5

DK = 128
DNH = 8
CHUNK = 64
CONV_CH = 3 * DNH * DK
CONV_W = 4

LW = 1024
LBLK = 128
LNB = LW // LBLK
LRU_C = 8.0

NE = 32
TOPK = 4
DFF = 2048
SW_ALPHA = 1.702
SW_LIMIT = 7.0
EPS = 1e-6

P_DQKV = 0
P_G = 3072
P_DZ = 9216
P_LX = 10240
P_Q = 11264
P_K = 12288
P_V = 12544
P_W = 12800

RB = 256
NB_MAX = -(-(NTOK * TOPK) // RB) + NE
ROWS = NB_MAX * RB
TN_E = 1024
NJ = DFF // TN_E
TOK_TILE = 128
N_TT = NTOK // TOK_TILE

VMEM_BIG = 56 * 1024 * 1024
VMEM_GMM = 60 * 1024 * 1024


def _sds(shape, dtype):
    return jax.ShapeDtypeStruct(shape, dtype)


def _cparams(n_axes, vmem=None):
    return pltpu.CompilerParams(dimension_semantics=("arbitrary",) * n_axes, vmem_limit_bytes=vmem)


def _dot(a, b):
    return jnp.dot(a, b, preferred_element_type=F32)


def _dot_nt(a, b):
    return lax.dot_general(a, b, (((1,), (1,)), ((), ())), preferred_element_type=F32)


def _dot_tn(a, b):
    return lax.dot_general(a, b, (((0,), (0,)), ((), ())), preferred_element_type=F32)


def _rms(x, g):
    return x * lax.rsqrt(jnp.mean(x * x, axis=-1, keepdims=True) + EPS) * g


def _sigmoid(x):
    return 1.0 / (1.0 + jnp.exp(-x))


def _silu(x):
    return x * _sigmoid(x)


def _softplus(x):
    return jnp.maximum(x, 0.0) + jnp.log1p(jnp.exp(-jnp.abs(x)))


def _ada_kernel(c_ref, w_ref, b_ref, o_ref):
    a = _silu(c_ref[...]).astype(BF16)
    o_ref[0] = _dot(a, w_ref[0].astype(BF16)) + b_ref[0]


def _ada_mod(c_all, w_ada, b_ada):
    rows = c_all.shape[0]
    tn = 1024
    return pl.pallas_call(
        _ada_kernel,
        grid=(DEPTH, 6 * D // tn),
        in_specs=[
            pl.BlockSpec((rows, D), lambda l, n: (0, 0)),
            pl.BlockSpec((1, D, tn), lambda l, n: (l, 0, n)),
            pl.BlockSpec((1, 1, tn), lambda l, n: (l, 0, n)),
        ],
        out_specs=pl.BlockSpec((1, rows, tn), lambda l, n: (l, 0, n)),
        out_shape=_sds((DEPTH, rows, 6 * D), F32),
        compiler_params=_cparams(2, VMEM_BIG),
        name="ada_mod",
    )(c_all, w_ada, b_ada.reshape(DEPTH, 1, 6 * D))


def _adaln_kernel(x_ref, g_ref, sh_ref, sc_ref, o_ref):
    y = _rms(x_ref[...], g_ref[...])
    o_ref[...] = (y * (1.0 + sc_ref[0, 0]) + sh_ref[0, 0]).astype(o_ref.dtype)


def _mod_spec(mod, j, tiles, tn=D, with_n=False):
    g, rb = mod.shape[1], mod.shape[2]
    tpg = tiles // g
    if with_n:
        return pl.BlockSpec((1, 1, rb, tn), lambda n, m: (j, m // tpg, 0, n))
    return pl.BlockSpec((1, 1, rb, tn), lambda m: (j, m // tpg, 0, 0))


def _adaln(x, g, mod, j_sh, j_sc, tm):
    m_rows = x.shape[0]
    tiles = m_rows // tm
    return pl.pallas_call(
        _adaln_kernel,
        grid=(tiles,),
        in_specs=[
            pl.BlockSpec((tm, D), lambda m: (m, 0)),
            pl.BlockSpec((1, D), lambda m: (0, 0)),
            _mod_spec(mod, j_sh, tiles),
            _mod_spec(mod, j_sc, tiles),
        ],
        out_specs=pl.BlockSpec((tm, D), lambda m: (m, 0)),
        out_shape=_sds((m_rows, D), BF16),
        compiler_params=_cparams(1),
        name="adaln1",
    )(x, g, mod, mod)


def _mm_kernel(x_ref, w_ref, o_ref):
    o_ref[...] = _dot(x_ref[...], w_ref[...]).astype(o_ref.dtype)


def _matmul(x, w, out_dtype, tm, tn, name):
    m_rows, k = x.shape
    n_cols = w.shape[1]
    return pl.pallas_call(
        _mm_kernel,
        grid=(n_cols // tn, m_rows // tm),
        in_specs=[
            pl.BlockSpec((tm, k), lambda n, m: (m, 0)),
            pl.BlockSpec((k, tn), lambda n, m: (0, n)),
        ],
        out_specs=pl.BlockSpec((tm, tn), lambda n, m: (m, n)),
        out_shape=_sds((m_rows, n_cols), out_dtype),
        compiler_params=_cparams(2, VMEM_BIG),
        name=name,
    )(x, w)


def _attn_prompt_kernel(sinks_ref, q_ref, kc_ref, vc_ref, kp_ref, vp_ref, qg_ref, kg_ref,
                        o_ref, ko_ref, vo_ref):
    nblk = SEQ // WINDOW
    first = (pl.program_id(0) % nblk) == 0
    row = lax.broadcasted_iota(I32, (WINDOW, 2 * WINDOW), 0)
    col = lax.broadcasted_iota(I32, (WINDOW, 2 * WINDOW), 1)
    lo = jnp.where(first, WINDOW, 0)
    mask = (col > row) & (col <= row + WINDOW) & (col >= lo)
    qg = qg_ref[...]
    kg = kg_ref[...]
    outs = []
    k_out = []
    for kh in range(NKV):
        ks = slice(kh * HD, (kh + 1) * HD)
        kc_n = _rms(kc_ref[:, ks], kg)
        kp_n = _rms(kp_ref[:, ks], kg)
        k_out.append(kc_n)
        kk = jnp.concatenate([kp_n, kc_n], axis=0).astype(BF16)
        vv = jnp.concatenate([vp_ref[:, ks], vc_ref[:, ks]], axis=0).astype(BF16)
        for g in range(GRP):
            h = kh * GRP + g
            qh = _rms(q_ref[:, h * HD:(h + 1) * HD], qg).astype(BF16)
            s = _dot_nt(qh, kk) * ATT_SCALE
            s = jnp.where(mask, s, -jnp.inf)
            sink = sinks_ref[h]
            m = jnp.maximum(jnp.max(s, axis=-1, keepdims=True), sink)
            p = jnp.exp(s - m)
            den = jnp.sum(p, axis=-1, keepdims=True) + jnp.exp(sink - m)
            outs.append(_dot(p.astype(BF16), vv) / den)
    o_ref[...] = jnp.concatenate(outs, axis=-1).astype(o_ref.dtype)
    ko_ref[0] = jnp.concatenate(k_out, axis=-1)
    vo_ref[0] = vc_ref[...]


def _attn_prompt(p_act, qg, kg, sinks):
    nblk = SEQ // WINDOW
    cq, ck, cv = P_Q // 1024, P_K // 256, P_V // 256
    return pl.pallas_call(
        _attn_prompt_kernel,
        grid=(MP // WINDOW,),
        in_specs=[
            pl.BlockSpec(memory_space=pltpu.SMEM),
            pl.BlockSpec((WINDOW, 1024), lambda g: (g, cq)),
            pl.BlockSpec((WINDOW, 256), lambda g: (g, ck)),
            pl.BlockSpec((WINDOW, 256), lambda g: (g, cv)),
            pl.BlockSpec((WINDOW, 256), lambda g: (jnp.maximum(g - 1, 0), ck)),
            pl.BlockSpec((WINDOW, 256), lambda g: (jnp.maximum(g - 1, 0), cv)),
            pl.BlockSpec((1, HD), lambda g: (0, 0)),
            pl.BlockSpec((1, HD), lambda g: (0, 0)),
        ],
        out_specs=[
            pl.BlockSpec((WINDOW, 1024), lambda g: (g, 0)),
            pl.BlockSpec((1, WINDOW, 256), lambda g: (g // nblk, 0, 0)),
            pl.BlockSpec((1, WINDOW, 256), lambda g: (g // nblk, 0, 0)),
        ],
        out_shape=[_sds((MP, 1024), BF16), _sds((BATCH, WINDOW, 256), F32),
                   _sds((BATCH, WINDOW, 256), F32)],
        compiler_params=_cparams(1),
        name="attn_prompt",
    )(sinks, p_act, p_act, p_act, p_act, p_act, qg, kg)


def _attn_sample_kernel(sinks_ref, q_ref, kn_ref, vn_ref, ck_ref, cv_ref, qg_ref, kg_ref,
                        o_ref, kno_ref):
    rows = GRP * DEC_SEQ
    t = lax.broadcasted_iota(I32, (rows, 1), 0) % DEC_SEQ
    g_of_row = lax.broadcasted_iota(I32, (rows, 1), 0) // DEC_SEQ
    col_c = lax.broadcasted_iota(I32, (rows, WINDOW), 1)
    col_n = lax.broadcasted_iota(I32, (rows, 8), 1)
    qg = qg_ref[...]
    kg = kg_ref[...]
    for kh in range(NKV):
        q16 = _rms(q_ref[0, kh], qg).astype(BF16)
        kn = _rms(kn_ref[0, kh], kg)
        kno_ref[0, kh] = kn
        s_c = _dot_nt(q16, ck_ref[0, kh].astype(BF16)) * ATT_SCALE
        s_n = _dot_nt(q16, kn.astype(BF16)) * ATT_SCALE
        s_c = jnp.where(col_c > t, s_c, -jnp.inf)
        s_n = jnp.where(col_n <= t, s_n, -jnp.inf)
        sink = jnp.zeros((rows, 1), F32)
        for g in range(GRP):
            sink = jnp.where(g_of_row == g, sinks_ref[kh * GRP + g], sink)
        m = jnp.maximum(jnp.maximum(jnp.max(s_c, axis=-1, keepdims=True),
                                    jnp.max(s_n, axis=-1, keepdims=True)), sink)
        p_c = jnp.exp(s_c - m)
        p_n = jnp.exp(s_n - m)
        den = (jnp.sum(p_c, axis=-1, keepdims=True) + jnp.sum(p_n, axis=-1, keepdims=True)
               + jnp.exp(sink - m))
        o = _dot(p_c.astype(BF16), cv_ref[0, kh].astype(BF16)) + _dot(
            p_n.astype(BF16), vn_ref[0, kh].astype(BF16))
        o_ref[0, kh] = o / den


def _attn_sample(q_r, kn_r, vn_r, ck_r, cv_r, qg, kg, sinks):
    rows = GRP * DEC_SEQ
    return pl.pallas_call(
        _attn_sample_kernel,
        grid=(DEC_BATCH,),
        in_specs=[
            pl.BlockSpec(memory_space=pltpu.SMEM),
            pl.BlockSpec((1, NKV, rows, HD), lambda b: (b, 0, 0, 0)),
            pl.BlockSpec((1, NKV, 8, HD), lambda b: (b, 0, 0, 0)),
            pl.BlockSpec((1, NKV, 8, HD), lambda b: (b, 0, 0, 0)),
            pl.BlockSpec((1, NKV, WINDOW, HD), lambda b: (b, 0, 0, 0)),
            pl.BlockSpec((1, NKV, WINDOW, HD), lambda b: (b, 0, 0, 0)),
            pl.BlockSpec((1, HD), lambda b: (0, 0)),
            pl.BlockSpec((1, HD), lambda b: (0, 0)),
        ],
        out_specs=[
            pl.BlockSpec((1, NKV, rows, HD), lambda b: (b, 0, 0, 0)),
            pl.BlockSpec((1, NKV, 8, HD), lambda b: (b, 0, 0, 0)),
        ],
        out_shape=[_sds((DEC_BATCH, NKV, rows, HD), F32), _sds((DEC_BATCH, NKV, 8, HD), F32)],
        compiler_params=_cparams(1),
        name="attn_sample",
    )(sinks, q_r, kn_r, vn_r, ck_r, cv_r, qg, kg)


def _cumsum_rows(x, c):
    row = lax.broadcasted_iota(I32, x.shape, 0)
    s = 1
    while s < c:
        x = x + jnp.where(row >= s, pltpu.roll(x, s, 0), 0.0)
        s *= 2
    return x


def _delta_chunk(c, nstack, conv_slice, g_full, beta_full, z_slice, norm_g, s_get, s_put, o_put):
    n = nstack * c
    lg = c.bit_length() - 1
    r = lax.broadcasted_iota(I32, (n, n), 0)
    cc = lax.broadcasted_iota(I32, (n, n), 1)
    same = (r >> lg) == (cc >> lg)
    incl = same & (r >= cc)
    strict = same & (r > cc)
    eye_b = r == cc
    eye = eye_b.astype(F32)
    gc = _cumsum_rows(g_full, c)
    for st in range(DNH // nstack):
        heads = range(st * nstack, (st + 1) * nstack)
        qs, ks, vs, gcs, bs = [], [], [], [], []
        for h in heads:
            qh = conv_slice(slice(h * DK, (h + 1) * DK))
            kh = conv_slice(slice(DNH * DK + h * DK, DNH * DK + (h + 1) * DK))
            vs.append(conv_slice(slice(2 * DNH * DK + h * DK, 2 * DNH * DK + (h + 1) * DK)))
            qs.append(qh * lax.rsqrt(jnp.sum(qh * qh, axis=-1, keepdims=True) + EPS) * (DK ** -0.5))
            ks.append(kh * lax.rsqrt(jnp.sum(kh * kh, axis=-1, keepdims=True) + EPS))
            gcs.append(gc[:, h:h + 1])
            bs.append(beta_full[:, DNH + h:DNH + h + 1])
        q = jnp.concatenate(qs, axis=0)
        k = jnp.concatenate(ks, axis=0)
        v = jnp.concatenate(vs, axis=0)
        gcol = jnp.concatenate(gcs, axis=0)
        bcol = jnp.concatenate(bs, axis=0)
        grow = jnp.sum(jnp.where(eye_b, gcol, 0.0), axis=0, keepdims=True)
        decay = jnp.exp(jnp.where(incl, gcol - grow, -jnp.inf))
        egc = jnp.exp(gcol)
        kb = k * bcol
        k16 = k.astype(BF16)
        a_mat = jnp.where(strict, _dot_nt(kb.astype(BF16), k16) * decay, 0.0)
        blk = 1
        t_inv = eye
        while blk < c:
            sh = blk.bit_length()
            pair = ((r >> sh) == (cc >> sh)) & ((r & blk) != 0) & ((cc & blk) == 0)
            off = jnp.where(pair, a_mat, 0.0)
            if blk == 1:
                t_inv = t_inv - off
            else:
                t16 = t_inv.astype(BF16)
                t_inv = t_inv - _dot(t16, _dot(off.astype(BF16), t16).astype(BF16))
            blk *= 2
        rhs = jnp.concatenate([v * bcol, kb * egc], axis=-1).astype(BF16)
        sol = _dot(t_inv.astype(BF16), rhs)
        qk16 = (_dot_nt(q.astype(BF16), k16) * decay).astype(BF16)
        q_dec = q * egc
        us, s_olds, g_lasts = [], [], []
        for i, h in enumerate(heads):
            hs = slice(i * c, (i + 1) * c)
            s_old = s_get(h)
            us.append(sol[hs, :DK] - _dot(sol[hs, DK:].astype(BF16), s_old.astype(BF16)))
            s_olds.append(s_old)
            g_lasts.append(gcol[(i + 1) * c - 1:(i + 1) * c, :])
        u16 = jnp.concatenate(us, axis=0).astype(BF16)
        o_in = _dot(qk16, u16)
        for i, h in enumerate(heads):
            hs = slice(i * c, (i + 1) * c)
            s16 = s_olds[i].astype(BF16)
            o = _dot(q_dec[hs].astype(BF16), s16) + o_in[hs]
            k_dec = k[hs] * jnp.exp(g_lasts[i] - gcol[hs])
            s_put(h, s_olds[i] * jnp.exp(g_lasts[i]) + _dot_tn(k_dec.astype(BF16), u16[hs]))
            zz = z_slice(slice(h * DK, (h + 1) * DK))
            o_put(h, _rms(o, norm_g) * _silu(zz))


def _delta_prompt_kernel(x_ref, prev_ref, dz_ref, dab_ref, cw_ref, alog_ref, dtb_ref, ng_ref,
                         o_ref, s_ref, dc_ref):
    c = pl.program_id(0)

    @pl.when(c == 0)
    def _():
        s_ref[...] = jnp.zeros_like(s_ref)

    row8 = lax.broadcasted_iota(I32, (8, DK), 0)
    keep_prev = c > 0
    for b in range(BATCH):
        def conv_slice(cs, b=b):
            x = x_ref[b, :, cs]
            prev = jnp.where(keep_prev, prev_ref[b, :, cs], 0.0)
            y = x * cw_ref[CONV_W - 1:CONV_W, cs]
            for s in range(1, CONV_W):
                xr = pltpu.roll(x, s, 0)
                top = jnp.where(row8 < s, pltpu.roll(prev, s, 0), xr[0:8])
                sh = jnp.concatenate([top, xr[8:]], axis=0)
                y = y + sh * cw_ref[CONV_W - 1 - s:CONV_W - s, cs]
            return _silu(y)

        dab = dab_ref[b]
        g_full = -jnp.exp(alog_ref[...]) * _softplus(dab + dtb_ref[...])
        beta_full = _sigmoid(dab)

        def s_get(h, b=b):
            return s_ref[b, h]

        def s_put(h, v, b=b):
            s_ref[b, h] = v

        def o_put(h, v, b=b):
            o_ref[b, :, h * DK:(h + 1) * DK] = v.astype(o_ref.dtype)

        _delta_chunk(CHUNK, 4, conv_slice, g_full, beta_full, lambda cs, b=b: dz_ref[b, :, cs],
                     ng_ref[...], s_get, s_put, o_put)

    @pl.when(c == pl.num_programs(0) - 1)
    def _():
        dc_ref[...] = x_ref[:, CHUNK - (CONV_W - 1):CHUNK, :]


def _delta_prompt(p_act, p_ab, conv_w, alog_row, dtb_row, norm_g):
    nck = SEQ // CHUNK
    p3 = p_act.reshape(BATCH, SEQ, p_act.shape[1])
    ab3 = p_ab.reshape(BATCH, SEQ, 128)
    return pl.pallas_call(
        _delta_prompt_kernel,
        grid=(nck,),
        in_specs=[
            pl.BlockSpec((BATCH, CHUNK, CONV_CH), lambda c: (0, c, P_DQKV // CONV_CH)),
            pl.BlockSpec((BATCH, 8, CONV_CH),
                         lambda c: (0, jnp.maximum(c * (CHUNK // 8) - 1, 0), P_DQKV // CONV_CH)),
            pl.BlockSpec((BATCH, CHUNK, 1024), lambda c: (0, c, P_DZ // 1024)),
            pl.BlockSpec((BATCH, CHUNK, 128), lambda c: (0, c, 0)),
            pl.BlockSpec((CONV_W, CONV_CH), lambda c: (0, 0)),
            pl.BlockSpec((1, 128), lambda c: (0, 0)),
            pl.BlockSpec((1, 128), lambda c: (0, 0)),
            pl.BlockSpec((1, DK), lambda c: (0, 0)),
        ],
        out_specs=[
            pl.BlockSpec((BATCH, CHUNK, 1024), lambda c: (0, c, 0)),
            pl.BlockSpec((BATCH, DNH, DK, DK), lambda c: (0, 0, 0, 0)),
            pl.BlockSpec((BATCH, CONV_W - 1, CONV_CH), lambda c: (0, 0, 0)),
        ],
        out_shape=[_sds((BATCH, SEQ, 1024), BF16), _sds((BATCH, DNH, DK, DK), F32),
                   _sds((BATCH, CONV_W - 1, CONV_CH), F32)],
        compiler_params=_cparams(1),
        name="delta_prompt",
    )(p3, p3, p3, ab3, conv_w, alog_row, dtb_row, norm_g)


def _delta_sample_kernel(xp_ref, dz_ref, dab_ref, s0_ref, cw_ref, alog_ref, dtb_ref, ng_ref,
                         o_ref, s_ref):
    row = lax.broadcasted_iota(I32, (8, DK), 0)
    live = row < DEC_SEQ

    def conv_slice(cs):
        xp = xp_ref[0, :, cs]
        y = jnp.zeros((8, DK), F32)
        for j in range(CONV_W):
            y = y + pltpu.roll(xp, 8 - 1 - j, 0) * cw_ref[j:j + 1, cs]
        return jnp.where(live, _silu(y), 0.0)

    dab = dab_ref[0]
    g_full = jnp.where(live, -jnp.exp(alog_ref[...]) * _softplus(dab + dtb_ref[...]), 0.0)
    beta_full = jnp.where(live, _sigmoid(dab), 0.0)

    def s_put(h, v):
        s_ref[0, h] = v

    def o_put(h, v):
        o_ref[0, :, h * DK:(h + 1) * DK] = v

    _delta_chunk(8, DNH, conv_slice, g_full, beta_full, lambda cs: dz_ref[0, :, cs], ng_ref[...],
                 lambda h: s0_ref[0, h], s_put, o_put)


def _delta_sample(xp8, dz8, dab8, s0, conv_w, alog_row, dtb_row, norm_g):
    return pl.pallas_call(
        _delta_sample_kernel,
        grid=(DEC_BATCH,),
        in_specs=[
            pl.BlockSpec((1, 8, CONV_CH), lambda b: (b, 0, 0)),
            pl.BlockSpec((1, 8, 1024), lambda b: (b, 0, 0)),
            pl.BlockSpec((1, 8, 128), lambda b: (b, 0, 0)),
            pl.BlockSpec((1, DNH, DK, DK), lambda b: (b, 0, 0, 0)),
            pl.BlockSpec((CONV_W, CONV_CH), lambda b: (0, 0)),
            pl.BlockSpec((1, 128), lambda b: (0, 0)),
            pl.BlockSpec((1, 128), lambda b: (0, 0)),
            pl.BlockSpec((1, DK), lambda b: (0, 0)),
        ],
        out_specs=[
            pl.BlockSpec((1, 8, 1024), lambda b: (b, 0, 0)),
            pl.BlockSpec((1, DNH, DK, DK), lambda b: (b, 0, 0, 0)),
        ],
        out_shape=[_sds((DEC_BATCH, 8, 1024), F32), _sds((DEC_BATCH, DNH, DK, DK), F32)],
        compiler_params=_cparams(1),
        name="delta_sample",
    )(xp8, dz8, dab8, s0, conv_w, alog_row, dtb_row, norm_g)


def _lru_gates(xc, wr, br, wi, bi, sp):
    x16 = xc.astype(BF16)
    r = _sigmoid(_dot(x16, wr) + br)
    i = _sigmoid(_dot(x16, wi) + bi)
    log_a = -LRU_C * r * sp
    a = jnp.exp(log_a)
    th = jnp.tanh(log_a)
    u = jnp.sqrt(-2.0 * th / (1.0 - th)) * (i * xc)
    return a, u


LRU_TT = 256


def _lru_prompt_kernel(x_ref, prev_ref, cw_ref, cb_ref, wr_ref, br_ref, wi_ref, bi_ref, lam_ref,
                       o_ref, h_ref, lc_ref):
    t = pl.program_id(1)
    row8 = lax.broadcasted_iota(I32, (8, LBLK), 0)
    row = lax.broadcasted_iota(I32, (LRU_TT, LBLK), 0)
    keep = t > 0
    for n in range(LNB):
        cs = slice(n * LBLK, (n + 1) * LBLK)
        x = x_ref[:, cs]
        prev = jnp.where(keep, prev_ref[:, cs], 0.0)
        y = x * cw_ref[CONV_W - 1:CONV_W, cs]
        for s in range(1, CONV_W):
            xr = pltpu.roll(x, s, 0)
            top = jnp.where(row8 < s, pltpu.roll(prev, s, 0), xr[0:8])
            y = y + jnp.concatenate([top, xr[8:]], axis=0) * cw_ref[CONV_W - 1 - s:CONV_W - s, cs]
        xc = y + cb_ref[:, cs]
        sp = _softplus(-lam_ref[:, cs])
        a, u = _lru_gates(xc, wr_ref[n], br_ref[:, cs], wi_ref[n], bi_ref[:, cs], sp)
        h0 = jnp.where(keep, h_ref[0, :, cs], 0.0)
        u = u + jnp.where(row == 0, a * h0, 0.0)
        s = 1
        while s < LRU_TT:
            valid = row >= s
            u_s = pltpu.roll(u, s, 0)
            a_s = pltpu.roll(a, s, 0)
            u = jnp.where(valid, a * u_s + u, u)
            a = jnp.where(valid, a * a_s, a)
            s *= 2
        o_ref[:, cs] = u.astype(o_ref.dtype)
        h_ref[0, :, cs] = u[LRU_TT - 1:LRU_TT, :]

    @pl.when(t == pl.num_programs(1) - 1)
    def _():
        lc_ref[0] = x_ref[LRU_TT - (CONV_W - 1):LRU_TT, :]


def _lru_prompt(p_act, cw, cb, wr, br, wi, bi, lam):
    ntt = SEQ // LRU_TT
    cl = P_LX // LW
    vec = pl.BlockSpec((1, LW), lambda b, t: (0, 0))
    mat = pl.BlockSpec((LNB, LBLK, LBLK), lambda b, t: (0, 0, 0))
    return pl.pallas_call(
        _lru_prompt_kernel,
        grid=(BATCH, ntt),
        in_specs=[
            pl.BlockSpec((LRU_TT, LW), lambda b, t: (b * ntt + t, cl)),
            pl.BlockSpec((8, LW),
                         lambda b, t: (jnp.maximum(b * (SEQ // 8) + t * (LRU_TT // 8) - 1, 0), cl)),
            pl.BlockSpec((CONV_W, LW), lambda b, t: (0, 0)),
            vec, mat, vec, mat, vec, vec,
        ],
        out_specs=[
            pl.BlockSpec((LRU_TT, LW), lambda b, t: (b * ntt + t, 0)),
            pl.BlockSpec((1, 1, LW), lambda b, t: (b, 0, 0)),
            pl.BlockSpec((1, CONV_W - 1, LW), lambda b, t: (b, 0, 0)),
        ],
        out_shape=[_sds((MP, LW), BF16), _sds((BATCH, 1, LW), F32),
                   _sds((BATCH, CONV_W - 1, LW), F32)],
        compiler_params=_cparams(2),
        name="lru_prompt",
    )(p_act, p_act, cw, cb, wr, br, wi, bi, lam)


def _lru_sample_kernel(x_ref, buf_ref, h0_ref, cw_ref, cb_ref, wr_ref, br_ref, wi_ref, bi_ref,
                       lam_ref, o_ref):
    for n in range(LNB):
        cs = slice(n * LBLK, (n + 1) * LBLK)
        xp = [buf_ref[j, :, cs] for j in range(CONV_W - 1)] + [x_ref[t, :, cs] for t in range(DEC_SEQ)]
        xcs = []
        for t in range(DEC_SEQ):
            y = xp[t] * cw_ref[0:1, cs]
            for j in range(1, CONV_W):
                y = y + xp[t + j] * cw_ref[j:j + 1, cs]
            xcs.append(y + cb_ref[:, cs])
        xc = jnp.concatenate(xcs, axis=0)
        sp = _softplus(-lam_ref[:, cs])
        a, u = _lru_gates(xc, wr_ref[n], br_ref[:, cs], wi_ref[n], bi_ref[:, cs], sp)
        h = h0_ref[:, cs]
        for t in range(DEC_SEQ):
            rs = slice(t * DEC_BATCH, (t + 1) * DEC_BATCH)
            h = a[rs] * h + u[rs]
            o_ref[t, :, cs] = h


def _lru_sample(x_tm, buf_tm, h0, cw, cb, wr, br, wi, bi, lam):
    return pl.pallas_call(
        _lru_sample_kernel,
        out_shape=_sds((DEC_SEQ, DEC_BATCH, LW), F32),
        name="lru_sample",
    )(x_tm, buf_tm, h0, cw, cb, wr, br, wi, bi, lam)


def _merge_kernel(oa_ref, ob_ref, oc_ref, w_ref, ga_ref, gb_ref, gc_ref, o_ref):
    acc = _sigmoid(ga_ref[...]) * _dot(oa_ref[...], w_ref[0])
    acc = acc + _sigmoid(gb_ref[...]) * _dot(ob_ref[...], w_ref[1])
    acc = acc + _sigmoid(gc_ref[...]) * _dot(oc_ref[...], w_ref[2])
    o_ref[...] = acc.astype(o_ref.dtype)


def _merge(o_a, o_b, o_c, wb, p_act, tm):
    m_rows = o_a.shape[0]
    tn = 512
    g0 = P_G // tn
    gs = D // tn
    br = pl.BlockSpec((tm, 1024), lambda n, m: (m, 0))
    return pl.pallas_call(
        _merge_kernel,
        grid=(D // tn, m_rows // tm),
        in_specs=[
            br, br, br,
            pl.BlockSpec((3, 1024, tn), lambda n, m: (0, 0, n)),
            pl.BlockSpec((tm, tn), lambda n, m: (m, g0 + n)),
            pl.BlockSpec((tm, tn), lambda n, m: (m, g0 + gs + n)),
            pl.BlockSpec((tm, tn), lambda n, m: (m, g0 + 2 * gs + n)),
        ],
        out_specs=pl.BlockSpec((tm, tn), lambda n, m: (m, n)),
        out_shape=_sds((m_rows, D), BF16),
        compiler_params=_cparams(2, VMEM_BIG),
        name="merge",
    )(o_a, o_b, o_c, wb, p_act, p_act, p_act)


def _outproj_kernel(a_ref, w_ref, x_ref, gt_ref, o_ref):
    o_ref[...] = x_ref[...] + gt_ref[0, 0] * _dot(a_ref[...], w_ref[...])


def _outproj(merged, w_out, x, mod, tm):
    m_rows = x.shape[0]
    tn = 1024
    tiles = m_rows // tm
    return pl.pallas_call(
        _outproj_kernel,
        grid=(D // tn, tiles),
        in_specs=[
            pl.BlockSpec((tm, D), lambda n, m: (m, 0)),
            pl.BlockSpec((D, tn), lambda n, m: (0, n)),
            pl.BlockSpec((tm, tn), lambda n, m: (m, n)),
            _mod_spec(mod, 2, tiles, tn, with_n=True),
        ],
        out_specs=pl.BlockSpec((tm, tn), lambda n, m: (m, n)),
        out_shape=_sds((m_rows, D), F32),
        compiler_params=_cparams(2, VMEM_BIG),
        name="outproj",
    )(merged, w_out, x, mod)


def _router_kernel(xp_ref, xs_ref, g_ref, shp_ref, scp_ref, shs_ref, scs_ref, wr_ref, br_ref,
                   h_ref, te_ref, tw_ref):
    tm = TOK_TILE
    is_s = pl.program_id(0) == N_TT - 1
    x = jnp.where(is_s, xs_ref[...], xp_ref[...])
    sc = jnp.where(is_s, scs_ref[0, 0], scp_ref[0, 0])
    sh = jnp.where(is_s, shs_ref[0, 0], shp_ref[0, 0])
    h = _rms(x, g_ref[...]) * (1.0 + sc) + sh
    h_ref[...] = h
    lane = lax.broadcasted_iota(I32, (tm, 128), 1)
    logits = _dot(h.astype(BF16), wr_ref[...]) + br_ref[...]
    logits = jnp.where(lane < NE, logits, -jnp.inf)
    lane4 = lax.broadcasted_iota(I32, (tm, TOPK), 1)
    te = jnp.zeros((tm, TOPK), I32)
    tl = jnp.zeros((tm, TOPK), F32)
    for k in range(TOPK):
        m = jnp.max(logits, axis=-1, keepdims=True)
        idx = jnp.min(jnp.where(logits == m, lane, 128), axis=-1, keepdims=True)
        te = jnp.where(lane4 == k, idx, te)
        tl = jnp.where(lane4 == k, m, tl)
        logits = jnp.where(lane == idx, -jnp.inf, logits)
    e = jnp.exp(tl - tl[:, 0:1])
    te_ref[...] = te
    tw_ref[...] = e / jnp.sum(e, axis=-1, keepdims=True)


def _router(x1p, x1s, g, mod_p, mod_s, wr, br):
    np_t = MP // TOK_TILE
    tpg = np_t // BATCH
    last_p = np_t - 1

    def mp(j):
        return pl.BlockSpec((1, 1, 1, D), lambda m: (j, jnp.minimum(m, last_p) // tpg, 0, 0))

    def ms(j):
        return pl.BlockSpec((1, 1, MS, D), lambda m: (j, 0, 0, 0))

    return pl.pallas_call(
        _router_kernel,
        grid=(N_TT,),
        in_specs=[
            pl.BlockSpec((TOK_TILE, D), lambda m: (jnp.minimum(m, last_p), 0)),
            pl.BlockSpec((MS, D), lambda m: (0, 0)),
            pl.BlockSpec((1, D), lambda m: (0, 0)),
            mp(3), mp(4), ms(3), ms(4),
            pl.BlockSpec((D, 128), lambda m: (0, 0)),
            pl.BlockSpec((1, 128), lambda m: (0, 0)),
        ],
        out_specs=[
            pl.BlockSpec((TOK_TILE, D), lambda m: (m, 0)),
            pl.BlockSpec((TOK_TILE, TOPK), lambda m: (m, 0)),
            pl.BlockSpec((TOK_TILE, TOPK), lambda m: (m, 0)),
        ],
        out_shape=[_sds((NTOK, D), F32), _sds((NTOK, TOPK), I32), _sds((NTOK, TOPK), F32)],
        compiler_params=_cparams(1),
        name="router",
    )(x1p, x1s, g, mod_p, mod_p, mod_s, mod_s, wr, br)


def _rank_kernel(te_ref, rk_ref, cnt_ref):
    @pl.when(pl.program_id(0) == 0)
    def _():
        cnt_ref[...] = jnp.zeros_like(cnt_ref)

    tt = TOK_TILE
    lane = lax.broadcasted_iota(I32, (tt, 128), 1)
    te = te_ref[...]
    hot = jnp.zeros((tt, 128), F32)
    for k in range(TOPK):
        hot = hot + (lane == te[:, k:k + 1]).astype(F32)
    r = lax.broadcasted_iota(I32, (tt, tt), 0)
    c = lax.broadcasted_iota(I32, (tt, tt), 1)
    before = (r > c).astype(BF16)
    tot = _dot(before, hot.astype(BF16)) + cnt_ref[...]
    lane4 = lax.broadcasted_iota(I32, (tt, TOPK), 1)
    rk = jnp.zeros((tt, TOPK), F32)
    for k in range(TOPK):
        rk_k = jnp.sum(jnp.where(lane == te[:, k:k + 1], tot, 0.0), axis=-1, keepdims=True)
        rk = jnp.where(lane4 == k, rk_k, rk)
    rk_ref[...] = rk.astype(I32)
    cnt_ref[...] = cnt_ref[...] + jnp.sum(hot, axis=0, keepdims=True)


def _rank(te):
    return pl.pallas_call(
        _rank_kernel,
        grid=(N_TT,),
        in_specs=[pl.BlockSpec((TOK_TILE, TOPK), lambda i: (i, 0))],
        out_specs=[pl.BlockSpec((TOK_TILE, TOPK), lambda i: (i, 0)),
                   pl.BlockSpec((1, 128), lambda i: (0, 0))],
        out_shape=[_sds((NTOK, TOPK), I32), _sds((1, 128), F32)],
        compiler_params=_cparams(1),
        name="rank",
    )(te)


def _scatter_kernel(dest_ref, last_ref, h_ref, xs_out, zbuf, zsem, sem):
    @pl.when(pl.program_id(0) == 0)
    def _():
        zbuf[...] = jnp.zeros_like(zbuf)

        def zero_block(row):
            return pltpu.make_async_copy(zbuf, xs_out.at[pl.ds(row, RB)], zsem)

        def extra_row(i):
            return pl.multiple_of(last_ref[NE] + i * RB, RB)

        n_extra = last_ref[NE + 1]
        for e in range(NE):
            zero_block(pl.multiple_of(last_ref[e], RB)).start()
        lax.fori_loop(0, n_extra, lambda i, c: (zero_block(extra_row(i)).start(), c)[1], 0)
        for e in range(NE):
            zero_block(pl.multiple_of(last_ref[e], RB)).wait()
        lax.fori_loop(0, n_extra, lambda i, c: (zero_block(extra_row(i)).wait(), c)[1], 0)

    def body(i, carry):
        for k in range(TOPK):
            d = dest_ref[0, 0, i * TOPK + k]
            pltpu.make_async_copy(h_ref.at[pl.ds(i, 1)], xs_out.at[pl.ds(d, 1)], sem).start(
                priority=k % 2)
        return carry

    lax.fori_loop(0, TOK_TILE, body, 0)
    for k in range(TOPK):
        pltpu.make_async_copy(h_ref, xs_out.at[pl.ds(0, TOK_TILE)], sem).wait()


def _scatter_rows(dest3, last_rows, h2):
    return pl.pallas_call(
        _scatter_kernel,
        grid=(N_TT,),
        in_specs=[
            pl.BlockSpec((1, 1, TOK_TILE * TOPK), lambda i: (i, 0, 0), memory_space=pltpu.SMEM),
            pl.BlockSpec(memory_space=pltpu.SMEM),
            pl.BlockSpec((TOK_TILE, D), lambda i: (i, 0)),
        ],
        out_specs=pl.BlockSpec(memory_space=pl.ANY),
        out_shape=_sds((ROWS, D), F32),
        scratch_shapes=[pltpu.VMEM((RB, D), F32), pltpu.SemaphoreType.DMA(()),
                        pltpu.SemaphoreType.DMA(())],
        compiler_params=_cparams(1),
        name="scatter_rows",
    )(dest3, last_rows, h2)


def _zero_unowned(zsrc, dst_at, used_ref, zsem):
    n = NB_MAX - used_ref[0]

    def cp(i):
        return pltpu.make_async_copy(zsrc, dst_at(used_ref[0] + i), zsem)

    lax.fori_loop(0, n, lambda i, c: (cp(i).start(), c)[1], 0)
    lax.fori_loop(0, n, lambda i, c: (cp(i).wait(), c)[1], 0)


def _row_of(blk):
    return pl.multiple_of(blk * RB, RB)


SB = 3
SROWS = SB * RB


def _stream_rows(n, first_blk, rows, x_src, o_dst, xbuf, obuf, xsem, osem, compute):
    row0 = _row_of(first_blk)

    def x_copies(i, slot):
        r = pl.multiple_of(row0 + i * rows, RB)
        return [pltpu.make_async_copy(src, dst, xsem.at[slot])
                for src, dst in x_src(r, rows, xbuf.at[slot, pl.ds(0, rows)])]

    def o_copy(i, slot):
        r = pl.multiple_of(row0 + i * rows, RB)
        return pltpu.make_async_copy(obuf.at[slot, pl.ds(0, rows)], o_dst(r, rows), osem.at[slot])

    @pl.when(n > 0)
    def _():
        for cp in x_copies(0, 0):
            cp.start()

        def body(i, carry):
            slot = i % 2
            for cp in x_copies(i, slot):
                cp.wait()

            @pl.when(i + 1 < n)
            def _():
                for cp in x_copies(i + 1, 1 - slot):
                    cp.start()

            @pl.when(i >= 2)
            def _():
                o_copy(i - 2, slot).wait()

            compute(xbuf.at[slot, pl.ds(0, rows)], obuf.at[slot, pl.ds(0, rows)])
            o_copy(i, slot).start()
            return carry

        lax.fori_loop(0, n, body, 0)

        @pl.when(n >= 2)
        def _():
            o_copy(n - 2, n % 2).wait()

        o_copy(n - 1, (n - 1) % 2).wait()


def _gmm_up_kernel(layer, bstart, nblk, used, x_hbm, w_hbm, bg_ref, bl_ref, act_hbm,
                   wstage, wg_s, wl_s, xbuf, obuf, wsem, xsem, osem, zsem):
    s = pl.program_id(0)
    e = s // NJ
    j = s % NJ
    nb = nblk[e]
    b0 = bstart[e]

    def w_copy(step, half):
        col = pl.multiple_of((half * NJ + step % NJ) * TN_E, TN_E)
        return pltpu.make_async_copy(w_hbm.at[layer, step // NJ, :, pl.ds(col, TN_E)],
                                     wstage.at[half], wsem.at[half])

    @pl.when(s == 0)
    def _():
        w_copy(0, 0).start()
        w_copy(0, 1).start()
        obuf[0] = jnp.zeros(obuf.shape[1:], obuf.dtype)
        for jj in range(NJ):
            _zero_unowned(obuf.at[0, pl.ds(0, RB)],
                          lambda blk, jj=jj: act_hbm.at[jj, pl.ds(_row_of(blk), RB)], used, zsem)

    w_copy(s, 0).wait()
    w_copy(s, 1).wait()
    wg_s[...] = wstage[0].astype(BF16)
    wl_s[...] = wstage[1].astype(BF16)

    @pl.when(s + 1 < pl.num_programs(0))
    def _():
        w_copy(s + 1, 0).start()
        w_copy(s + 1, 1).start()

    def compute(x_ref, o_ref):
        x = x_ref[...].astype(BF16)
        glu = jnp.minimum(_dot(x, wg_s[...]) + bg_ref[0, 0], SW_LIMIT)
        lin = jnp.clip(_dot(x, wl_s[...]) + bl_ref[0, 0], -SW_LIMIT, SW_LIMIT)
        o_ref[...] = (glu * _sigmoid(SW_ALPHA * glu) * (lin + 1.0)).astype(o_ref.dtype)

    def x_src(r, rows, dst):
        return [(x_hbm.at[pl.ds(r, rows)], dst)]

    def o_dst(r, rows):
        return act_hbm.at[j, pl.ds(r, rows)]

    n_big = nb // SB
    _stream_rows(n_big, b0, SROWS, x_src, o_dst, xbuf, obuf, xsem, osem, compute)
    _stream_rows(nb - n_big * SB, b0 + n_big * SB, RB, x_src, o_dst, xbuf, obuf, xsem, osem,
                 compute)


def _gmm_up(tables, xs, w_gu, b_gu, layer):
    grid_spec = pltpu.PrefetchScalarGridSpec(
        num_scalar_prefetch=3,
        grid=(NE * NJ,),
        in_specs=[
            pl.BlockSpec(memory_space=pl.ANY),
            pl.BlockSpec(memory_space=pl.ANY),
            pl.BlockSpec((1, 1, 1, TN_E), lambda s, *_: (layer, s // NJ, 0, s % NJ)),
            pl.BlockSpec((1, 1, 1, TN_E), lambda s, *_: (layer, s // NJ, 0, NJ + s % NJ)),
        ],
        out_specs=pl.BlockSpec(memory_space=pl.ANY),
        scratch_shapes=[pltpu.VMEM((2, D, TN_E), F32),
                        pltpu.VMEM((D, TN_E), BF16), pltpu.VMEM((D, TN_E), BF16),
                        pltpu.VMEM((2, SROWS, D), F32), pltpu.VMEM((2, SROWS, TN_E), BF16),
                        pltpu.SemaphoreType.DMA((2,)), pltpu.SemaphoreType.DMA((2,)),
                        pltpu.SemaphoreType.DMA((2,)), pltpu.SemaphoreType.DMA(())],
    )
    return pl.pallas_call(
        functools.partial(_gmm_up_kernel, layer),
        grid_spec=grid_spec,
        out_shape=_sds((NJ, ROWS, TN_E), BF16),
        compiler_params=_cparams(1, VMEM_GMM),
        name="gmm_up",
    )(*tables, xs, w_gu, b_gu, b_gu)


def _gmm_down_kernel(layer, bstart, nblk, used, x_hbm, w_hbm, b_ref, out_hbm,
                     wstage, w_s, xbuf, obuf, wsem, xsem, osem, zsem):
    e = pl.program_id(0)
    nb = nblk[e]
    b0 = bstart[e]

    def w_copy(step):
        return pltpu.make_async_copy(w_hbm.at[layer, step], wstage, wsem)

    @pl.when(e == 0)
    def _():
        w_copy(0).start()
        obuf[0] = jnp.zeros(obuf.shape[1:], obuf.dtype)
        _zero_unowned(obuf.at[0, pl.ds(0, RB)],
                      lambda blk: out_hbm.at[pl.ds(_row_of(blk), RB)], used, zsem)

    w_copy(e).wait()
    w_s[...] = wstage[...].astype(BF16)

    @pl.when(e + 1 < pl.num_programs(0))
    def _():
        w_copy(e + 1).start()

    def compute(x_ref, o_ref):
        acc = b_ref[0, 0] + _dot(x_ref[:, 0:TN_E], w_s[0:TN_E, :])
        for jj in range(1, NJ):
            acc = acc + _dot(x_ref[:, jj * TN_E:(jj + 1) * TN_E], w_s[jj * TN_E:(jj + 1) * TN_E, :])
        o_ref[...] = acc

    def x_src(r, rows, dst):
        return [(x_hbm.at[jj, pl.ds(r, rows)], dst.at[:, pl.ds(jj * TN_E, TN_E)])
                for jj in range(NJ)]

    def o_dst(r, rows):
        return out_hbm.at[pl.ds(r, rows)]

    n_big = nb // SB
    _stream_rows(n_big, b0, SROWS, x_src, o_dst, xbuf, obuf, xsem, osem, compute)
    _stream_rows(nb - n_big * SB, b0 + n_big * SB, RB, x_src, o_dst, xbuf, obuf, xsem, osem,
                 compute)


def _gmm_down(tables, act, w_down, b_down, layer):
    grid_spec = pltpu.PrefetchScalarGridSpec(
        num_scalar_prefetch=3,
        grid=(NE,),
        in_specs=[
            pl.BlockSpec(memory_space=pl.ANY),
            pl.BlockSpec(memory_space=pl.ANY),
            pl.BlockSpec((1, 1, 1, D), lambda e, *_: (layer, e, 0, 0)),
        ],
        out_specs=pl.BlockSpec(memory_space=pl.ANY),
        scratch_shapes=[pltpu.VMEM((DFF, D), F32), pltpu.VMEM((DFF, D), BF16),
                        pltpu.VMEM((2, SROWS, DFF), BF16), pltpu.VMEM((2, SROWS, D), F32),
                        pltpu.SemaphoreType.DMA(()), pltpu.SemaphoreType.DMA((2,)),
                        pltpu.SemaphoreType.DMA((2,)), pltpu.SemaphoreType.DMA(())],
    )
    return pl.pallas_call(
        functools.partial(_gmm_down_kernel, layer),
        grid_spec=grid_spec,
        out_shape=_sds((ROWS, D), F32),
        compiler_params=_cparams(1, VMEM_GMM),
        name="gmm_down",
    )(*tables, act, w_down, b_down)


def _combine_kernel(dest_ref, rows_hbm, tw_ref, x_ref, gt_ref, o_ref, buf, sem):
    def body(i, carry):
        for k in range(TOPK):
            d = dest_ref[0, 0, i * TOPK + k]
            pltpu.make_async_copy(rows_hbm.at[pl.ds(d, 1)], buf.at[k, pl.ds(i, 1)], sem).start(
                priority=k % 2)
        return carry

    lax.fori_loop(0, TOK_TILE, body, 0)
    for k in range(TOPK):
        pltpu.make_async_copy(rows_hbm.at[pl.ds(0, TOK_TILE)], buf.at[k], sem).wait()
    tw = tw_ref[...]
    y = tw[:, 0:1] * buf[0]
    for k in range(1, TOPK):
        y = y + tw[:, k:k + 1] * buf[k]
    o_ref[...] = x_ref[...] + gt_ref[0, 0] * y


def _combine(dest3, out_rows, tw, x1, mod, tile0):
    m_rows = x1.shape[0]
    tiles = m_rows // TOK_TILE
    return pl.pallas_call(
        _combine_kernel,
        grid=(tiles,),
        in_specs=[
            pl.BlockSpec((1, 1, TOK_TILE * TOPK), lambda m: (tile0 + m, 0, 0),
                         memory_space=pltpu.SMEM),
            pl.BlockSpec(memory_space=pl.ANY),
            pl.BlockSpec((TOK_TILE, TOPK), lambda m: (tile0 + m, 0)),
            pl.BlockSpec((TOK_TILE, D), lambda m: (m, 0)),
            _mod_spec(mod, 5, tiles),
        ],
        out_specs=pl.BlockSpec((TOK_TILE, D), lambda m: (m, 0)),
        out_shape=_sds((m_rows, D), F32),
        scratch_shapes=[pltpu.VMEM((TOPK, TOK_TILE, D), F32), pltpu.SemaphoreType.DMA(())],
        compiler_params=_cparams(1),
        name="combine",
    )(dest3, out_rows, tw, x1, mod)


def _routing_tables(te, rank, cnt):
    counts = cnt[0, :NE].astype(I32)
    nblk = (counts + RB - 1) // RB
    bstart = jnp.cumsum(nblk) - nblk
    used = jnp.sum(nblk)
    te_d = te.reshape(N_TT, TOK_TILE * TOPK)
    start_d = jnp.zeros_like(te_d)
    for e in range(NE):
        start_d = jnp.where(te_d == e, bstart[e] * RB, start_d)
    dest3 = (start_d + rank.reshape(N_TT, TOK_TILE * TOPK)).reshape(N_TT, 1, TOK_TILE * TOPK)
    empty = nblk == 0
    n_empty = jnp.sum(empty.astype(I32))
    last_blk = jnp.where(empty, used + jnp.cumsum(empty.astype(I32)) - 1, bstart + nblk - 1)
    extra0 = used + n_empty
    last_rows = jnp.concatenate([last_blk * RB, jnp.stack([extra0 * RB, NB_MAX - extra0])])
    tables = (bstart.astype(I32), nblk.astype(I32), used.reshape(1).astype(I32))
    return dest3, tables, last_rows.astype(I32)


def _moe(layer, h2_all, te_all, w_gu, b_gu, w_down, b_down):
    rank, cnt = _rank(te_all)
    dest3, steps, last_rows = _routing_tables(te_all, rank, cnt)
    xs = _scatter_rows(dest3, last_rows, h2_all)
    act = _gmm_up(steps, xs, w_gu, b_gu.reshape(DEPTH, NE, 1, 2 * DFF), layer)
    out_rows = _gmm_down(steps, act, w_down, b_down.reshape(DEPTH, NE, 1, D), layer)
    return dest3, out_rows


def _layer(l, xp, xs, st, w, mod_p, mod_s):
    cast = lambda a: a.astype(BF16)
    w_in = w["w_in"][l]
    w_main = cast(jnp.concatenate(
        [w_in[:, 1536:4608], w_in[:, 6672:12816], w_in[:, 4624:5648], w_in[:, 5648:6672],
         w_in[:, 0:1024], w_in[:, 1024:1280], w_in[:, 1280:1536]], axis=1))
    w_ab = cast(jnp.pad(w_in[:, 4608:4624], ((0, 0), (0, 112))))
    wb = cast(w["w_branch"][l])
    w_out = cast(w["w_out"][l])
    wr = cast(jnp.pad(w["w_router"][l], ((0, 0), (0, 128 - NE))))
    br = jnp.pad(w["b_router"][l], (0, 128 - NE)).reshape(1, 128)
    g1 = w["norm1_g"][l].reshape(1, D)
    g2 = w["norm2_g"][l].reshape(1, D)
    qg = w["q_norm_g"][l].reshape(1, HD)
    kg = w["k_norm_g"][l].reshape(1, HD)
    sinks = w["sinks"][l]
    dn_cw = w["dn_conv_w"][l]
    alog_row = jnp.pad(w["dn_a_log"][l], (0, 128 - DNH)).reshape(1, 128)
    dtb_row = jnp.pad(w["dn_dt_bias"][l], (0, 128 - DNH)).reshape(1, 128)
    dn_ng = w["dn_norm_g"][l].reshape(1, DK)
    l_cw = w["lru_conv_w"][l]
    l_cb = w["lru_conv_b"][l].reshape(1, LW)
    l_wr = cast(w["lru_w_r"][l])
    l_wi = cast(w["lru_w_i"][l])
    l_br = w["lru_b_r"][l].reshape(1, LW)
    l_bi = w["lru_b_i"][l].reshape(1, LW)
    l_lam = w["lru_lambda"][l].reshape(1, LW)

    h1p = _adaln(xp, g1, mod_p, 0, 1, 256)
    pp = _matmul(h1p, w_main, F32, 1024, 1280, "proj")
    pab = _matmul(h1p, w_ab, F32, 2048, 128, "proj_ab")
    oa_p, k_p, v_p = _attn_prompt(pp, qg, kg, sinks)
    ob_p, d_p, dc_p = _delta_prompt(pp, pab, dn_cw, alog_row, dtb_row, dn_ng)
    ob_p = ob_p.reshape(MP, 1024)
    oc_p, h_p, hc_p = _lru_prompt(pp, l_cw, l_cb, l_wr, l_br, l_wi, l_bi, l_lam)
    mg_p = _merge(oa_p, ob_p, oc_p, wb, pp, 1024)
    x1p = _outproj(mg_p, w_out, xp, mod_p, 1024)

    ck, cv, s0, dconv0, h0, lconv0 = st
    h1s = _adaln(xs, g1, mod_s, 0, 1, MS)
    ps = _matmul(h1s, w_main, F32, MS, 512, "proj")
    psab = _matmul(h1s, w_ab, F32, MS, 128, "proj_ab")
    q_r = ps[:, P_Q:P_K].reshape(DEC_BATCH, DEC_SEQ, NKV, GRP, HD).transpose(0, 2, 3, 1, 4)
    q_r = q_r.reshape(DEC_BATCH, NKV, GRP * DEC_SEQ, HD)
    pad_t = ((0, 0), (0, 0), (0, 8 - DEC_SEQ), (0, 0))
    kn_r = jnp.pad(ps[:, P_K:P_V].reshape(DEC_BATCH, DEC_SEQ, NKV, HD).transpose(0, 2, 1, 3), pad_t)
    vn_raw = ps[:, P_V:P_W].reshape(DEC_BATCH, DEC_SEQ, NKV, HD)
    vn_r = jnp.pad(vn_raw.transpose(0, 2, 1, 3), pad_t)
    oa_r, kno = _attn_sample(q_r, kn_r, vn_r, ck.transpose(0, 2, 1, 3), cv.transpose(0, 2, 1, 3),
                             qg, kg, sinks)
    oa_s = oa_r.reshape(DEC_BATCH, NKV, GRP, DEC_SEQ, HD).transpose(0, 3, 1, 2, 4)
    oa_s = oa_s.reshape(MS, 1024).astype(BF16)
    k_s = jnp.concatenate([ck[:, DEC_SEQ:], kno[:, :, :DEC_SEQ].transpose(0, 2, 1, 3)], axis=1)
    v_s = jnp.concatenate([cv[:, DEC_SEQ:], vn_raw], axis=1)

    x_dq = ps[:, P_DQKV:P_DQKV + CONV_CH].reshape(DEC_BATCH, DEC_SEQ, CONV_CH)
    xp8 = jnp.concatenate([jnp.zeros((DEC_BATCH, 1, CONV_CH), F32), dconv0, x_dq], axis=1)
    pad8 = ((0, 0), (0, 8 - DEC_SEQ), (0, 0))
    dz8 = jnp.pad(ps[:, P_DZ:P_LX].reshape(DEC_BATCH, DEC_SEQ, 1024), pad8)
    dab8 = jnp.pad(psab.reshape(DEC_BATCH, DEC_SEQ, 128), pad8)
    ob_r, d_s = _delta_sample(xp8, dz8, dab8, s0, dn_cw, alog_row, dtb_row, dn_ng)
    ob_s = ob_r[:, :DEC_SEQ].reshape(MS, 1024).astype(BF16)
    dc_s = x_dq[:, DEC_SEQ - (CONV_W - 1):]

    x_lx = ps[:, P_LX:P_Q].reshape(DEC_BATCH, DEC_SEQ, LW)
    oc_tm = _lru_sample(x_lx.transpose(1, 0, 2), lconv0.transpose(1, 0, 2), h0, l_cw, l_cb,
                        l_wr, l_br, l_wi, l_bi, l_lam)
    oc_s = oc_tm.transpose(1, 0, 2).reshape(MS, LW).astype(BF16)
    h_s = oc_tm[DEC_SEQ - 1]
    hc_s = x_lx[:, DEC_SEQ - (CONV_W - 1):]

    mg_s = _merge(oa_s, ob_s, oc_s, wb, ps, MS)
    x1s = _outproj(mg_s, w_out, xs, mod_s, MS)

    h2_all, te_all, tw_all = _router(x1p, x1s, g2, mod_p, mod_s, wr, br)
    dest3, out_rows = _moe(l, h2_all, te_all, w["w_gate_up"], w["b_gate_up"],
                           w["w_down"], w["b_down"])
    np_t = MP // TOK_TILE
    xp_new = _combine(dest3, out_rows, tw_all, x1p, mod_p, 0)
    xs_new = _combine(dest3, out_rows, tw_all, x1s, mod_s, np_t)

    st_p = (k_p.reshape(BATCH, WINDOW, NKV, HD), v_p.reshape(BATCH, WINDOW, NKV, HD), d_p, dc_p,
            h_p.reshape(BATCH, LW), hc_p)
    st_s = (k_s, v_s, d_s, dc_s, h_s, hc_s)
    return xp_new, xs_new, st_p, st_s


def kernel(x_prompt, x_sample, cache_k, cache_v, state_delta, state_delta_conv, state_lru, state_lru_conv, c_prompt, c_sample, w_ada, b_ada, norm1_g, norm2_g, w_in, q_norm_g, k_norm_g, sinks, dn_conv_w, dn_a_log, dn_dt_bias, dn_norm_g, lru_conv_w, lru_conv_b, lru_w_r, lru_b_r, lru_w_i, lru_b_i, lru_lambda, w_branch, w_out, w_router, b_router, w_gate_up, b_gate_up, w_down, b_down):
    w = dict(w_in=w_in, w_branch=w_branch, w_out=w_out, w_router=w_router, b_router=b_router,
             norm1_g=norm1_g, norm2_g=norm2_g, q_norm_g=q_norm_g, k_norm_g=k_norm_g, sinks=sinks,
             dn_conv_w=dn_conv_w, dn_a_log=dn_a_log, dn_dt_bias=dn_dt_bias, dn_norm_g=dn_norm_g,
             lru_conv_w=lru_conv_w, lru_conv_b=lru_conv_b, lru_w_r=lru_w_r, lru_b_r=lru_b_r,
             lru_w_i=lru_w_i, lru_b_i=lru_b_i, lru_lambda=lru_lambda, w_gate_up=w_gate_up,
             b_gate_up=b_gate_up, w_down=w_down, b_down=b_down)
    n_c = BATCH + DEC_BATCH
    c_all = jnp.concatenate([c_prompt, c_sample, jnp.zeros((40 - n_c, D), F32)], axis=0)
    mod_all = _ada_mod(c_all, w_ada, b_ada)

    xp = x_prompt.reshape(MP, D)
    xs = x_sample.reshape(MS, D)
    new_p, new_s = [], []
    for l in range(DEPTH):
        mod_p = mod_all[l, :BATCH].reshape(BATCH, 6, 1, D).transpose(1, 0, 2, 3)
        mod_s = jnp.repeat(mod_all[l, BATCH:n_c].reshape(DEC_BATCH, 6, D), DEC_SEQ, axis=0)
        mod_s = mod_s.transpose(1, 0, 2).reshape(6, 1, MS, D)
        st = (cache_k[l], cache_v[l], state_delta[l], state_delta_conv[l], state_lru[l],
              state_lru_conv[l])
        xp, xs, st_p, st_s = _layer(l, xp, xs, st, w, mod_p, mod_s)
        new_p.append(st_p)
        new_s.append(st_s)
    k_p, v_p, d_p, dc_p, h_p, hc_p = (jnp.stack(z) for z in zip(*new_p))
    k_s, v_s, d_s, dc_s, h_s, hc_s = (jnp.stack(z) for z in zip(*new_s))
    return (xp.reshape(BATCH, SEQ, D), xs.reshape(DEC_BATCH, DEC_SEQ, D),
            k_p, v_p, d_p, dc_p, h_p, hc_p, k_s, v_s, d_s, dc_s, h_s, hc_s)
```

```python
import functools

import jax
import jax.numpy as jnp
from jax import lax
from jax.experimental import pallas as pl
from jax.experimental.pallas import tpu as pltpu

F32 = jnp.float32
BF16 = jnp.bfloat16
I32 = jnp.int32

D = 2048
BATCH = 2
SEQ = 4096
DEC_BATCH = 32
DEC_SEQ = 4
DEPTH = 2
MP = BATCH * SEQ
MS = DEC_BATCH * DEC_SEQ
NTOK = MP + MS

WINDOW = 128
HD = 64
NH = 16
NKV = 4
GRP = NH // NKV
ATT_SCALE = HD ** -0.5

DK = 128
DNH = 8
CHUNK = 64
CONV_CH = 3 * DNH * DK
CONV_W = 4

LW = 1024
LBLK = 128
LNB = LW // LBLK
LRU_C = 8.0

NE = 32
TOPK = 4
DFF = 2048
SW_ALPHA = 1.702
SW_LIMIT = 7.0
EPS = 1e-6

P_DQKV = 0
P_G = 3072
P_DZ = 9216
P_LX = 10240
P_Q = 11264
P_K = 12288
P_V = 12544
P_W = 12800

RB = 256
NB_MAX = -(-(NTOK * TOPK) // RB) + NE
ROWS = NB_MAX * RB
TN_E = 1024
NJ = DFF // TN_E
TOK_TILE = 128
N_TT = NTOK // TOK_TILE

VMEM_BIG = 56 * 1024 * 1024
VMEM_GMM = 60 * 1024 * 1024


def _sds(shape, dtype):
    return jax.ShapeDtypeStruct(shape, dtype)


def _cparams(n_axes, vmem=None):
    return pltpu.CompilerParams(dimension_semantics=("arbitrary",) * n_axes, vmem_limit_bytes=vmem)


def _dot(a, b):
    return jnp.dot(a, b, preferred_element_type=F32)


def _dot_nt(a, b):
    return lax.dot_general(a, b, (((1,), (1,)), ((), ())), preferred_element_type=F32)


def _dot_tn(a, b):
    return lax.dot_general(a, b, (((0,), (0,)), ((), ())), preferred_element_type=F32)


def _rms(x, g):
    return x * lax.rsqrt(jnp.mean(x * x, axis=-1, keepdims=True) + EPS) * g


def _sigmoid(x):
    return 1.0 / (1.0 + jnp.exp(-x))


def _silu(x):
    return x * _sigmoid(x)


def _softplus(x):
    return jnp.maximum(x, 0.0) + jnp.log1p(jnp.exp(-jnp.abs(x)))


def _ada_kernel(c_ref, w_ref, b_ref, o_ref):
    a = _silu(c_ref[...]).astype(BF16)
    o_ref[0] = _dot(a, w_ref[0].astype(BF16)) + b_ref[0]


def _ada_mod(c_all, w_ada, b_ada):
    rows = c_all.shape[0]
    tn = 1024
    return pl.pallas_call(
        _ada_kernel,
        grid=(DEPTH, 6 * D // tn),
        in_specs=[
            pl.BlockSpec((rows, D), lambda l, n: (0, 0)),
            pl.BlockSpec((1, D, tn), lambda l, n: (l, 0, n)),
            pl.BlockSpec((1, 1, tn), lambda l, n: (l, 0, n)),
        ],
        out_specs=pl.BlockSpec((1, rows, tn), lambda l, n: (l, 0, n)),
        out_shape=_sds((DEPTH, rows, 6 * D), F32),
        compiler_params=_cparams(2, VMEM_BIG),
        name="ada_mod",
    )(c_all, w_ada, b_ada.reshape(DEPTH, 1, 6 * D))


def _adaln_kernel(x_ref, g_ref, sh_ref, sc_ref, o_ref):
    y = _rms(x_ref[...], g_ref[...])
    o_ref[...] = (y * (1.0 + sc_ref[0, 0]) + sh_ref[0, 0]).astype(o_ref.dtype)


def _mod_spec(mod, j, tiles, tn=D, with_n=False):
    g, rb = mod.shape[1], mod.shape[2]
    tpg = tiles // g
    if with_n:
        return pl.BlockSpec((1, 1, rb, tn), lambda n, m: (j, m // tpg, 0, n))
    return pl.BlockSpec((1, 1, rb, tn), lambda m: (j, m // tpg, 0, 0))


def _adaln(x, g, mod, j_sh, j_sc, tm):
    m_rows = x.shape[0]
    tiles = m_rows // tm
    return pl.pallas_call(
        _adaln_kernel,
        grid=(tiles,),
        in_specs=[
            pl.BlockSpec((tm, D), lambda m: (m, 0)),
            pl.BlockSpec((1, D), lambda m: (0, 0)),
            _mod_spec(mod, j_sh, tiles),
            _mod_spec(mod, j_sc, tiles),
        ],
        out_specs=pl.BlockSpec((tm, D), lambda m: (m, 0)),
        out_shape=_sds((m_rows, D), BF16),
        compiler_params=_cparams(1),
        name="adaln1",
    )(x, g, mod, mod)


def _mm_kernel(x_ref, w_ref, o_ref):
    o_ref[...] = _dot(x_ref[...], w_ref[...]).astype(o_ref.dtype)


def _matmul(x, w, out_dtype, tm, tn, name):
    m_rows, k = x.shape
    n_cols = w.shape[1]
    return pl.pallas_call(
        _mm_kernel,
        grid=(n_cols // tn, m_rows // tm),
        in_specs=[
            pl.BlockSpec((tm, k), lambda n, m: (m, 0)),
            pl.BlockSpec((k, tn), lambda n, m: (0, n)),
        ],
        out_specs=pl.BlockSpec((tm, tn), lambda n, m: (m, n)),
        out_shape=_sds((m_rows, n_cols), out_dtype),
        compiler_params=_cparams(2, VMEM_BIG),
        name=name,
    )(x, w)


def _attn_prompt_kernel(sinks_ref, q_ref, kc_ref, vc_ref, kp_ref, vp_ref, qg_ref, kg_ref,
                        o_ref, ko_ref, vo_ref):
    nblk = SEQ // WINDOW
    first = (pl.program_id(0) % nblk) == 0
    row = lax.broadcasted_iota(I32, (WINDOW, 2 * WINDOW), 0)
    col = lax.broadcasted_iota(I32, (WINDOW, 2 * WINDOW), 1)
    lo = jnp.where(first, WINDOW, 0)
    mask = (col > row) & (col <= row + WINDOW) & (col >= lo)
    qg = qg_ref[...]
    kg = kg_ref[...]
    outs = []
    k_out = []
    for kh in range(NKV):
        ks = slice(kh * HD, (kh + 1) * HD)
        kc_n = _rms(kc_ref[:, ks], kg)
        kp_n = _rms(kp_ref[:, ks], kg)
        k_out.append(kc_n)
        kk = jnp.concatenate([kp_n, kc_n], axis=0).astype(BF16)
        vv = jnp.concatenate([vp_ref[:, ks], vc_ref[:, ks]], axis=0).astype(BF16)
        for g in range(GRP):
            h = kh * GRP + g
            qh = _rms(q_ref[:, h * HD:(h + 1) * HD], qg).astype(BF16)
            s = _dot_nt(qh, kk) * ATT_SCALE
            s = jnp.where(mask, s, -jnp.inf)
            sink = sinks_ref[h]
            m = jnp.maximum(jnp.max(s, axis=-1, keepdims=True), sink)
            p = jnp.exp(s - m)
            den = jnp.sum(p, axis=-1, keepdims=True) + jnp.exp(sink - m)
            outs.append(_dot(p.astype(BF16), vv) / den)
    o_ref[...] = jnp.concatenate(outs, axis=-1).astype(o_ref.dtype)
    ko_ref[0] = jnp.concatenate(k_out, axis=-1)
    vo_ref[0] = vc_ref[...]


def _attn_prompt(p_act, qg, kg, sinks):
    nblk = SEQ // WINDOW
    cq, ck, cv = P_Q // 1024, P_K // 256, P_V // 256
    return pl.pallas_call(
        _attn_prompt_kernel,
        grid=(MP // WINDOW,),
        in_specs=[
            pl.BlockSpec(memory_space=pltpu.SMEM),
            pl.BlockSpec((WINDOW, 1024), lambda g: (g, cq)),
            pl.BlockSpec((WINDOW, 256), lambda g: (g, ck)),
            pl.BlockSpec((WINDOW, 256), lambda g: (g, cv)),
            pl.BlockSpec((WINDOW, 256), lambda g: (jnp.maximum(g - 1, 0), ck)),
            pl.BlockSpec((WINDOW, 256), lambda g: (jnp.maximum(g - 1, 0), cv)),
            pl.BlockSpec((1, HD), lambda g: (0, 0)),
            pl.BlockSpec((1, HD), lambda g: (0, 0)),
        ],
        out_specs=[
            pl.BlockSpec((WINDOW, 1024), lambda g: (g, 0)),
            pl.BlockSpec((1, WINDOW, 256), lambda g: (g // nblk, 0, 0)),
            pl.BlockSpec((1, WINDOW, 256), lambda g: (g // nblk, 0, 0)),
        ],
        out_shape=[_sds((MP, 1024), BF16), _sds((BATCH, WINDOW, 256), F32),
                   _sds((BATCH, WINDOW, 256), F32)],
        compiler_params=_cparams(1),
        name="attn_prompt",
    )(sinks, p_act, p_act, p_act, p_act, p_act, qg, kg)


def _attn_sample_kernel(sinks_ref, q_ref, kn_ref, vn_ref, ck_ref, cv_ref, qg_ref, kg_ref,
                        o_ref, kno_ref):
    rows = GRP * DEC_SEQ
    t = lax.broadcasted_iota(I32, (rows, 1), 0) % DEC_SEQ
    g_of_row = lax.broadcasted_iota(I32, (rows, 1), 0) // DEC_SEQ
    col_c = lax.broadcasted_iota(I32, (rows, WINDOW), 1)
    col_n = lax.broadcasted_iota(I32, (rows, 8), 1)
    qg = qg_ref[...]
    kg = kg_ref[...]
    for kh in range(NKV):
        q16 = _rms(q_ref[0, kh], qg).astype(BF16)
        kn = _rms(kn_ref[0, kh], kg)
        kno_ref[0, kh] = kn
        s_c = _dot_nt(q16, ck_ref[0, kh].astype(BF16)) * ATT_SCALE
        s_n = _dot_nt(q16, kn.astype(BF16)) * ATT_SCALE
        s_c = jnp.where(col_c > t, s_c, -jnp.inf)
        s_n = jnp.where(col_n <= t, s_n, -jnp.inf)
        sink = jnp.zeros((rows, 1), F32)
        for g in range(GRP):
            sink = jnp.where(g_of_row == g, sinks_ref[kh * GRP + g], sink)
        m = jnp.maximum(jnp.maximum(jnp.max(s_c, axis=-1, keepdims=True),
                                    jnp.max(s_n, axis=-1, keepdims=True)), sink)
        p_c = jnp.exp(s_c - m)
        p_n = jnp.exp(s_n - m)
        den = (jnp.sum(p_c, axis=-1, keepdims=True) + jnp.sum(p_n, axis=-1, keepdims=True)
               + jnp.exp(sink - m))
        o = _dot(p_c.astype(BF16), cv_ref[0, kh].astype(BF16)) + _dot(
            p_n.astype(BF16), vn_ref[0, kh].astype(BF16))
        o_ref[0, kh] = o / den


def _attn_sample(q_r, kn_r, vn_r, ck_r, cv_r, qg, kg, sinks):
    rows = GRP * DEC_SEQ
    return pl.pallas_call(
        _attn_sample_kernel,
        grid=(DEC_BATCH,),
        in_specs=[
            pl.BlockSpec(memory_space=pltpu.SMEM),
            pl.BlockSpec((1, NKV, rows, HD), lambda b: (b, 0, 0, 0)),
            pl.BlockSpec((1, NKV, 8, HD), lambda b: (b, 0, 0, 0)),
            pl.BlockSpec((1, NKV, 8, HD), lambda b: (b, 0, 0, 0)),
            pl.BlockSpec((1, NKV, WINDOW, HD), lambda b: (b, 0, 0, 0)),
            pl.BlockSpec((1, NKV, WINDOW, HD), lambda b: (b, 0, 0, 0)),
            pl.BlockSpec((1, HD), lambda b: (0, 0)),
            pl.BlockSpec((1, HD), lambda b: (0, 0)),
        ],
        out_specs=[
            pl.BlockSpec((1, NKV, rows, HD), lambda b: (b, 0, 0, 0)),
            pl.BlockSpec((1, NKV, 8, HD), lambda b: (b, 0, 0, 0)),
        ],
        out_shape=[_sds((DEC_BATCH, NKV, rows, HD), F32), _sds((DEC_BATCH, NKV, 8, HD), F32)],
        compiler_params=_cparams(1),
        name="attn_sample",
    )(sinks, q_r, kn_r, vn_r, ck_r, cv_r, qg, kg)


def _cumsum_rows(x, c):
    row = lax.broadcasted_iota(I32, x.shape, 0)
    s = 1
    while s < c:
        x = x + jnp.where(row >= s, pltpu.roll(x, s, 0), 0.0)
        s *= 2
    return x


def _delta_chunk(c, nstack, conv_slice, g_full, beta_full, z_slice, norm_g, s_get, s_put, o_put):
    n = nstack * c
    lg = c.bit_length() - 1
    r = lax.broadcasted_iota(I32, (n, n), 0)
    cc = lax.broadcasted_iota(I32, (n, n), 1)
    same = (r >> lg) == (cc >> lg)
    incl = same & (r >= cc)
    strict = same & (r > cc)
    eye_b = r == cc
    eye = eye_b.astype(F32)
    gc = _cumsum_rows(g_full, c)
    for st in range(DNH // nstack):
        heads = range(st * nstack, (st + 1) * nstack)
        qs, ks, vs, gcs, bs = [], [], [], [], []
        for h in heads:
            qh = conv_slice(slice(h * DK, (h + 1) * DK))
            kh = conv_slice(slice(DNH * DK + h * DK, DNH * DK + (h + 1) * DK))
            vs.append(conv_slice(slice(2 * DNH * DK + h * DK, 2 * DNH * DK + (h + 1) * DK)))
            qs.append(qh * lax.rsqrt(jnp.sum(qh * qh, axis=-1, keepdims=True) + EPS) * (DK ** -0.5))
            ks.append(kh * lax.rsqrt(jnp.sum(kh * kh, axis=-1, keepdims=True) + EPS))
            gcs.append(gc[:, h:h + 1])
            bs.append(beta_full[:, DNH + h:DNH + h + 1])
        q = jnp.concatenate(qs, axis=0)
        k = jnp.concatenate(ks, axis=0)
        v = jnp.concatenate(vs, axis=0)
        gcol = jnp.concatenate(gcs, axis=0)
        bcol = jnp.concatenate(bs, axis=0)
        grow = jnp.sum(jnp.where(eye_b, gcol, 0.0), axis=0, keepdims=True)
        decay = jnp.exp(jnp.where(incl, gcol - grow, -jnp.inf))
        egc = jnp.exp(gcol)
        kb = k * bcol
        k16 = k.astype(BF16)
        a_mat = jnp.where(strict, _dot_nt(kb.astype(BF16), k16) * decay, 0.0)
        blk = 1
        t_inv = eye
        while blk < c:
            sh = blk.bit_length()
            pair = ((r >> sh) == (cc >> sh)) & ((r & blk) != 0) & ((cc & blk) == 0)
            off = jnp.where(pair, a_mat, 0.0)
            if blk == 1:
                t_inv = t_inv - off
            else:
                t16 = t_inv.astype(BF16)
                t_inv = t_inv - _dot(t16, _dot(off.astype(BF16), t16).astype(BF16))
            blk *= 2
        rhs = jnp.concatenate([v * bcol, kb * egc], axis=-1).astype(BF16)
        sol = _dot(t_inv.astype(BF16), rhs)
        qk16 = (_dot_nt(q.astype(BF16), k16) * decay).astype(BF16)
        q_dec = q * egc
        us, s_olds, g_lasts = [], [], []
        for i, h in enumerate(heads):
            hs = slice(i * c, (i + 1) * c)
            s_old = s_get(h)
            us.append(sol[hs, :DK] - _dot(sol[hs, DK:].astype(BF16), s_old.astype(BF16)))
            s_olds.append(s_old)
            g_lasts.append(gcol[(i + 1) * c - 1:(i + 1) * c, :])
        u16 = jnp.concatenate(us, axis=0).astype(BF16)
        o_in = _dot(qk16, u16)
        for i, h in enumerate(heads):
            hs = slice(i * c, (i + 1) * c)
            s16 = s_olds[i].astype(BF16)
            o = _dot(q_dec[hs].astype(BF16), s16) + o_in[hs]
            k_dec = k[hs] * jnp.exp(g_lasts[i] - gcol[hs])
            s_put(h, s_olds[i] * jnp.exp(g_lasts[i]) + _dot_tn(k_dec.astype(BF16), u16[hs]))
            zz = z_slice(slice(h * DK, (h + 1) * DK))
            o_put(h, _rms(o, norm_g) * _silu(zz))


def _delta_prompt_kernel(x_ref, prev_ref, dz_ref, dab_ref, cw_ref, alog_ref, dtb_ref, ng_ref,
                         o_ref, s_ref, dc_ref):
    c = pl.program_id(0)

    @pl.when(c == 0)
    def _():
        s_ref[...] = jnp.zeros_like(s_ref)

    row8 = lax.broadcasted_iota(I32, (8, DK), 0)
    keep_prev = c > 0
    for b in range(BATCH):
        def conv_slice(cs, b=b):
            x = x_ref[b, :, cs]
            prev = jnp.where(keep_prev, prev_ref[b, :, cs], 0.0)
            y = x * cw_ref[CONV_W - 1:CONV_W, cs]
            for s in range(1, CONV_W):
                xr = pltpu.roll(x, s, 0)
                top = jnp.where(row8 < s, pltpu.roll(prev, s, 0), xr[0:8])
                sh = jnp.concatenate([top, xr[8:]], axis=0)
                y = y + sh * cw_ref[CONV_W - 1 - s:CONV_W - s, cs]
            return _silu(y)

        dab = dab_ref[b]
        g_full = -jnp.exp(alog_ref[...]) * _softplus(dab + dtb_ref[...])
        beta_full = _sigmoid(dab)

        def s_get(h, b=b):
            return s_ref[b, h]

        def s_put(h, v, b=b):
            s_ref[b, h] = v

        def o_put(h, v, b=b):
            o_ref[b, :, h * DK:(h + 1) * DK] = v.astype(o_ref.dtype)

        _delta_chunk(CHUNK, 4, conv_slice, g_full, beta_full, lambda cs, b=b: dz_ref[b, :, cs],
                     ng_ref[...], s_get, s_put, o_put)

    @pl.when(c == pl.num_programs(0) - 1)
    def _():
        dc_ref[...] = x_ref[:, CHUNK - (CONV_W - 1):CHUNK, :]


def _delta_prompt(p_act, p_ab, conv_w, alog_row, dtb_row, norm_g):
    nck = SEQ // CHUNK
    p3 = p_act.reshape(BATCH, SEQ, p_act.shape[1])
    ab3 = p_ab.reshape(BATCH, SEQ, 128)
    return pl.pallas_call(
        _delta_prompt_kernel,
        grid=(nck,),
        in_specs=[
            pl.BlockSpec((BATCH, CHUNK, CONV_CH), lambda c: (0, c, P_DQKV // CONV_CH)),
            pl.BlockSpec((BATCH, 8, CONV_CH),
                         lambda c: (0, jnp.maximum(c * (CHUNK // 8) - 1, 0), P_DQKV // CONV_CH)),
            pl.BlockSpec((BATCH, CHUNK, 1024), lambda c: (0, c, P_DZ // 1024)),
            pl.BlockSpec((BATCH, CHUNK, 128), lambda c: (0, c, 0)),
            pl.BlockSpec((CONV_W, CONV_CH), lambda c: (0, 0)),
            pl.BlockSpec((1, 128), lambda c: (0, 0)),
            pl.BlockSpec((1, 128), lambda c: (0, 0)),
            pl.BlockSpec((1, DK), lambda c: (0, 0)),
        ],
        out_specs=[
            pl.BlockSpec((BATCH, CHUNK, 1024), lambda c: (0, c, 0)),
            pl.BlockSpec((BATCH, DNH, DK, DK), lambda c: (0, 0, 0, 0)),
            pl.BlockSpec((BATCH, CONV_W - 1, CONV_CH), lambda c: (0, 0, 0)),
        ],
        out_shape=[_sds((BATCH, SEQ, 1024), BF16), _sds((BATCH, DNH, DK, DK), F32),
                   _sds((BATCH, CONV_W - 1, CONV_CH), F32)],
        compiler_params=_cparams(1),
        name="delta_prompt",
    )(p3, p3, p3, ab3, conv_w, alog_row, dtb_row, norm_g)


def _delta_sample_kernel(xp_ref, dz_ref, dab_ref, s0_ref, cw_ref, alog_ref, dtb_ref, ng_ref,
                         o_ref, s_ref):
    row = lax.broadcasted_iota(I32, (8, DK), 0)
    live = row < DEC_SEQ

    def conv_slice(cs):
        xp = xp_ref[0, :, cs]
        y = jnp.zeros((8, DK), F32)
        for j in range(CONV_W):
            y = y + pltpu.roll(xp, 8 - 1 - j, 0) * cw_ref[j:j + 1, cs]
        return jnp.where(live, _silu(y), 0.0)

    dab = dab_ref[0]
    g_full = jnp.where(live, -jnp.exp(alog_ref[...]) * _softplus(dab + dtb_ref[...]), 0.0)
    beta_full = jnp.where(live, _sigmoid(dab), 0.0)

    def s_put(h, v):
        s_ref[0, h] = v

    def o_put(h, v):
        o_ref[0, :, h * DK:(h + 1) * DK] = v

    _delta_chunk(8, DNH, conv_slice, g_full, beta_full, lambda cs: dz_ref[0, :, cs], ng_ref[...],
                 lambda h: s0_ref[0, h], s_put, o_put)


def _delta_sample(xp8, dz8, dab8, s0, conv_w, alog_row, dtb_row, norm_g):
    return pl.pallas_call(
        _delta_sample_kernel,
        grid=(DEC_BATCH,),
        in_specs=[
            pl.BlockSpec((1, 8, CONV_CH), lambda b: (b, 0, 0)),
            pl.BlockSpec((1, 8, 1024), lambda b: (b, 0, 0)),
            pl.BlockSpec((1, 8, 128), lambda b: (b, 0, 0)),
            pl.BlockSpec((1, DNH, DK, DK), lambda b: (b, 0, 0, 0)),
            pl.BlockSpec((CONV_W, CONV_CH), lambda b: (0, 0)),
            pl.BlockSpec((1, 128), lambda b: (0, 0)),
            pl.BlockSpec((1, 128), lambda b: (0, 0)),
            pl.BlockSpec((1, DK), lambda b: (0, 0)),
        ],
        out_specs=[
            pl.BlockSpec((1, 8, 1024), lambda b: (b, 0, 0)),
            pl.BlockSpec((1, DNH, DK, DK), lambda b: (b, 0, 0, 0)),
        ],
        out_shape=[_sds((DEC_BATCH, 8, 1024), F32), _sds((DEC_BATCH, DNH, DK, DK), F32)],
        compiler_params=_cparams(1),
        name="delta_sample",
    )(xp8, dz8, dab8, s0, conv_w, alog_row, dtb_row, norm_g)


def _lru_gates(xc, wr, br, wi, bi, sp):
    x16 = xc.astype(BF16)
    r = _sigmoid(_dot(x16, wr) + br)
    i = _sigmoid(_dot(x16, wi) + bi)
    log_a = -LRU_C * r * sp
    a = jnp.exp(log_a)
    th = jnp.tanh(log_a)
    u = jnp.sqrt(-2.0 * th / (1.0 - th)) * (i * xc)
    return a, u


LRU_TT = 256


def _lru_prompt_kernel(x_ref, prev_ref, cw_ref, cb_ref, wr_ref, br_ref, wi_ref, bi_ref, lam_ref,
                       o_ref, h_ref, lc_ref):
    t = pl.program_id(1)
    row8 = lax.broadcasted_iota(I32, (8, LBLK), 0)
    row = lax.broadcasted_iota(I32, (LRU_TT, LBLK), 0)
    keep = t > 0
    for n in range(LNB):
        cs = slice(n * LBLK, (n + 1) * LBLK)
        x = x_ref[:, cs]
        prev = jnp.where(keep, prev_ref[:, cs], 0.0)
        y = x * cw_ref[CONV_W - 1:CONV_W, cs]
        for s in range(1, CONV_W):
            xr = pltpu.roll(x, s, 0)
            top = jnp.where(row8 < s, pltpu.roll(prev, s, 0), xr[0:8])
            y = y + jnp.concatenate([top, xr[8:]], axis=0) * cw_ref[CONV_W - 1 - s:CONV_W - s, cs]
        xc = y + cb_ref[:, cs]
        sp = _softplus(-lam_ref[:, cs])
        a, u = _lru_gates(xc, wr_ref[n], br_ref[:, cs], wi_ref[n], bi_ref[:, cs], sp)
        h0 = jnp.where(keep, h_ref[0, :, cs], 0.0)
        u = u + jnp.where(row == 0, a * h0, 0.0)
        s = 1
        while s < LRU_TT:
            valid = row >= s
            u_s = pltpu.roll(u, s, 0)
            a_s = pltpu.roll(a, s, 0)
            u = jnp.where(valid, a * u_s + u, u)
            a = jnp.where(valid, a * a_s, a)
            s *= 2
        o_ref[:, cs] = u.astype(o_ref.dtype)
        h_ref[0, :, cs] = u[LRU_TT - 1:LRU_TT, :]

    @pl.when(t == pl.num_programs(1) - 1)
    def _():
        lc_ref[0] = x_ref[LRU_TT - (CONV_W - 1):LRU_TT, :]


def _lru_prompt(p_act, cw, cb, wr, br, wi, bi, lam):
    ntt = SEQ // LRU_TT
    cl = P_LX // LW
    vec = pl.BlockSpec((1, LW), lambda b, t: (0, 0))
    mat = pl.BlockSpec((LNB, LBLK, LBLK), lambda b, t: (0, 0, 0))
    return pl.pallas_call(
        _lru_prompt_kernel,
        grid=(BATCH, ntt),
        in_specs=[
            pl.BlockSpec((LRU_TT, LW), lambda b, t: (b * ntt + t, cl)),
            pl.BlockSpec((8, LW),
                         lambda b, t: (jnp.maximum(b * (SEQ // 8) + t * (LRU_TT // 8) - 1, 0), cl)),
            pl.BlockSpec((CONV_W, LW), lambda b, t: (0, 0)),
            vec, mat, vec, mat, vec, vec,
        ],
        out_specs=[
            pl.BlockSpec((LRU_TT, LW), lambda b, t: (b * ntt + t, 0)),
            pl.BlockSpec((1, 1, LW), lambda b, t: (b, 0, 0)),
            pl.BlockSpec((1, CONV_W - 1, LW), lambda b, t: (b, 0, 0)),
        ],
        out_shape=[_sds((MP, LW), BF16), _sds((BATCH, 1, LW), F32),
                   _sds((BATCH, CONV_W - 1, LW), F32)],
        compiler_params=_cparams(2),
        name="lru_prompt",
    )(p_act, p_act, cw, cb, wr, br, wi, bi, lam)


def _lru_sample_kernel(x_ref, buf_ref, h0_ref, cw_ref, cb_ref, wr_ref, br_ref, wi_ref, bi_ref,
                       lam_ref, o_ref):
    for n in range(LNB):
        cs = slice(n * LBLK, (n + 1) * LBLK)
        xp = [buf_ref[j, :, cs] for j in range(CONV_W - 1)] + [x_ref[t, :, cs] for t in range(DEC_SEQ)]
        xcs = []
        for t in range(DEC_SEQ):
            y = xp[t] * cw_ref[0:1, cs]
            for j in range(1, CONV_W):
                y = y + xp[t + j] * cw_ref[j:j + 1, cs]
            xcs.append(y + cb_ref[:, cs])
        xc = jnp.concatenate(xcs, axis=0)
        sp = _softplus(-lam_ref[:, cs])
        a, u = _lru_gates(xc, wr_ref[n], br_ref[:, cs], wi_ref[n], bi_ref[:, cs], sp)
        h = h0_ref[:, cs]
        for t in range(DEC_SEQ):
            rs = slice(t * DEC_BATCH, (t + 1) * DEC_BATCH)
            h = a[rs] * h + u[rs]
            o_ref[t, :, cs] = h


def _lru_sample(x_tm, buf_tm, h0, cw, cb, wr, br, wi, bi, lam):
    return pl.pallas_call(
        _lru_sample_kernel,
        out_shape=_sds((DEC_SEQ, DEC_BATCH, LW), F32),
        name="lru_sample",
    )(x_tm, buf_tm, h0, cw, cb, wr, br, wi, bi, lam)


def _merge_kernel(oa_ref, ob_ref, oc_ref, w_ref, ga_ref, gb_ref, gc_ref, o_ref):
    acc = _sigmoid(ga_ref[...]) * _dot(oa_ref[...], w_ref[0])
    acc = acc + _sigmoid(gb_ref[...]) * _dot(ob_ref[...], w_ref[1])
    acc = acc + _sigmoid(gc_ref[...]) * _dot(oc_ref[...], w_ref[2])
    o_ref[...] = acc.astype(o_ref.dtype)


def _merge(o_a, o_b, o_c, wb, p_act, tm):
    m_rows = o_a.shape[0]
    tn = 512
    g0 = P_G // tn
    gs = D // tn
    br = pl.BlockSpec((tm, 1024), lambda n, m: (m, 0))
    return pl.pallas_call(
        _merge_kernel,
        grid=(D // tn, m_rows // tm),
        in_specs=[
            br, br, br,
            pl.BlockSpec((3, 1024, tn), lambda n, m: (0, 0, n)),
            pl.BlockSpec((tm, tn), lambda n, m: (m, g0 + n)),
            pl.BlockSpec((tm, tn), lambda n, m: (m, g0 + gs + n)),
            pl.BlockSpec((tm, tn), lambda n, m: (m, g0 + 2 * gs + n)),
        ],
        out_specs=pl.BlockSpec((tm, tn), lambda n, m: (m, n)),
        out_shape=_sds((m_rows, D), BF16),
        compiler_params=_cparams(2, VMEM_BIG),
        name="merge",
    )(o_a, o_b, o_c, wb, p_act, p_act, p_act)


def _outproj_kernel(a_ref, w_ref, x_ref, gt_ref, o_ref):
    o_ref[...] = x_ref[...] + gt_ref[0, 0] * _dot(a_ref[...], w_ref[...])


def _outproj(merged, w_out, x, mod, tm):
    m_rows = x.shape[0]
    tn = 1024
    tiles = m_rows // tm
    return pl.pallas_call(
        _outproj_kernel,
        grid=(D // tn, tiles),
        in_specs=[
            pl.BlockSpec((tm, D), lambda n, m: (m, 0)),
            pl.BlockSpec((D, tn), lambda n, m: (0, n)),
            pl.BlockSpec((tm, tn), lambda n, m: (m, n)),
            _mod_spec(mod, 2, tiles, tn, with_n=True),
        ],
        out_specs=pl.BlockSpec((tm, tn), lambda n, m: (m, n)),
        out_shape=_sds((m_rows, D), F32),
        compiler_params=_cparams(2, VMEM_BIG),
        name="outproj",
    )(merged, w_out, x, mod)


def _router_kernel(xp_ref, xs_ref, g_ref, shp_ref, scp_ref, shs_ref, scs_ref, wr_ref, br_ref,
                   h_ref, te_ref, tw_ref):
    tm = TOK_TILE
    is_s = pl.program_id(0) == N_TT - 1
    x = jnp.where(is_s, xs_ref[...], xp_ref[...])
    sc = jnp.where(is_s, scs_ref[0, 0], scp_ref[0, 0])
    sh = jnp.where(is_s, shs_ref[0, 0], shp_ref[0, 0])
    h = _rms(x, g_ref[...]) * (1.0 + sc) + sh
    h_ref[...] = h
    lane = lax.broadcasted_iota(I32, (tm, 128), 1)
    logits = _dot(h.astype(BF16), wr_ref[...]) + br_ref[...]
    logits = jnp.where(lane < NE, logits, -jnp.inf)
    lane4 = lax.broadcasted_iota(I32, (tm, TOPK), 1)
    te = jnp.zeros((tm, TOPK), I32)
    tl = jnp.zeros((tm, TOPK), F32)
    for k in range(TOPK):
        m = jnp.max(logits, axis=-1, keepdims=True)
        idx = jnp.min(jnp.where(logits == m, lane, 128), axis=-1, keepdims=True)
        te = jnp.where(lane4 == k, idx, te)
        tl = jnp.where(lane4 == k, m, tl)
        logits = jnp.where(lane == idx, -jnp.inf, logits)
    e = jnp.exp(tl - tl[:, 0:1])
    te_ref[...] = te
    tw_ref[...] = e / jnp.sum(e, axis=-1, keepdims=True)


def _router(x1p, x1s, g, mod_p, mod_s, wr, br):
    np_t = MP // TOK_TILE
    tpg = np_t // BATCH
    last_p = np_t - 1

    def mp(j):
        return pl.BlockSpec((1, 1, 1, D), lambda m: (j, jnp.minimum(m, last_p) // tpg, 0, 0))

    def ms(j):
        return pl.BlockSpec((1, 1, MS, D), lambda m: (j, 0, 0, 0))

    return pl.pallas_call(
        _router_kernel,
        grid=(N_TT,),
        in_specs=[
            pl.BlockSpec((TOK_TILE, D), lambda m: (jnp.minimum(m, last_p), 0)),
            pl.BlockSpec((MS, D), lambda m: (0, 0)),
            pl.BlockSpec((1, D), lambda m: (0, 0)),
            mp(3), mp(4), ms(3), ms(4),
            pl.BlockSpec((D, 128), lambda m: (0, 0)),
            pl.BlockSpec((1, 128), lambda m: (0, 0)),
        ],
        out_specs=[
            pl.BlockSpec((TOK_TILE, D), lambda m: (m, 0)),
            pl.BlockSpec((TOK_TILE, TOPK), lambda m: (m, 0)),
            pl.BlockSpec((TOK_TILE, TOPK), lambda m: (m, 0)),
        ],
        out_shape=[_sds((NTOK, D), F32), _sds((NTOK, TOPK), I32), _sds((NTOK, TOPK), F32)],
        compiler_params=_cparams(1),
        name="router",
    )(x1p, x1s, g, mod_p, mod_p, mod_s, mod_s, wr, br)


def _rank_kernel(te_ref, rk_ref, cnt_ref):
    @pl.when(pl.program_id(0) == 0)
    def _():
        cnt_ref[...] = jnp.zeros_like(cnt_ref)

    tt = TOK_TILE
    lane = lax.broadcasted_iota(I32, (tt, 128), 1)
    te = te_ref[...]
    hot = jnp.zeros((tt, 128), F32)
    for k in range(TOPK):
        hot = hot + (lane == te[:, k:k + 1]).astype(F32)
    r = lax.broadcasted_iota(I32, (tt, tt), 0)
    c = lax.broadcasted_iota(I32, (tt, tt), 1)
    before = (r > c).astype(BF16)
    tot = _dot(before, hot.astype(BF16)) + cnt_ref[...]
    lane4 = lax.broadcasted_iota(I32, (tt, TOPK), 1)
    rk = jnp.zeros((tt, TOPK), F32)
    for k in range(TOPK):
        rk_k = jnp.sum(jnp.where(lane == te[:, k:k + 1], tot, 0.0), axis=-1, keepdims=True)
        rk = jnp.where(lane4 == k, rk_k, rk)
    rk_ref[...] = rk.astype(I32)
    cnt_ref[...] = cnt_ref[...] + jnp.sum(hot, axis=0, keepdims=True)


def _rank(te):
    return pl.pallas_call(
        _rank_kernel,
        grid=(N_TT,),
        in_specs=[pl.BlockSpec((TOK_TILE, TOPK), lambda i: (i, 0))],
        out_specs=[pl.BlockSpec((TOK_TILE, TOPK), lambda i: (i, 0)),
                   pl.BlockSpec((1, 128), lambda i: (0, 0))],
        out_shape=[_sds((NTOK, TOPK), I32), _sds((1, 128), F32)],
        compiler_params=_cparams(1),
        name="rank",
    )(te)


def _scatter_kernel(dest_ref, last_ref, h_ref, xs_out, zbuf, zsem, sem):
    @pl.when(pl.program_id(0) == 0)
    def _():
        zbuf[...] = jnp.zeros_like(zbuf)

        def zero_block(row):
            return pltpu.make_async_copy(zbuf, xs_out.at[pl.ds(row, RB)], zsem)

        def extra_row(i):
            return pl.multiple_of(last_ref[NE] + i * RB, RB)

        n_extra = last_ref[NE + 1]
        for e in range(NE):
            zero_block(pl.multiple_of(last_ref[e], RB)).start()
        lax.fori_loop(0, n_extra, lambda i, c: (zero_block(extra_row(i)).start(), c)[1], 0)
        for e in range(NE):
            zero_block(pl.multiple_of(last_ref[e], RB)).wait()
        lax.fori_loop(0, n_extra, lambda i, c: (zero_block(extra_row(i)).wait(), c)[1], 0)

    def body(i, carry):
        for k in range(TOPK):
            d = dest_ref[0, 0, i * TOPK + k]
            pltpu.make_async_copy(h_ref.at[pl.ds(i, 1)], xs_out.at[pl.ds(d, 1)], sem).start(
                priority=k % 2)
        return carry

    lax.fori_loop(0, TOK_TILE, body, 0)
    for k in range(TOPK):
        pltpu.make_async_copy(h_ref, xs_out.at[pl.ds(0, TOK_TILE)], sem).wait()


def _scatter_rows(dest3, last_rows, h2):
    return pl.pallas_call(
        _scatter_kernel,
        grid=(N_TT,),
        in_specs=[
            pl.BlockSpec((1, 1, TOK_TILE * TOPK), lambda i: (i, 0, 0), memory_space=pltpu.SMEM),
            pl.BlockSpec(memory_space=pltpu.SMEM),
            pl.BlockSpec((TOK_TILE, D), lambda i: (i, 0)),
        ],
        out_specs=pl.BlockSpec(memory_space=pl.ANY),
        out_shape=_sds((ROWS, D), F32),
        scratch_shapes=[pltpu.VMEM((RB, D), F32), pltpu.SemaphoreType.DMA(()),
                        pltpu.SemaphoreType.DMA(())],
        compiler_params=_cparams(1),
        name="scatter_rows",
    )(dest3, last_rows, h2)


def _zero_unowned(zsrc, dst_at, used_ref, zsem):
    n = NB_MAX - used_ref[0]

    def cp(i):
        return pltpu.make_async_copy(zsrc, dst_at(used_ref[0] + i), zsem)

    lax.fori_loop(0, n, lambda i, c: (cp(i).start(), c)[1], 0)
    lax.fori_loop(0, n, lambda i, c: (cp(i).wait(), c)[1], 0)


def _row_of(blk):
    return pl.multiple_of(blk * RB, RB)


SB = 3
SROWS = SB * RB


def _stream_rows(n, first_blk, rows, x_src, o_dst, xbuf, obuf, xsem, osem, compute,
                 prologue=None):
    row0 = _row_of(first_blk)

    def x_copies(i, slot):
        r = pl.multiple_of(row0 + i * rows, RB)
        return [pltpu.make_async_copy(src, dst, xsem.at[slot])
                for src, dst in x_src(r, rows, xbuf.at[slot, pl.ds(0, rows)])]

    def o_copy(i, slot):
        r = pl.multiple_of(row0 + i * rows, RB)
        return pltpu.make_async_copy(obuf.at[slot, pl.ds(0, rows)], o_dst(r, rows), osem.at[slot])

    @pl.when(n > 0)
    def _():
        for cp in x_copies(0, 0):
            cp.start()
        if prologue is not None:
            prologue()

        def body(i, carry):
            slot = i % 2
            for cp in x_copies(i, slot):
                cp.wait()

            @pl.when(i + 1 < n)
            def _():
                for cp in x_copies(i + 1, 1 - slot):
                    cp.start()

            @pl.when(i >= 2)
            def _():
                o_copy(i - 2, slot).wait()

            compute(xbuf.at[slot, pl.ds(0, rows)], obuf.at[slot, pl.ds(0, rows)])
            o_copy(i, slot).start()
            return carry

        lax.fori_loop(0, n, body, 0)

        @pl.when(n >= 2)
        def _():
            o_copy(n - 2, n % 2).wait()

        o_copy(n - 1, (n - 1) % 2).wait()


def _gmm_up_kernel(bstart, nblk, used, x_hbm, wg_ref, wl_ref, bg_ref, bl_ref, act_hbm,
                   wg_s, wl_s, xbuf, obuf, xsem, osem, zsem):
    s = pl.program_id(0)
    e = s // NJ
    j = s % NJ
    nb = nblk[e]
    b0 = bstart[e]

    @pl.when(s == 0)
    def _():
        obuf[0] = jnp.zeros(obuf.shape[1:], obuf.dtype)
        for jj in range(NJ):
            _zero_unowned(obuf.at[0],
                          lambda blk, jj=jj: act_hbm.at[jj, pl.ds(_row_of(blk), RB)], used, zsem)

    def cast_weights():
        wg_s[...] = wg_ref[0, 0].astype(BF16)
        wl_s[...] = wl_ref[0, 0].astype(BF16)

    def compute(x_ref, o_ref):
        x = x_ref[...].astype(BF16)
        glu = jnp.minimum(_dot(x, wg_s[...]) + bg_ref[0, 0], SW_LIMIT)
        lin = jnp.clip(_dot(x, wl_s[...]) + bl_ref[0, 0], -SW_LIMIT, SW_LIMIT)
        o_ref[...] = (glu * _sigmoid(SW_ALPHA * glu) * (lin + 1.0)).astype(o_ref.dtype)

    def x_src(r, rows, dst):
        return [(x_hbm.at[pl.ds(r, rows)], dst)]

    def o_dst(r, rows):
        return act_hbm.at[j, pl.ds(r, rows)]

    _stream_rows(nb, b0, RB, x_src, o_dst, xbuf, obuf, xsem, osem, compute, cast_weights)


def _gmm_up(tables, xs, w_gu, b_gu, layer):
    grid_spec = pltpu.PrefetchScalarGridSpec(
        num_scalar_prefetch=3,
        grid=(NE * NJ,),
        in_specs=[
            pl.BlockSpec(memory_space=pl.ANY),
            pl.BlockSpec((1, 1, D, TN_E), lambda s, *_: (layer, s // NJ, 0, s % NJ)),
            pl.BlockSpec((1, 1, D, TN_E), lambda s, *_: (layer, s // NJ, 0, NJ + s % NJ)),
            pl.BlockSpec((1, 1, 1, TN_E), lambda s, *_: (layer, s // NJ, 0, s % NJ)),
            pl.BlockSpec((1, 1, 1, TN_E), lambda s, *_: (layer, s // NJ, 0, NJ + s % NJ)),
        ],
        out_specs=pl.BlockSpec(memory_space=pl.ANY),
        scratch_shapes=[pltpu.VMEM((D, TN_E), BF16), pltpu.VMEM((D, TN_E), BF16),
                        pltpu.VMEM((2, RB, D), F32), pltpu.VMEM((2, RB, TN_E), BF16),
                        pltpu.SemaphoreType.DMA((2,)), pltpu.SemaphoreType.DMA((2,)),
                        pltpu.SemaphoreType.DMA(())],
    )
    return pl.pallas_call(
        _gmm_up_kernel,
        grid_spec=grid_spec,
        out_shape=_sds((NJ, ROWS, TN_E), BF16),
        compiler_params=_cparams(1, VMEM_BIG),
        name="gmm_up",
    )(*tables, xs, w_gu, w_gu, b_gu, b_gu)


def _gmm_down_kernel(layer, bstart, nblk, used, x_hbm, w_hbm, b_ref, out_hbm,
                     wstage, w_s, xbuf, obuf, wsem, xsem, osem, zsem):
    e = pl.program_id(0)
    nb = nblk[e]
    b0 = bstart[e]

    def w_copy(step):
        return pltpu.make_async_copy(w_hbm.at[layer, step], wstage, wsem)

    @pl.when(e == 0)
    def _():
        w_copy(0).start()
        obuf[0] = jnp.zeros(obuf.shape[1:], obuf.dtype)
        _zero_unowned(obuf.at[0, pl.ds(0, RB)],
                      lambda blk: out_hbm.at[pl.ds(_row_of(blk), RB)], used, zsem)

    w_copy(e).wait()
    w_s[...] = wstage[...].astype(BF16)

    @pl.when(e + 1 < pl.num_programs(0))
    def _():
        w_copy(e + 1).start()

    def compute(x_ref, o_ref):
        acc = b_ref[0, 0] + _dot(x_ref[:, 0:TN_E], w_s[0:TN_E, :])
        for jj in range(1, NJ):
            acc = acc + _dot(x_ref[:, jj * TN_E:(jj + 1) * TN_E], w_s[jj * TN_E:(jj + 1) * TN_E, :])
        o_ref[...] = acc

    def x_src(r, rows, dst):
        return [(x_hbm.at[jj, pl.ds(r, rows)], dst.at[:, pl.ds(jj * TN_E, TN_E)])
                for jj in range(NJ)]

    def o_dst(r, rows):
        return out_hbm.at[pl.ds(r, rows)]

    n_big = nb // SB
    _stream_rows(n_big, b0, SROWS, x_src, o_dst, xbuf, obuf, xsem, osem, compute)
    _stream_rows(nb - n_big * SB, b0 + n_big * SB, RB, x_src, o_dst, xbuf, obuf, xsem, osem,
                 compute)


def _gmm_down(tables, act, w_down, b_down, layer):
    grid_spec = pltpu.PrefetchScalarGridSpec(
        num_scalar_prefetch=3,
        grid=(NE,),
        in_specs=[
            pl.BlockSpec(memory_space=pl.ANY),
            pl.BlockSpec(memory_space=pl.ANY),
            pl.BlockSpec((1, 1, 1, D), lambda e, *_: (layer, e, 0, 0)),
        ],
        out_specs=pl.BlockSpec(memory_space=pl.ANY),
        scratch_shapes=[pltpu.VMEM((DFF, D), F32), pltpu.VMEM((DFF, D), BF16),
                        pltpu.VMEM((2, SROWS, DFF), BF16), pltpu.VMEM((2, SROWS, D), F32),
                        pltpu.SemaphoreType.DMA(()), pltpu.SemaphoreType.DMA((2,)),
                        pltpu.SemaphoreType.DMA((2,)), pltpu.SemaphoreType.DMA(())],
    )
    return pl.pallas_call(
        functools.partial(_gmm_down_kernel, layer),
        grid_spec=grid_spec,
        out_shape=_sds((ROWS, D), F32),
        compiler_params=_cparams(1, VMEM_GMM),
        name="gmm_down",
    )(*tables, act, w_down, b_down)


def _combine_kernel(dest_ref, rows_hbm, tw_ref, x_ref, gt_ref, o_ref, buf, sem):
    def body(i, carry):
        for k in range(TOPK):
            d = dest_ref[0, 0, i * TOPK + k]
            pltpu.make_async_copy(rows_hbm.at[pl.ds(d, 1)], buf.at[k, pl.ds(i, 1)], sem).start(
                priority=k % 2)
        return carry

    lax.fori_loop(0, TOK_TILE, body, 0)
    for k in range(TOPK):
        pltpu.make_async_copy(rows_hbm.at[pl.ds(0, TOK_TILE)], buf.at[k], sem).wait()
    tw = tw_ref[...]
    y = tw[:, 0:1] * buf[0]
    for k in range(1, TOPK):
        y = y + tw[:, k:k + 1] * buf[k]
    o_ref[...] = x_ref[...] + gt_ref[0, 0] * y


def _combine(dest3, out_rows, tw, x1, mod, tile0):
    m_rows = x1.shape[0]
    tiles = m_rows // TOK_TILE
    return pl.pallas_call(
        _combine_kernel,
        grid=(tiles,),
        in_specs=[
            pl.BlockSpec((1, 1, TOK_TILE * TOPK), lambda m: (tile0 + m, 0, 0),
                         memory_space=pltpu.SMEM),
            pl.BlockSpec(memory_space=pl.ANY),
            pl.BlockSpec((TOK_TILE, TOPK), lambda m: (tile0 + m, 0)),
            pl.BlockSpec((TOK_TILE, D), lambda m: (m, 0)),
            _mod_spec(mod, 5, tiles),
        ],
        out_specs=pl.BlockSpec((TOK_TILE, D), lambda m: (m, 0)),
        out_shape=_sds((m_rows, D), F32),
        scratch_shapes=[pltpu.VMEM((TOPK, TOK_TILE, D), F32), pltpu.SemaphoreType.DMA(())],
        compiler_params=_cparams(1),
        name="combine",
    )(dest3, out_rows, tw, x1, mod)


def _routing_tables(te, rank, cnt):
    counts = cnt[0, :NE].astype(I32)
    nblk = (counts + RB - 1) // RB
    bstart = jnp.cumsum(nblk) - nblk
    used = jnp.sum(nblk)
    te_d = te.reshape(N_TT, TOK_TILE * TOPK)
    start_d = jnp.zeros_like(te_d)
    for e in range(NE):
        start_d = jnp.where(te_d == e, bstart[e] * RB, start_d)
    dest3 = (start_d + rank.reshape(N_TT, TOK_TILE * TOPK)).reshape(N_TT, 1, TOK_TILE * TOPK)
    empty = nblk == 0
    n_empty = jnp.sum(empty.astype(I32))
    last_blk = jnp.where(empty, used + jnp.cumsum(empty.astype(I32)) - 1, bstart + nblk - 1)
    extra0 = used + n_empty
    last_rows = jnp.concatenate([last_blk * RB, jnp.stack([extra0 * RB, NB_MAX - extra0])])
    tables = (bstart.astype(I32), nblk.astype(I32), used.reshape(1).astype(I32))
    return dest3, tables, last_rows.astype(I32)


def _moe(layer, h2_all, te_all, w_gu, b_gu, w_down, b_down):
    rank, cnt = _rank(te_all)
    dest3, steps, last_rows = _routing_tables(te_all, rank, cnt)
    xs = _scatter_rows(dest3, last_rows, h2_all)
    act = _gmm_up(steps, xs, w_gu, b_gu.reshape(DEPTH, NE, 1, 2 * DFF), layer)
    out_rows = _gmm_down(steps, act, w_down, b_down.reshape(DEPTH, NE, 1, D), layer)
    return dest3, out_rows


def _layer(l, xp, xs, st, w, mod_p, mod_s):
    cast = lambda a: a.astype(BF16)
    w_in = w["w_in"][l]
    w_main = cast(jnp.concatenate(
        [w_in[:, 1536:4608], w_in[:, 6672:12816], w_in[:, 4624:5648], w_in[:, 5648:6672],
         w_in[:, 0:1024], w_in[:, 1024:1280], w_in[:, 1280:1536]], axis=1))
    w_ab = cast(jnp.pad(w_in[:, 4608:4624], ((0, 0), (0, 112))))
    wb = cast(w["w_branch"][l])
    w_out = cast(w["w_out"][l])
    wr = cast(jnp.pad(w["w_router"][l], ((0, 0), (0, 128 - NE))))
    br = jnp.pad(w["b_router"][l], (0, 128 - NE)).reshape(1, 128)
    g1 = w["norm1_g"][l].reshape(1, D)
    g2 = w["norm2_g"][l].reshape(1, D)
    qg = w["q_norm_g"][l].reshape(1, HD)
    kg = w["k_norm_g"][l].reshape(1, HD)
    sinks = w["sinks"][l]
    dn_cw = w["dn_conv_w"][l]
    alog_row = jnp.pad(w["dn_a_log"][l], (0, 128 - DNH)).reshape(1, 128)
    dtb_row = jnp.pad(w["dn_dt_bias"][l], (0, 128 - DNH)).reshape(1, 128)
    dn_ng = w["dn_norm_g"][l].reshape(1, DK)
    l_cw = w["lru_conv_w"][l]
    l_cb = w["lru_conv_b"][l].reshape(1, LW)
    l_wr = cast(w["lru_w_r"][l])
    l_wi = cast(w["lru_w_i"][l])
    l_br = w["lru_b_r"][l].reshape(1, LW)
    l_bi = w["lru_b_i"][l].reshape(1, LW)
    l_lam = w["lru_lambda"][l].reshape(1, LW)

    h1p = _adaln(xp, g1, mod_p, 0, 1, 256)
    pp = _matmul(h1p, w_main, F32, 1024, 1280, "proj")
    pab = _matmul(h1p, w_ab, F32, 2048, 128, "proj_ab")
    oa_p, k_p, v_p = _attn_prompt(pp, qg, kg, sinks)
    ob_p, d_p, dc_p = _delta_prompt(pp, pab, dn_cw, alog_row, dtb_row, dn_ng)
    ob_p = ob_p.reshape(MP, 1024)
    oc_p, h_p, hc_p = _lru_prompt(pp, l_cw, l_cb, l_wr, l_br, l_wi, l_bi, l_lam)
    mg_p = _merge(oa_p, ob_p, oc_p, wb, pp, 1024)
    x1p = _outproj(mg_p, w_out, xp, mod_p, 1024)

    ck, cv, s0, dconv0, h0, lconv0 = st
    h1s = _adaln(xs, g1, mod_s, 0, 1, MS)
    ps = _matmul(h1s, w_main, F32, MS, 512, "proj")
    psab = _matmul(h1s, w_ab, F32, MS, 128, "proj_ab")
    q_r = ps[:, P_Q:P_K].reshape(DEC_BATCH, DEC_SEQ, NKV, GRP, HD).transpose(0, 2, 3, 1, 4)
    q_r = q_r.reshape(DEC_BATCH, NKV, GRP * DEC_SEQ, HD)
    pad_t = ((0, 0), (0, 0), (0, 8 - DEC_SEQ), (0, 0))
    kn_r = jnp.pad(ps[:, P_K:P_V].reshape(DEC_BATCH, DEC_SEQ, NKV, HD).transpose(0, 2, 1, 3), pad_t)
    vn_raw = ps[:, P_V:P_W].reshape(DEC_BATCH, DEC_SEQ, NKV, HD)
    vn_r = jnp.pad(vn_raw.transpose(0, 2, 1, 3), pad_t)
    oa_r, kno = _attn_sample(q_r, kn_r, vn_r, ck.transpose(0, 2, 1, 3), cv.transpose(0, 2, 1, 3),
                             qg, kg, sinks)
    oa_s = oa_r.reshape(DEC_BATCH, NKV, GRP, DEC_SEQ, HD).transpose(0, 3, 1, 2, 4)
    oa_s = oa_s.reshape(MS, 1024).astype(BF16)
    k_s = jnp.concatenate([ck[:, DEC_SEQ:], kno[:, :, :DEC_SEQ].transpose(0, 2, 1, 3)], axis=1)
    v_s = jnp.concatenate([cv[:, DEC_SEQ:], vn_raw], axis=1)

    x_dq = ps[:, P_DQKV:P_DQKV + CONV_CH].reshape(DEC_BATCH, DEC_SEQ, CONV_CH)
    xp8 = jnp.concatenate([jnp.zeros((DEC_BATCH, 1, CONV_CH), F32), dconv0, x_dq], axis=1)
    pad8 = ((0, 0), (0, 8 - DEC_SEQ), (0, 0))
    dz8 = jnp.pad(ps[:, P_DZ:P_LX].reshape(DEC_BATCH, DEC_SEQ, 1024), pad8)
    dab8 = jnp.pad(psab.reshape(DEC_BATCH, DEC_SEQ, 128), pad8)
    ob_r, d_s = _delta_sample(xp8, dz8, dab8, s0, dn_cw, alog_row, dtb_row, dn_ng)
    ob_s = ob_r[:, :DEC_SEQ].reshape(MS, 1024).astype(BF16)
    dc_s = x_dq[:, DEC_SEQ - (CONV_W - 1):]

    x_lx = ps[:, P_LX:P_Q].reshape(DEC_BATCH, DEC_SEQ, LW)
    oc_tm = _lru_sample(x_lx.transpose(1, 0, 2), lconv0.transpose(1, 0, 2), h0, l_cw, l_cb,
                        l_wr, l_br, l_wi, l_bi, l_lam)
    oc_s = oc_tm.transpose(1, 0, 2).reshape(MS, LW).astype(BF16)
    h_s = oc_tm[DEC_SEQ - 1]
    hc_s = x_lx[:, DEC_SEQ - (CONV_W - 1):]

    mg_s = _merge(oa_s, ob_s, oc_s, wb, ps, MS)
    x1s = _outproj(mg_s, w_out, xs, mod_s, MS)

    h2_all, te_all, tw_all = _router(x1p, x1s, g2, mod_p, mod_s, wr, br)
    dest3, out_rows = _moe(l, h2_all, te_all, w["w_gate_up"], w["b_gate_up"],
                           w["w_down"], w["b_down"])
    np_t = MP // TOK_TILE
    xp_new = _combine(dest3, out_rows, tw_all, x1p, mod_p, 0)
    xs_new = _combine(dest3, out_rows, tw_all, x1s, mod_s, np_t)

    st_p = (k_p.reshape(BATCH, WINDOW, NKV, HD), v_p.reshape(BATCH, WINDOW, NKV, HD), d_p, dc_p,
            h_p.reshape(BATCH, LW), hc_p)
    st_s = (k_s, v_s, d_s, dc_s, h_s, hc_s)
    return xp_new, xs_new, st_p, st_s


def kernel(x_prompt, x_sample, cache_k, cache_v, state_delta, state_delta_conv, state_lru, state_lru_conv, c_prompt, c_sample, w_ada, b_ada, norm1_g, norm2_g, w_in, q_norm_g, k_norm_g, sinks, dn_conv_w, dn_a_log, dn_dt_bias, dn_norm_g, lru_conv_w, lru_conv_b, lru_w_r, lru_b_r, lru_w_i, lru_b_i, lru_lambda, w_branch, w_out, w_router, b_router, w_gate_up, b_gate_up, w_down, b_down):
    w = dict(w_in=w_in, w_branch=w_branch, w_out=w_out, w_router=w_router, b_router=b_router,
             norm1_g=norm1_g, norm2_g=norm2_g, q_norm_g=q_norm_g, k_norm_g=k_norm_g, sinks=sinks,
             dn_conv_w=dn_conv_w, dn_a_log=dn_a_log, dn_dt_bias=dn_dt_bias, dn_norm_g=dn_norm_g,
             lru_conv_w=lru_conv_w, lru_conv_b=lru_conv_b, lru_w_r=lru_w_r, lru_b_r=lru_b_r,
             lru_w_i=lru_w_i, lru_b_i=lru_b_i, lru_lambda=lru_lambda, w_gate_up=w_gate_up,
             b_gate_up=b_gate_up, w_down=w_down, b_down=b_down)
    n_c = BATCH + DEC_BATCH
    c_all = jnp.concatenate([c_prompt, c_sample, jnp.zeros((40 - n_c, D), F32)], axis=0)
    mod_all = _ada_mod(c_all, w_ada, b_ada)

    xp = x_prompt.reshape(MP, D)
    xs = x_sample.reshape(MS, D)
    new_p, new_s = [], []
    for l in range(DEPTH):
        mod_p = mod_all[l, :BATCH].reshape(BATCH, 6, 1, D).transpose(1, 0, 2, 3)
        mod_s = jnp.repeat(mod_all[l, BATCH:n_c].reshape(DEC_BATCH, 6, D), DEC_SEQ, axis=0)
        mod_s = mod_s.transpose(1, 0, 2).reshape(6, 1, MS, D)
        st = (cache_k[l], cache_v[l], state_delta[l], state_delta_conv[l], state_lru[l],
              state_lru_conv[l])
        xp, xs, st_p, st_s = _layer(l, xp, xs, st, w, mod_p, mod_s)
        new_p.append(st_p)
        new_s.append(st_s)
    k_p, v_p, d_p, dc_p, h_p, hc_p = (jnp.stack(z) for z in zip(*new_p))
    k_s, v_s, d_s, dc_s, h_s, hc_s = (jnp.stack(z) for z in zip(*new_s))
    return (xp.reshape(BATCH, SEQ, D), xs.reshape(DEC_BATCH, DEC_SEQ, D),
            k_p, v_p, d_p, dc_p, h_p, hc_p, k_s, v_s, d_s, dc_s, h_s, hc_s)
```

```python
import functools

import jax
import jax.numpy as jnp
from jax import lax
from jax.experimental import pallas as pl
from jax.experimental.pallas import tpu as pltpu

F32 = jnp.float32
BF16 = jnp.bfloat16
I32 = jnp.int32

D = 2048
BATCH = 2
SEQ = 4096
DEC_BATCH = 32
DEC_SEQ = 4
DEPTH = 2
MP = BATCH * SEQ
MS = DEC_BATCH * DEC_SEQ
NTOK = MP + MS

WINDOW = 128
HD = 64
NH = 16
NKV = 4
GRP = NH // NKV
ATT_SCALE = HD ** -0.5

DK = 128
DNH = 8
CHUNK = 64
CONV_CH = 3 * DNH * DK
CONV_W = 4

LW = 1024
LBLK = 128
LNB = LW // LBLK
LRU_C = 8.0

NE = 32
TOPK = 4
DFF = 2048
SW_ALPHA = 1.702
SW_LIMIT = 7.0
EPS = 1e-6

P_DQKV = 0
P_G = 3072
P_DZ = 9216
P_LX = 10240
P_Q = 11264
P_K = 12288
P_V = 12544
P_W = 12800

RB = 256
NB_MAX = -(-(NTOK * TOPK) // RB) + NE
ROWS = NB_MAX * RB
TN_E = 1024
NJ = DFF // TN_E
TOK_TILE = 128
N_TT = NTOK // TOK_TILE

VMEM_BIG = 56 * 1024 * 1024
VMEM_GMM = 60 * 1024 * 1024


def _sds(shape, dtype):
    return jax.ShapeDtypeStruct(shape, dtype)


def _cparams(n_axes, vmem=None):
    return pltpu.CompilerParams(dimension_semantics=("arbitrary",) * n_axes, vmem_limit_bytes=vmem)


def _dot(a, b):
    return jnp.dot(a, b, preferred_element_type=F32)


def _dot_nt(a, b):
    return lax.dot_general(a, b, (((1,), (1,)), ((), ())), preferred_element_type=F32)


def _dot_tn(a, b):
    return lax.dot_general(a, b, (((0,), (0,)), ((), ())), preferred_element_type=F32)


def _rms(x, g):
    return x * lax.rsqrt(jnp.mean(x * x, axis=-1, keepdims=True) + EPS) * g


def _sigmoid(x):
    return 1.0 / (1.0 + jnp.exp(-x))


def _silu(x):
    return x * _sigmoid(x)


def _softplus(x):
    return jnp.maximum(x, 0.0) + jnp.log1p(jnp.exp(-jnp.abs(x)))


def _ada_kernel(c_ref, w_ref, b_ref, o_ref):
    a = _silu(c_ref[...]).astype(BF16)
    o_ref[0] = _dot(a, w_ref[0].astype(BF16)) + b_ref[0]


def _ada_mod(c_all, w_ada, b_ada):
    rows = c_all.shape[0]
    tn = 1024
    return pl.pallas_call(
        _ada_kernel,
        grid=(DEPTH, 6 * D // tn),
        in_specs=[
            pl.BlockSpec((rows, D), lambda l, n: (0, 0)),
            pl.BlockSpec((1, D, tn), lambda l, n: (l, 0, n)),
            pl.BlockSpec((1, 1, tn), lambda l, n: (l, 0, n)),
        ],
        out_specs=pl.BlockSpec((1, rows, tn), lambda l, n: (l, 0, n)),
        out_shape=_sds((DEPTH, rows, 6 * D), F32),
        compiler_params=_cparams(2, VMEM_BIG),
        name="ada_mod",
    )(c_all, w_ada, b_ada.reshape(DEPTH, 1, 6 * D))


def _adaln_kernel(x_ref, g_ref, sh_ref, sc_ref, o_ref):
    y = _rms(x_ref[...], g_ref[...])
    o_ref[...] = (y * (1.0 + sc_ref[0, 0]) + sh_ref[0, 0]).astype(o_ref.dtype)


def _mod_spec(mod, j, tiles, tn=D, with_n=False):
    g, rb = mod.shape[1], mod.shape[2]
    tpg = tiles // g
    if with_n:
        return pl.BlockSpec((1, 1, rb, tn), lambda n, m: (j, m // tpg, 0, n))
    return pl.BlockSpec((1, 1, rb, tn), lambda m: (j, m // tpg, 0, 0))


def _adaln(x, g, mod, j_sh, j_sc, tm):
    m_rows = x.shape[0]
    tiles = m_rows // tm
    return pl.pallas_call(
        _adaln_kernel,
        grid=(tiles,),
        in_specs=[
            pl.BlockSpec((tm, D), lambda m: (m, 0)),
            pl.BlockSpec((1, D), lambda m: (0, 0)),
            _mod_spec(mod, j_sh, tiles),
            _mod_spec(mod, j_sc, tiles),
        ],
        out_specs=pl.BlockSpec((tm, D), lambda m: (m, 0)),
        out_shape=_sds((m_rows, D), BF16),
        compiler_params=_cparams(1),
        name="adaln1",
    )(x, g, mod, mod)


def _mm_kernel(x_ref, w_ref, o_ref):
    o_ref[...] = _dot(x_ref[...], w_ref[...]).astype(o_ref.dtype)


def _matmul(x, w, out_dtype, tm, tn, name):
    m_rows, k = x.shape
    n_cols = w.shape[1]
    return pl.pallas_call(
        _mm_kernel,
        grid=(n_cols // tn, m_rows // tm),
        in_specs=[
            pl.BlockSpec((tm, k), lambda n, m: (m, 0)),
            pl.BlockSpec((k, tn), lambda n, m: (0, n)),
        ],
        out_specs=pl.BlockSpec((tm, tn), lambda n, m: (m, n)),
        out_shape=_sds((m_rows, n_cols), out_dtype),
        compiler_params=_cparams(2, VMEM_BIG),
        name=name,
    )(x, w)


def _attn_prompt_kernel(sinks_ref, q_ref, kc_ref, vc_ref, kp_ref, vp_ref, qg_ref, kg_ref,
                        o_ref, ko_ref, vo_ref):
    nblk = SEQ // WINDOW
    first = (pl.program_id(0) % nblk) == 0
    row = lax.broadcasted_iota(I32, (WINDOW, 2 * WINDOW), 0)
    col = lax.broadcasted_iota(I32, (WINDOW, 2 * WINDOW), 1)
    lo = jnp.where(first, WINDOW, 0)
    mask = (col > row) & (col <= row + WINDOW) & (col >= lo)
    qg = qg_ref[...]
    kg = kg_ref[...]
    outs = []
    k_out = []
    for kh in range(NKV):
        ks = slice(kh * HD, (kh + 1) * HD)
        kc_n = _rms(kc_ref[:, ks], kg)
        kp_n = _rms(kp_ref[:, ks], kg)
        k_out.append(kc_n)
        kk = jnp.concatenate([kp_n, kc_n], axis=0).astype(BF16)
        vv = jnp.concatenate([vp_ref[:, ks], vc_ref[:, ks]], axis=0).astype(BF16)
        for g in range(GRP):
            h = kh * GRP + g
            qh = _rms(q_ref[:, h * HD:(h + 1) * HD], qg).astype(BF16)
            s = _dot_nt(qh, kk) * ATT_SCALE
            s = jnp.where(mask, s, -jnp.inf)
            sink = sinks_ref[h]
            m = jnp.maximum(jnp.max(s, axis=-1, keepdims=True), sink)
            p = jnp.exp(s - m)
            den = jnp.sum(p, axis=-1, keepdims=True) + jnp.exp(sink - m)
            outs.append(_dot(p.astype(BF16), vv) / den)
    o_ref[...] = jnp.concatenate(outs, axis=-1).astype(o_ref.dtype)
    ko_ref[0] = jnp.concatenate(k_out, axis=-1)
    vo_ref[0] = vc_ref[...]


def _attn_prompt(p_act, qg, kg, sinks):
    nblk = SEQ // WINDOW
    cq, ck, cv = P_Q // 1024, P_K // 256, P_V // 256
    return pl.pallas_call(
        _attn_prompt_kernel,
        grid=(MP // WINDOW,),
        in_specs=[
            pl.BlockSpec(memory_space=pltpu.SMEM),
            pl.BlockSpec((WINDOW, 1024), lambda g: (g, cq)),
            pl.BlockSpec((WINDOW, 256), lambda g: (g, ck)),
            pl.BlockSpec((WINDOW, 256), lambda g: (g, cv)),
            pl.BlockSpec((WINDOW, 256), lambda g: (jnp.maximum(g - 1, 0), ck)),
            pl.BlockSpec((WINDOW, 256), lambda g: (jnp.maximum(g - 1, 0), cv)),
            pl.BlockSpec((1, HD), lambda g: (0, 0)),
            pl.BlockSpec((1, HD), lambda g: (0, 0)),
        ],
        out_specs=[
            pl.BlockSpec((WINDOW, 1024), lambda g: (g, 0)),
            pl.BlockSpec((1, WINDOW, 256), lambda g: (g // nblk, 0, 0)),
            pl.BlockSpec((1, WINDOW, 256), lambda g: (g // nblk, 0, 0)),
        ],
        out_shape=[_sds((MP, 1024), BF16), _sds((BATCH, WINDOW, 256), F32),
                   _sds((BATCH, WINDOW, 256), F32)],
        compiler_params=_cparams(1),
        name="attn_prompt",
    )(sinks, p_act, p_act, p_act, p_act, p_act, qg, kg)


def _attn_sample_kernel(sinks_ref, q_ref, kn_ref, vn_ref, ck_ref, cv_ref, qg_ref, kg_ref,
                        o_ref, kno_ref):
    rows = GRP * DEC_SEQ
    t = lax.broadcasted_iota(I32, (rows, 1), 0) % DEC_SEQ
    g_of_row = lax.broadcasted_iota(I32, (rows, 1), 0) // DEC_SEQ
    col_c = lax.broadcasted_iota(I32, (rows, WINDOW), 1)
    col_n = lax.broadcasted_iota(I32, (rows, 8), 1)
    qg = qg_ref[...]
    kg = kg_ref[...]
    for kh in range(NKV):
        q16 = _rms(q_ref[0, kh], qg).astype(BF16)
        kn = _rms(kn_ref[0, kh], kg)
        kno_ref[0, kh] = kn
        s_c = _dot_nt(q16, ck_ref[0, kh].astype(BF16)) * ATT_SCALE
        s_n = _dot_nt(q16, kn.astype(BF16)) * ATT_SCALE
        s_c = jnp.where(col_c > t, s_c, -jnp.inf)
        s_n = jnp.where(col_n <= t, s_n, -jnp.inf)
        sink = jnp.zeros((rows, 1), F32)
        for g in range(GRP):
            sink = jnp.where(g_of_row == g, sinks_ref[kh * GRP + g], sink)
        m = jnp.maximum(jnp.maximum(jnp.max(s_c, axis=-1, keepdims=True),
                                    jnp.max(s_n, axis=-1, keepdims=True)), sink)
        p_c = jnp.exp(s_c - m)
        p_n = jnp.exp(s_n - m)
        den = (jnp.sum(p_c, axis=-1, keepdims=True) + jnp.sum(p_n, axis=-1, keepdims=True)
               + jnp.exp(sink - m))
        o = _dot(p_c.astype(BF16), cv_ref[0, kh].astype(BF16)) + _dot(
            p_n.astype(BF16), vn_ref[0, kh].astype(BF16))
        o_ref[0, kh] = o / den


def _attn_sample(q_r, kn_r, vn_r, ck_r, cv_r, qg, kg, sinks):
    rows = GRP * DEC_SEQ
    return pl.pallas_call(
        _attn_sample_kernel,
        grid=(DEC_BATCH,),
        in_specs=[
            pl.BlockSpec(memory_space=pltpu.SMEM),
            pl.BlockSpec((1, NKV, rows, HD), lambda b: (b, 0, 0, 0)),
            pl.BlockSpec((1, NKV, 8, HD), lambda b: (b, 0, 0, 0)),
            pl.BlockSpec((1, NKV, 8, HD), lambda b: (b, 0, 0, 0)),
            pl.BlockSpec((1, NKV, WINDOW, HD), lambda b: (b, 0, 0, 0)),
            pl.BlockSpec((1, NKV, WINDOW, HD), lambda b: (b, 0, 0, 0)),
            pl.BlockSpec((1, HD), lambda b: (0, 0)),
            pl.BlockSpec((1, HD), lambda b: (0, 0)),
        ],
        out_specs=[
            pl.BlockSpec((1, NKV, rows, HD), lambda b: (b, 0, 0, 0)),
            pl.BlockSpec((1, NKV, 8, HD), lambda b: (b, 0, 0, 0)),
        ],
        out_shape=[_sds((DEC_BATCH, NKV, rows, HD), F32), _sds((DEC_BATCH, NKV, 8, HD), F32)],
        compiler_params=_cparams(1),
        name="attn_sample",
    )(sinks, q_r, kn_r, vn_r, ck_r, cv_r, qg, kg)


def _cumsum_rows(x, c):
    row = lax.broadcasted_iota(I32, x.shape, 0)
    s = 1
    while s < c:
        x = x + jnp.where(row >= s, pltpu.roll(x, s, 0), 0.0)
        s *= 2
    return x


def _delta_chunk(c, nstack, conv_slice, g_full, beta_full, z_slice, norm_g, s_get, s_put, o_put):
    n = nstack * c
    lg = c.bit_length() - 1
    r = lax.broadcasted_iota(I32, (n, n), 0)
    cc = lax.broadcasted_iota(I32, (n, n), 1)
    same = (r >> lg) == (cc >> lg)
    incl = same & (r >= cc)
    strict = same & (r > cc)
    eye_b = r == cc
    eye = eye_b.astype(F32)
    gc = _cumsum_rows(g_full, c)
    for st in range(DNH // nstack):
        heads = range(st * nstack, (st + 1) * nstack)
        qs, ks, vs, gcs, bs = [], [], [], [], []
        for h in heads:
            qh = conv_slice(slice(h * DK, (h + 1) * DK))
            kh = conv_slice(slice(DNH * DK + h * DK, DNH * DK + (h + 1) * DK))
            vs.append(conv_slice(slice(2 * DNH * DK + h * DK, 2 * DNH * DK + (h + 1) * DK)))
            qs.append(qh * lax.rsqrt(jnp.sum(qh * qh, axis=-1, keepdims=True) + EPS) * (DK ** -0.5))
            ks.append(kh * lax.rsqrt(jnp.sum(kh * kh, axis=-1, keepdims=True) + EPS))
            gcs.append(gc[:, h:h + 1])
            bs.append(beta_full[:, DNH + h:DNH + h + 1])
        q = jnp.concatenate(qs, axis=0)
        k = jnp.concatenate(ks, axis=0)
        v = jnp.concatenate(vs, axis=0)
        gcol = jnp.concatenate(gcs, axis=0)
        bcol = jnp.concatenate(bs, axis=0)
        grow = jnp.sum(jnp.where(eye_b, gcol, 0.0), axis=0, keepdims=True)
        decay = jnp.exp(jnp.where(incl, gcol - grow, -jnp.inf))
        egc = jnp.exp(gcol)
        kb = k * bcol
        k16 = k.astype(BF16)
        a_mat = jnp.where(strict, _dot_nt(kb.astype(BF16), k16) * decay, 0.0)
        blk = 1
        t_inv = eye
        while blk < c:
            sh = blk.bit_length()
            pair = ((r >> sh) == (cc >> sh)) & ((r & blk) != 0) & ((cc & blk) == 0)
            off = jnp.where(pair, a_mat, 0.0)
            if blk == 1:
                t_inv = t_inv - off
            else:
                t16 = t_inv.astype(BF16)
                t_inv = t_inv - _dot(t16, _dot(off.astype(BF16), t16).astype(BF16))
            blk *= 2
        rhs = jnp.concatenate([v * bcol, kb * egc], axis=-1).astype(BF16)
        sol = _dot(t_inv.astype(BF16), rhs)
        qk16 = (_dot_nt(q.astype(BF16), k16) * decay).astype(BF16)
        q_dec = q * egc
        us, s_olds, g_lasts = [], [], []
        for i, h in enumerate(heads):
            hs = slice(i * c, (i + 1) * c)
            s_old = s_get(h)
            us.append(sol[hs, :DK] - _dot(sol[hs, DK:].astype(BF16), s_old.astype(BF16)))
            s_olds.append(s_old)
            g_lasts.append(gcol[(i + 1) * c - 1:(i + 1) * c, :])
        u16 = jnp.concatenate(us, axis=0).astype(BF16)
        o_in = _dot(qk16, u16)
        for i, h in enumerate(heads):
            hs = slice(i * c, (i + 1) * c)
            s16 = s_olds[i].astype(BF16)
            o = _dot(q_dec[hs].astype(BF16), s16) + o_in[hs]
            k_dec = k[hs] * jnp.exp(g_lasts[i] - gcol[hs])
            s_put(h, s_olds[i] * jnp.exp(g_lasts[i]) + _dot_tn(k_dec.astype(BF16), u16[hs]))
            zz = z_slice(slice(h * DK, (h + 1) * DK))
            o_put(h, _rms(o, norm_g) * _silu(zz))


def _delta_prompt_kernel(x_ref, prev_ref, dz_ref, dab_ref, cw_ref, alog_ref, dtb_ref, ng_ref,
                         o_ref, s_ref, dc_ref):
    c = pl.program_id(0)

    @pl.when(c == 0)
    def _():
        s_ref[...] = jnp.zeros_like(s_ref)

    row8 = lax.broadcasted_iota(I32, (8, DK), 0)
    keep_prev = c > 0
    for b in range(BATCH):
        def conv_slice(cs, b=b):
            x = x_ref[b, :, cs]
            prev = jnp.where(keep_prev, prev_ref[b, :, cs], 0.0)
            y = x * cw_ref[CONV_W - 1:CONV_W, cs]
            for s in range(1, CONV_W):
                xr = pltpu.roll(x, s, 0)
                top = jnp.where(row8 < s, pltpu.roll(prev, s, 0), xr[0:8])
                sh = jnp.concatenate([top, xr[8:]], axis=0)
                y = y + sh * cw_ref[CONV_W - 1 - s:CONV_W - s, cs]
            return _silu(y)

        dab = dab_ref[b]
        g_full = -jnp.exp(alog_ref[...]) * _softplus(dab + dtb_ref[...])
        beta_full = _sigmoid(dab)

        def s_get(h, b=b):
            return s_ref[b, h]

        def s_put(h, v, b=b):
            s_ref[b, h] = v

        def o_put(h, v, b=b):
            o_ref[b, :, h * DK:(h + 1) * DK] = v.astype(o_ref.dtype)

        _delta_chunk(CHUNK, 4, conv_slice, g_full, beta_full, lambda cs, b=b: dz_ref[b, :, cs],
                     ng_ref[...], s_get, s_put, o_put)

    @pl.when(c == pl.num_programs(0) - 1)
    def _():
        dc_ref[...] = x_ref[:, CHUNK - (CONV_W - 1):CHUNK, :]


def _delta_prompt(p_act, p_ab, conv_w, alog_row, dtb_row, norm_g):
    nck = SEQ // CHUNK
    p3 = p_act.reshape(BATCH, SEQ, p_act.shape[1])
    ab3 = p_ab.reshape(BATCH, SEQ, 128)
    return pl.pallas_call(
        _delta_prompt_kernel,
        grid=(nck,),
        in_specs=[
            pl.BlockSpec((BATCH, CHUNK, CONV_CH), lambda c: (0, c, P_DQKV // CONV_CH)),
            pl.BlockSpec((BATCH, 8, CONV_CH),
                         lambda c: (0, jnp.maximum(c * (CHUNK // 8) - 1, 0), P_DQKV // CONV_CH)),
            pl.BlockSpec((BATCH, CHUNK, 1024), lambda c: (0, c, P_DZ // 1024)),
            pl.BlockSpec((BATCH, CHUNK, 128), lambda c: (0, c, 0)),
            pl.BlockSpec((CONV_W, CONV_CH), lambda c: (0, 0)),
            pl.BlockSpec((1, 128), lambda c: (0, 0)),
            pl.BlockSpec((1, 128), lambda c: (0, 0)),
            pl.BlockSpec((1, DK), lambda c: (0, 0)),
        ],
        out_specs=[
            pl.BlockSpec((BATCH, CHUNK, 1024), lambda c: (0, c, 0)),
            pl.BlockSpec((BATCH, DNH, DK, DK), lambda c: (0, 0, 0, 0)),
            pl.BlockSpec((BATCH, CONV_W - 1, CONV_CH), lambda c: (0, 0, 0)),
        ],
        out_shape=[_sds((BATCH, SEQ, 1024), BF16), _sds((BATCH, DNH, DK, DK), F32),
                   _sds((BATCH, CONV_W - 1, CONV_CH), F32)],
        compiler_params=_cparams(1),
        name="delta_prompt",
    )(p3, p3, p3, ab3, conv_w, alog_row, dtb_row, norm_g)


def _delta_sample_kernel(xp_ref, dz_ref, dab_ref, s0_ref, cw_ref, alog_ref, dtb_ref, ng_ref,
                         o_ref, s_ref):
    row = lax.broadcasted_iota(I32, (8, DK), 0)
    live = row < DEC_SEQ

    def conv_slice(cs):
        xp = xp_ref[0, :, cs]
        y = jnp.zeros((8, DK), F32)
        for j in range(CONV_W):
            y = y + pltpu.roll(xp, 8 - 1 - j, 0) * cw_ref[j:j + 1, cs]
        return jnp.where(live, _silu(y), 0.0)

    dab = dab_ref[0]
    g_full = jnp.where(live, -jnp.exp(alog_ref[...]) * _softplus(dab + dtb_ref[...]), 0.0)
    beta_full = jnp.where(live, _sigmoid(dab), 0.0)

    def s_put(h, v):
        s_ref[0, h] = v

    def o_put(h, v):
        o_ref[0, :, h * DK:(h + 1) * DK] = v

    _delta_chunk(8, DNH, conv_slice, g_full, beta_full, lambda cs: dz_ref[0, :, cs], ng_ref[...],
                 lambda h: s0_ref[0, h], s_put, o_put)


def _delta_sample(xp8, dz8, dab8, s0, conv_w, alog_row, dtb_row, norm_g):
    return pl.pallas_call(
        _delta_sample_kernel,
        grid=(DEC_BATCH,),
        in_specs=[
            pl.BlockSpec((1, 8, CONV_CH), lambda b: (b, 0, 0)),
            pl.BlockSpec((1, 8, 1024), lambda b: (b, 0, 0)),
            pl.BlockSpec((1, 8, 128), lambda b: (b, 0, 0)),
            pl.BlockSpec((1, DNH, DK, DK), lambda b: (b, 0, 0, 0)),
            pl.BlockSpec((CONV_W, CONV_CH), lambda b: (0, 0)),
            pl.BlockSpec((1, 128), lambda b: (0, 0)),
            pl.BlockSpec((1, 128), lambda b: (0, 0)),
            pl.BlockSpec((1, DK), lambda b: (0, 0)),
        ],
        out_specs=[
            pl.BlockSpec((1, 8, 1024), lambda b: (b, 0, 0)),
            pl.BlockSpec((1, DNH, DK, DK), lambda b: (b, 0, 0, 0)),
        ],
        out_shape=[_sds((DEC_BATCH, 8, 1024), F32), _sds((DEC_BATCH, DNH, DK, DK), F32)],
        compiler_params=_cparams(1),
        name="delta_sample",
    )(xp8, dz8, dab8, s0, conv_w, alog_row, dtb_row, norm_g)


def _lru_gates(xc, wr, br, wi, bi, sp):
    x16 = xc.astype(BF16)
    r = _sigmoid(_dot(x16, wr) + br)
    i = _sigmoid(_dot(x16, wi) + bi)
    log_a = -LRU_C * r * sp
    a = jnp.exp(log_a)
    th = jnp.tanh(log_a)
    u = jnp.sqrt(-2.0 * th / (1.0 - th)) * (i * xc)
    return a, u


LRU_TT = 256


def _lru_prompt_kernel(x_ref, prev_ref, cw_ref, cb_ref, wr_ref, br_ref, wi_ref, bi_ref, lam_ref,
                       o_ref, h_ref, lc_ref):
    t = pl.program_id(1)
    row8 = lax.broadcasted_iota(I32, (8, LBLK), 0)
    row = lax.broadcasted_iota(I32, (LRU_TT, LBLK), 0)
    keep = t > 0
    for n in range(LNB):
        cs = slice(n * LBLK, (n + 1) * LBLK)
        x = x_ref[:, cs]
        prev = jnp.where(keep, prev_ref[:, cs], 0.0)
        y = x * cw_ref[CONV_W - 1:CONV_W, cs]
        for s in range(1, CONV_W):
            xr = pltpu.roll(x, s, 0)
            top = jnp.where(row8 < s, pltpu.roll(prev, s, 0), xr[0:8])
            y = y + jnp.concatenate([top, xr[8:]], axis=0) * cw_ref[CONV_W - 1 - s:CONV_W - s, cs]
        xc = y + cb_ref[:, cs]
        sp = _softplus(-lam_ref[:, cs])
        a, u = _lru_gates(xc, wr_ref[n], br_ref[:, cs], wi_ref[n], bi_ref[:, cs], sp)
        h0 = jnp.where(keep, h_ref[0, :, cs], 0.0)
        u = u + jnp.where(row == 0, a * h0, 0.0)
        s = 1
        while s < LRU_TT:
            valid = row >= s
            u_s = pltpu.roll(u, s, 0)
            a_s = pltpu.roll(a, s, 0)
            u = jnp.where(valid, a * u_s + u, u)
            a = jnp.where(valid, a * a_s, a)
            s *= 2
        o_ref[:, cs] = u.astype(o_ref.dtype)
        h_ref[0, :, cs] = u[LRU_TT - 1:LRU_TT, :]

    @pl.when(t == pl.num_programs(1) - 1)
    def _():
        lc_ref[0] = x_ref[LRU_TT - (CONV_W - 1):LRU_TT, :]


def _lru_prompt(p_act, cw, cb, wr, br, wi, bi, lam):
    ntt = SEQ // LRU_TT
    cl = P_LX // LW
    vec = pl.BlockSpec((1, LW), lambda b, t: (0, 0))
    mat = pl.BlockSpec((LNB, LBLK, LBLK), lambda b, t: (0, 0, 0))
    return pl.pallas_call(
        _lru_prompt_kernel,
        grid=(BATCH, ntt),
        in_specs=[
            pl.BlockSpec((LRU_TT, LW), lambda b, t: (b * ntt + t, cl)),
            pl.BlockSpec((8, LW),
                         lambda b, t: (jnp.maximum(b * (SEQ // 8) + t * (LRU_TT // 8) - 1, 0), cl)),
            pl.BlockSpec((CONV_W, LW), lambda b, t: (0, 0)),
            vec, mat, vec, mat, vec, vec,
        ],
        out_specs=[
            pl.BlockSpec((LRU_TT, LW), lambda b, t: (b * ntt + t, 0)),
            pl.BlockSpec((1, 1, LW), lambda b, t: (b, 0, 0)),
            pl.BlockSpec((1, CONV_W - 1, LW), lambda b, t: (b, 0, 0)),
        ],
        out_shape=[_sds((MP, LW), BF16), _sds((BATCH, 1, LW), F32),
                   _sds((BATCH, CONV_W - 1, LW), F32)],
        compiler_params=_cparams(2),
        name="lru_prompt",
    )(p_act, p_act, cw, cb, wr, br, wi, bi, lam)


def _lru_sample_kernel(x_ref, buf_ref, h0_ref, cw_ref, cb_ref, wr_ref, br_ref, wi_ref, bi_ref,
                       lam_ref, o_ref):
    for n in range(LNB):
        cs = slice(n * LBLK, (n + 1) * LBLK)
        xp = [buf_ref[j, :, cs] for j in range(CONV_W - 1)] + [x_ref[t, :, cs] for t in range(DEC_SEQ)]
        xcs = []
        for t in range(DEC_SEQ):
            y = xp[t] * cw_ref[0:1, cs]
            for j in range(1, CONV_W):
                y = y + xp[t + j] * cw_ref[j:j + 1, cs]
            xcs.append(y + cb_ref[:, cs])
        xc = jnp.concatenate(xcs, axis=0)
        sp = _softplus(-lam_ref[:, cs])
        a, u = _lru_gates(xc, wr_ref[n], br_ref[:, cs], wi_ref[n], bi_ref[:, cs], sp)
        h = h0_ref[:, cs]
        for t in range(DEC_SEQ):
            rs = slice(t * DEC_BATCH, (t + 1) * DEC_BATCH)
            h = a[rs] * h + u[rs]
            o_ref[t, :, cs] = h


def _lru_sample(x_tm, buf_tm, h0, cw, cb, wr, br, wi, bi, lam):
    return pl.pallas_call(
        _lru_sample_kernel,
        out_shape=_sds((DEC_SEQ, DEC_BATCH, LW), F32),
        name="lru_sample",
    )(x_tm, buf_tm, h0, cw, cb, wr, br, wi, bi, lam)


def _merge_kernel(oa_ref, ob_ref, oc_ref, w_ref, ga_ref, gb_ref, gc_ref, o_ref):
    acc = _sigmoid(ga_ref[...]) * _dot(oa_ref[...], w_ref[0])
    acc = acc + _sigmoid(gb_ref[...]) * _dot(ob_ref[...], w_ref[1])
    acc = acc + _sigmoid(gc_ref[...]) * _dot(oc_ref[...], w_ref[2])
    o_ref[...] = acc.astype(o_ref.dtype)


def _merge(o_a, o_b, o_c, wb, p_act, tm):
    m_rows = o_a.shape[0]
    tn = 1024
    g0 = P_G // tn
    gs = D // tn
    br = pl.BlockSpec((tm, 1024), lambda n, m: (m, 0))
    return pl.pallas_call(
        _merge_kernel,
        grid=(D // tn, m_rows // tm),
        in_specs=[
            br, br, br,
            pl.BlockSpec((3, 1024, tn), lambda n, m: (0, 0, n)),
            pl.BlockSpec((tm, tn), lambda n, m: (m, g0 + n)),
            pl.BlockSpec((tm, tn), lambda n, m: (m, g0 + gs + n)),
            pl.BlockSpec((tm, tn), lambda n, m: (m, g0 + 2 * gs + n)),
        ],
        out_specs=pl.BlockSpec((tm, tn), lambda n, m: (m, n)),
        out_shape=_sds((m_rows, D), BF16),
        compiler_params=_cparams(2, VMEM_BIG),
        name="merge",
    )(o_a, o_b, o_c, wb, p_act, p_act, p_act)


def _outproj_kernel(a_ref, w_ref, x_ref, gt_ref, o_ref):
    o_ref[...] = x_ref[...] + gt_ref[0, 0] * _dot(a_ref[...], w_ref[...])


def _outproj(merged, w_out, x, mod, tm):
    m_rows = x.shape[0]
    tn = D
    tiles = m_rows // tm
    return pl.pallas_call(
        _outproj_kernel,
        grid=(D // tn, tiles),
        in_specs=[
            pl.BlockSpec((tm, D), lambda n, m: (m, 0)),
            pl.BlockSpec((D, tn), lambda n, m: (0, n)),
            pl.BlockSpec((tm, tn), lambda n, m: (m, n)),
            _mod_spec(mod, 2, tiles, tn, with_n=True),
        ],
        out_specs=pl.BlockSpec((tm, tn), lambda n, m: (m, n)),
        out_shape=_sds((m_rows, D), F32),
        compiler_params=_cparams(2, VMEM_BIG),
        name="outproj",
    )(merged, w_out, x, mod)


def _router_kernel(xp_ref, xs_ref, g_ref, shp_ref, scp_ref, shs_ref, scs_ref, wr_ref, br_ref,
                   h_ref, te_ref, tw_ref):
    tm = TOK_TILE
    is_s = pl.program_id(0) == N_TT - 1
    x = jnp.where(is_s, xs_ref[...], xp_ref[...])
    sc = jnp.where(is_s, scs_ref[0, 0], scp_ref[0, 0])
    sh = jnp.where(is_s, shs_ref[0, 0], shp_ref[0, 0])
    h = _rms(x, g_ref[...]) * (1.0 + sc) + sh
    h_ref[...] = h
    lane = lax.broadcasted_iota(I32, (tm, 128), 1)
    logits = _dot(h.astype(BF16), wr_ref[...]) + br_ref[...]
    logits = jnp.where(lane < NE, logits, -jnp.inf)
    lane4 = lax.broadcasted_iota(I32, (tm, TOPK), 1)
    te = jnp.zeros((tm, TOPK), I32)
    tl = jnp.zeros((tm, TOPK), F32)
    for k in range(TOPK):
        m = jnp.max(logits, axis=-1, keepdims=True)
        idx = jnp.min(jnp.where(logits == m, lane, 128), axis=-1, keepdims=True)
        te = jnp.where(lane4 == k, idx, te)
        tl = jnp.where(lane4 == k, m, tl)
        logits = jnp.where(lane == idx, -jnp.inf, logits)
    e = jnp.exp(tl - tl[:, 0:1])
    te_ref[...] = te
    tw_ref[...] = e / jnp.sum(e, axis=-1, keepdims=True)


def _router(x1p, x1s, g, mod_p, mod_s, wr, br):
    np_t = MP // TOK_TILE
    tpg = np_t // BATCH
    last_p = np_t - 1

    def mp(j):
        return pl.BlockSpec((1, 1, 1, D), lambda m: (j, jnp.minimum(m, last_p) // tpg, 0, 0))

    def ms(j):
        return pl.BlockSpec((1, 1, MS, D), lambda m: (j, 0, 0, 0))

    return pl.pallas_call(
        _router_kernel,
        grid=(N_TT,),
        in_specs=[
            pl.BlockSpec((TOK_TILE, D), lambda m: (jnp.minimum(m, last_p), 0)),
            pl.BlockSpec((MS, D), lambda m: (0, 0)),
            pl.BlockSpec((1, D), lambda m: (0, 0)),
            mp(3), mp(4), ms(3), ms(4),
            pl.BlockSpec((D, 128), lambda m: (0, 0)),
            pl.BlockSpec((1, 128), lambda m: (0, 0)),
        ],
        out_specs=[
            pl.BlockSpec((TOK_TILE, D), lambda m: (m, 0)),
            pl.BlockSpec((TOK_TILE, TOPK), lambda m: (m, 0)),
            pl.BlockSpec((TOK_TILE, TOPK), lambda m: (m, 0)),
        ],
        out_shape=[_sds((NTOK, D), F32), _sds((NTOK, TOPK), I32), _sds((NTOK, TOPK), F32)],
        compiler_params=_cparams(1),
        name="router",
    )(x1p, x1s, g, mod_p, mod_p, mod_s, mod_s, wr, br)


def _rank_kernel(te_ref, rk_ref, cnt_ref):
    @pl.when(pl.program_id(0) == 0)
    def _():
        cnt_ref[...] = jnp.zeros_like(cnt_ref)

    tt = TOK_TILE
    lane = lax.broadcasted_iota(I32, (tt, 128), 1)
    te = te_ref[...]
    hot = jnp.zeros((tt, 128), F32)
    for k in range(TOPK):
        hot = hot + (lane == te[:, k:k + 1]).astype(F32)
    r = lax.broadcasted_iota(I32, (tt, tt), 0)
    c = lax.broadcasted_iota(I32, (tt, tt), 1)
    before = (r > c).astype(BF16)
    tot = _dot(before, hot.astype(BF16)) + cnt_ref[...]
    lane4 = lax.broadcasted_iota(I32, (tt, TOPK), 1)
    rk = jnp.zeros((tt, TOPK), F32)
    for k in range(TOPK):
        rk_k = jnp.sum(jnp.where(lane == te[:, k:k + 1], tot, 0.0), axis=-1, keepdims=True)
        rk = jnp.where(lane4 == k, rk_k, rk)
    rk_ref[...] = rk.astype(I32)
    cnt_ref[...] = cnt_ref[...] + jnp.sum(hot, axis=0, keepdims=True)


def _rank(te):
    return pl.pallas_call(
        _rank_kernel,
        grid=(N_TT,),
        in_specs=[pl.BlockSpec((TOK_TILE, TOPK), lambda i: (i, 0))],
        out_specs=[pl.BlockSpec((TOK_TILE, TOPK), lambda i: (i, 0)),
                   pl.BlockSpec((1, 128), lambda i: (0, 0))],
        out_shape=[_sds((NTOK, TOPK), I32), _sds((1, 128), F32)],
        compiler_params=_cparams(1),
        name="rank",
    )(te)


def _scatter_kernel(dest_ref, last_ref, h_ref, xs_out, zbuf, zsem, sem):
    @pl.when(pl.program_id(0) == 0)
    def _():
        zbuf[...] = jnp.zeros_like(zbuf)

        def zero_block(row):
            return pltpu.make_async_copy(zbuf, xs_out.at[pl.ds(row, RB)], zsem)

        def extra_row(i):
            return pl.multiple_of(last_ref[NE] + i * RB, RB)

        n_extra = last_ref[NE + 1]
        for e in range(NE):
            zero_block(pl.multiple_of(last_ref[e], RB)).start()
        lax.fori_loop(0, n_extra, lambda i, c: (zero_block(extra_row(i)).start(), c)[1], 0)
        for e in range(NE):
            zero_block(pl.multiple_of(last_ref[e], RB)).wait()
        lax.fori_loop(0, n_extra, lambda i, c: (zero_block(extra_row(i)).wait(), c)[1], 0)

    def body(i, carry):
        for k in range(TOPK):
            d = dest_ref[0, 0, i * TOPK + k]
            pltpu.make_async_copy(h_ref.at[pl.ds(i, 1)], xs_out.at[pl.ds(d, 1)], sem).start(
                priority=k % 2)
        return carry

    lax.fori_loop(0, TOK_TILE, body, 0)
    for k in range(TOPK):
        pltpu.make_async_copy(h_ref, xs_out.at[pl.ds(0, TOK_TILE)], sem).wait()


def _scatter_rows(dest3, last_rows, h2):
    return pl.pallas_call(
        _scatter_kernel,
        grid=(N_TT,),
        in_specs=[
            pl.BlockSpec((1, 1, TOK_TILE * TOPK), lambda i: (i, 0, 0), memory_space=pltpu.SMEM),
            pl.BlockSpec(memory_space=pltpu.SMEM),
            pl.BlockSpec((TOK_TILE, D), lambda i: (i, 0)),
        ],
        out_specs=pl.BlockSpec(memory_space=pl.ANY),
        out_shape=_sds((ROWS, D), F32),
        scratch_shapes=[pltpu.VMEM((RB, D), F32), pltpu.SemaphoreType.DMA(()),
                        pltpu.SemaphoreType.DMA(())],
        compiler_params=_cparams(1),
        name="scatter_rows",
    )(dest3, last_rows, h2)


def _zero_unowned(zsrc, dst_at, used_ref, zsem):
    n = NB_MAX - used_ref[0]

    def cp(i):
        return pltpu.make_async_copy(zsrc, dst_at(used_ref[0] + i), zsem)

    lax.fori_loop(0, n, lambda i, c: (cp(i).start(), c)[1], 0)
    lax.fori_loop(0, n, lambda i, c: (cp(i).wait(), c)[1], 0)


def _row_of(blk):
    return pl.multiple_of(blk * RB, RB)


SB = 3
SROWS = SB * RB


def _stream_rows(n, first_blk, rows, x_src, o_dst, xbuf, obuf, xsem, osem, compute,
                 prologue=None):
    row0 = _row_of(first_blk)

    def x_copies(i, slot):
        r = pl.multiple_of(row0 + i * rows, RB)
        return [pltpu.make_async_copy(src, dst, xsem.at[slot])
                for src, dst in x_src(r, rows, xbuf.at[slot, pl.ds(0, rows)])]

    def o_copy(i, slot):
        r = pl.multiple_of(row0 + i * rows, RB)
        return pltpu.make_async_copy(obuf.at[slot, pl.ds(0, rows)], o_dst(r, rows), osem.at[slot])

    @pl.when(n > 0)
    def _():
        for cp in x_copies(0, 0):
            cp.start()
        if prologue is not None:
            prologue()

        def body(i, carry):
            slot = i % 2
            for cp in x_copies(i, slot):
                cp.wait()

            @pl.when(i + 1 < n)
            def _():
                for cp in x_copies(i + 1, 1 - slot):
                    cp.start()

            @pl.when(i >= 2)
            def _():
                o_copy(i - 2, slot).wait()

            compute(xbuf.at[slot, pl.ds(0, rows)], obuf.at[slot, pl.ds(0, rows)])
            o_copy(i, slot).start()
            return carry

        lax.fori_loop(0, n, body, 0)

        @pl.when(n >= 2)
        def _():
            o_copy(n - 2, n % 2).wait()

        o_copy(n - 1, (n - 1) % 2).wait()


def _gmm_up_kernel(bstart, nblk, used, x_hbm, wg_ref, wl_ref, bg_ref, bl_ref, act_hbm,
                   wg_s, wl_s, xbuf, obuf, xsem, osem, zsem):
    s = pl.program_id(0)
    e = s // NJ
    j = s % NJ
    nb = nblk[e]
    b0 = bstart[e]

    @pl.when(s == 0)
    def _():
        obuf[0] = jnp.zeros(obuf.shape[1:], obuf.dtype)
        for jj in range(NJ):
            _zero_unowned(obuf.at[0],
                          lambda blk, jj=jj: act_hbm.at[jj, pl.ds(_row_of(blk), RB)], used, zsem)

    def cast_weights():
        wg_s[...] = wg_ref[0, 0].astype(BF16)
        wl_s[...] = wl_ref[0, 0].astype(BF16)

    def compute(x_ref, o_ref):
        x = x_ref[...].astype(BF16)
        glu = jnp.minimum(_dot(x, wg_s[...]) + bg_ref[0, 0], SW_LIMIT)
        lin = jnp.clip(_dot(x, wl_s[...]) + bl_ref[0, 0], -SW_LIMIT, SW_LIMIT)
        o_ref[...] = (glu * _sigmoid(SW_ALPHA * glu) * (lin + 1.0)).astype(o_ref.dtype)

    def x_src(r, rows, dst):
        return [(x_hbm.at[pl.ds(r, rows)], dst)]

    def o_dst(r, rows):
        return act_hbm.at[j, pl.ds(r, rows)]

    _stream_rows(nb, b0, RB, x_src, o_dst, xbuf, obuf, xsem, osem, compute, cast_weights)


def _gmm_up(tables, xs, w_gu, b_gu, layer):
    grid_spec = pltpu.PrefetchScalarGridSpec(
        num_scalar_prefetch=3,
        grid=(NE * NJ,),
        in_specs=[
            pl.BlockSpec(memory_space=pl.ANY),
            pl.BlockSpec((1, 1, D, TN_E), lambda s, *_: (layer, s // NJ, 0, s % NJ)),
            pl.BlockSpec((1, 1, D, TN_E), lambda s, *_: (layer, s // NJ, 0, NJ + s % NJ)),
            pl.BlockSpec((1, 1, 1, TN_E), lambda s, *_: (layer, s // NJ, 0, s % NJ)),
            pl.BlockSpec((1, 1, 1, TN_E), lambda s, *_: (layer, s // NJ, 0, NJ + s % NJ)),
        ],
        out_specs=pl.BlockSpec(memory_space=pl.ANY),
        scratch_shapes=[pltpu.VMEM((D, TN_E), BF16), pltpu.VMEM((D, TN_E), BF16),
                        pltpu.VMEM((2, RB, D), F32), pltpu.VMEM((2, RB, TN_E), BF16),
                        pltpu.SemaphoreType.DMA((2,)), pltpu.SemaphoreType.DMA((2,)),
                        pltpu.SemaphoreType.DMA(())],
    )
    return pl.pallas_call(
        _gmm_up_kernel,
        grid_spec=grid_spec,
        out_shape=_sds((NJ, ROWS, TN_E), BF16),
        compiler_params=_cparams(1, VMEM_BIG),
        name="gmm_up",
    )(*tables, xs, w_gu, w_gu, b_gu, b_gu)


def _gmm_down_kernel(layer, bstart, nblk, used, x_hbm, w_hbm, b_ref, out_hbm,
                     wstage, w_s, xbuf, obuf, wsem, xsem, osem, zsem):
    e = pl.program_id(0)
    nb = nblk[e]
    b0 = bstart[e]

    def w_copy(step):
        return pltpu.make_async_copy(w_hbm.at[layer, step], wstage, wsem)

    @pl.when(e == 0)
    def _():
        w_copy(0).start()
        obuf[0] = jnp.zeros(obuf.shape[1:], obuf.dtype)
        _zero_unowned(obuf.at[0, pl.ds(0, RB)],
                      lambda blk: out_hbm.at[pl.ds(_row_of(blk), RB)], used, zsem)

    w_copy(e).wait()
    w_s[...] = wstage[...].astype(BF16)

    @pl.when(e + 1 < pl.num_programs(0))
    def _():
        w_copy(e + 1).start()

    def compute(x_ref, o_ref):
        acc = b_ref[0, 0] + _dot(x_ref[:, 0:TN_E], w_s[0:TN_E, :])
        for jj in range(1, NJ):
            acc = acc + _dot(x_ref[:, jj * TN_E:(jj + 1) * TN_E], w_s[jj * TN_E:(jj + 1) * TN_E, :])
        o_ref[...] = acc

    def x_src(r, rows, dst):
        return [(x_hbm.at[jj, pl.ds(r, rows)], dst.at[:, pl.ds(jj * TN_E, TN_E)])
                for jj in range(NJ)]

    def o_dst(r, rows):
        return out_hbm.at[pl.ds(r, rows)]

    n_big = nb // SB
    _stream_rows(n_big, b0, SROWS, x_src, o_dst, xbuf, obuf, xsem, osem, compute)
    _stream_rows(nb - n_big * SB, b0 + n_big * SB, RB, x_src, o_dst, xbuf, obuf, xsem, osem,
                 compute)


def _gmm_down(tables, act, w_down, b_down, layer):
    grid_spec = pltpu.PrefetchScalarGridSpec(
        num_scalar_prefetch=3,
        grid=(NE,),
        in_specs=[
            pl.BlockSpec(memory_space=pl.ANY),
            pl.BlockSpec(memory_space=pl.ANY),
            pl.BlockSpec((1, 1, 1, D), lambda e, *_: (layer, e, 0, 0)),
        ],
        out_specs=pl.BlockSpec(memory_space=pl.ANY),
        scratch_shapes=[pltpu.VMEM((DFF, D), F32), pltpu.VMEM((DFF, D), BF16),
                        pltpu.VMEM((2, SROWS, DFF), BF16), pltpu.VMEM((2, SROWS, D), F32),
                        pltpu.SemaphoreType.DMA(()), pltpu.SemaphoreType.DMA((2,)),
                        pltpu.SemaphoreType.DMA((2,)), pltpu.SemaphoreType.DMA(())],
    )
    return pl.pallas_call(
        functools.partial(_gmm_down_kernel, layer),
        grid_spec=grid_spec,
        out_shape=_sds((ROWS, D), F32),
        compiler_params=_cparams(1, VMEM_GMM),
        name="gmm_down",
    )(*tables, act, w_down, b_down)


def _combine_kernel(dest_ref, rows_hbm, tw_ref, x_ref, gt_ref, o_ref, buf, sem):
    def body(i, carry):
        for k in range(TOPK):
            d = dest_ref[0, 0, i * TOPK + k]
            pltpu.make_async_copy(rows_hbm.at[pl.ds(d, 1)], buf.at[k, pl.ds(i, 1)], sem).start(
                priority=k % 2)
        return carry

    lax.fori_loop(0, TOK_TILE, body, 0)
    for k in range(TOPK):
        pltpu.make_async_copy(rows_hbm.at[pl.ds(0, TOK_TILE)], buf.at[k], sem).wait()
    tw = tw_ref[...]
    y = tw[:, 0:1] * buf[0]
    for k in range(1, TOPK):
        y = y + tw[:, k:k + 1] * buf[k]
    o_ref[...] = x_ref[...] + gt_ref[0, 0] * y


def _combine(dest3, out_rows, tw, x1, mod, tile0):
    m_rows = x1.shape[0]
    tiles = m_rows // TOK_TILE
    return pl.pallas_call(
        _combine_kernel,
        grid=(tiles,),
        in_specs=[
            pl.BlockSpec((1, 1, TOK_TILE * TOPK), lambda m: (tile0 + m, 0, 0),
                         memory_space=pltpu.SMEM),
            pl.BlockSpec(memory_space=pl.ANY),
            pl.BlockSpec((TOK_TILE, TOPK), lambda m: (tile0 + m, 0)),
            pl.BlockSpec((TOK_TILE, D), lambda m: (m, 0)),
            _mod_spec(mod, 5, tiles),
        ],
        out_specs=pl.BlockSpec((TOK_TILE, D), lambda m: (m, 0)),
        out_shape=_sds((m_rows, D), F32),
        scratch_shapes=[pltpu.VMEM((TOPK, TOK_TILE, D), F32), pltpu.SemaphoreType.DMA(())],
        compiler_params=_cparams(1),
        name="combine",
    )(dest3, out_rows, tw, x1, mod)


def _routing_tables(te, rank, cnt):
    counts = cnt[0, :NE].astype(I32)
    nblk = (counts + RB - 1) // RB
    bstart = jnp.cumsum(nblk) - nblk
    used = jnp.sum(nblk)
    te_d = te.reshape(N_TT, TOK_TILE * TOPK)
    start_d = jnp.zeros_like(te_d)
    for e in range(NE):
        start_d = jnp.where(te_d == e, bstart[e] * RB, start_d)
    dest3 = (start_d + rank.reshape(N_TT, TOK_TILE * TOPK)).reshape(N_TT, 1, TOK_TILE * TOPK)
    empty = nblk == 0
    n_empty = jnp.sum(empty.astype(I32))
    last_blk = jnp.where(empty, used + jnp.cumsum(empty.astype(I32)) - 1, bstart + nblk - 1)
    extra0 = used + n_empty
    last_rows = jnp.concatenate([last_blk * RB, jnp.stack([extra0 * RB, NB_MAX - extra0])])
    tables = (bstart.astype(I32), nblk.astype(I32), used.reshape(1).astype(I32))
    return dest3, tables, last_rows.astype(I32)


def _moe(layer, h2_all, te_all, w_gu, b_gu, w_down, b_down):
    rank, cnt = _rank(te_all)
    dest3, steps, last_rows = _routing_tables(te_all, rank, cnt)
    xs = _scatter_rows(dest3, last_rows, h2_all)
    act = _gmm_up(steps, xs, w_gu, b_gu.reshape(DEPTH, NE, 1, 2 * DFF), layer)
    out_rows = _gmm_down(steps, act, w_down, b_down.reshape(DEPTH, NE, 1, D), layer)
    return dest3, out_rows


def _layer(l, xp, xs, st, w, mod_p, mod_s):
    cast = lambda a: a.astype(BF16)
    w_in = w["w_in"][l]
    w_main = cast(jnp.concatenate(
        [w_in[:, 1536:4608], w_in[:, 6672:12816], w_in[:, 4624:5648], w_in[:, 5648:6672],
         w_in[:, 0:1024], w_in[:, 1024:1280], w_in[:, 1280:1536]], axis=1))
    w_ab = cast(jnp.pad(w_in[:, 4608:4624], ((0, 0), (0, 112))))
    wb = cast(w["w_branch"][l])
    w_out = cast(w["w_out"][l])
    wr = cast(jnp.pad(w["w_router"][l], ((0, 0), (0, 128 - NE))))
    br = jnp.pad(w["b_router"][l], (0, 128 - NE)).reshape(1, 128)
    g1 = w["norm1_g"][l].reshape(1, D)
    g2 = w["norm2_g"][l].reshape(1, D)
    qg = w["q_norm_g"][l].reshape(1, HD)
    kg = w["k_norm_g"][l].reshape(1, HD)
    sinks = w["sinks"][l]
    dn_cw = w["dn_conv_w"][l]
    alog_row = jnp.pad(w["dn_a_log"][l], (0, 128 - DNH)).reshape(1, 128)
    dtb_row = jnp.pad(w["dn_dt_bias"][l], (0, 128 - DNH)).reshape(1, 128)
    dn_ng = w["dn_norm_g"][l].reshape(1, DK)
    l_cw = w["lru_conv_w"][l]
    l_cb = w["lru_conv_b"][l].reshape(1, LW)
    l_wr = cast(w["lru_w_r"][l])
    l_wi = cast(w["lru_w_i"][l])
    l_br = w["lru_b_r"][l].reshape(1, LW)
    l_bi = w["lru_b_i"][l].reshape(1, LW)
    l_lam = w["lru_lambda"][l].reshape(1, LW)

    h1p = _adaln(xp, g1, mod_p, 0, 1, 256)
    pp = _matmul(h1p, w_main, F32, 1024, 1280, "proj")
    pab = _matmul(h1p, w_ab, F32, 2048, 128, "proj_ab")
    oa_p, k_p, v_p = _attn_prompt(pp, qg, kg, sinks)
    ob_p, d_p, dc_p = _delta_prompt(pp, pab, dn_cw, alog_row, dtb_row, dn_ng)
    ob_p = ob_p.reshape(MP, 1024)
    oc_p, h_p, hc_p = _lru_prompt(pp, l_cw, l_cb, l_wr, l_br, l_wi, l_bi, l_lam)
    mg_p = _merge(oa_p, ob_p, oc_p, wb, pp, 512)
    x1p = _outproj(mg_p, w_out, xp, mod_p, 512)

    ck, cv, s0, dconv0, h0, lconv0 = st
    h1s = _adaln(xs, g1, mod_s, 0, 1, MS)
    ps = _matmul(h1s, w_main, F32, MS, 512, "proj")
    psab = _matmul(h1s, w_ab, F32, MS, 128, "proj_ab")
    q_r = ps[:, P_Q:P_K].reshape(DEC_BATCH, DEC_SEQ, NKV, GRP, HD).transpose(0, 2, 3, 1, 4)
    q_r = q_r.reshape(DEC_BATCH, NKV, GRP * DEC_SEQ, HD)
    pad_t = ((0, 0), (0, 0), (0, 8 - DEC_SEQ), (0, 0))
    kn_r = jnp.pad(ps[:, P_K:P_V].reshape(DEC_BATCH, DEC_SEQ, NKV, HD).transpose(0, 2, 1, 3), pad_t)
    vn_raw = ps[:, P_V:P_W].reshape(DEC_BATCH, DEC_SEQ, NKV, HD)
    vn_r = jnp.pad(vn_raw.transpose(0, 2, 1, 3), pad_t)
    oa_r, kno = _attn_sample(q_r, kn_r, vn_r, ck.transpose(0, 2, 1, 3), cv.transpose(0, 2, 1, 3),
                             qg, kg, sinks)
    oa_s = oa_r.reshape(DEC_BATCH, NKV, GRP, DEC_SEQ, HD).transpose(0, 3, 1, 2, 4)
    oa_s = oa_s.reshape(MS, 1024).astype(BF16)
    k_s = jnp.concatenate([ck[:, DEC_SEQ:], kno[:, :, :DEC_SEQ].transpose(0, 2, 1, 3)], axis=1)
    v_s = jnp.concatenate([cv[:, DEC_SEQ:], vn_raw], axis=1)

    x_dq = ps[:, P_DQKV:P_DQKV + CONV_CH].reshape(DEC_BATCH, DEC_SEQ, CONV_CH)
    xp8 = jnp.concatenate([jnp.zeros((DEC_BATCH, 1, CONV_CH), F32), dconv0, x_dq], axis=1)
    pad8 = ((0, 0), (0, 8 - DEC_SEQ), (0, 0))
    dz8 = jnp.pad(ps[:, P_DZ:P_LX].reshape(DEC_BATCH, DEC_SEQ, 1024), pad8)
    dab8 = jnp.pad(psab.reshape(DEC_BATCH, DEC_SEQ, 128), pad8)
    ob_r, d_s = _delta_sample(xp8, dz8, dab8, s0, dn_cw, alog_row, dtb_row, dn_ng)
    ob_s = ob_r[:, :DEC_SEQ].reshape(MS, 1024).astype(BF16)
    dc_s = x_dq[:, DEC_SEQ - (CONV_W - 1):]

    x_lx = ps[:, P_LX:P_Q].reshape(DEC_BATCH, DEC_SEQ, LW)
    oc_tm = _lru_sample(x_lx.transpose(1, 0, 2), lconv0.transpose(1, 0, 2), h0, l_cw, l_cb,
                        l_wr, l_br, l_wi, l_bi, l_lam)
    oc_s = oc_tm.transpose(1, 0, 2).reshape(MS, LW).astype(BF16)
    h_s = oc_tm[DEC_SEQ - 1]
    hc_s = x_lx[:, DEC_SEQ - (CONV_W - 1):]

    mg_s = _merge(oa_s, ob_s, oc_s, wb, ps, MS)
    x1s = _outproj(mg_s, w_out, xs, mod_s, MS)

    h2_all, te_all, tw_all = _router(x1p, x1s, g2, mod_p, mod_s, wr, br)
    dest3, out_rows = _moe(l, h2_all, te_all, w["w_gate_up"], w["b_gate_up"],
                           w["w_down"], w["b_down"])
    np_t = MP // TOK_TILE
    xp_new = _combine(dest3, out_rows, tw_all, x1p, mod_p, 0)
    xs_new = _combine(dest3, out_rows, tw_all, x1s, mod_s, np_t)

    st_p = (k_p.reshape(BATCH, WINDOW, NKV, HD), v_p.reshape(BATCH, WINDOW, NKV, HD), d_p, dc_p,
            h_p.reshape(BATCH, LW), hc_p)
    st_s = (k_s, v_s, d_s, dc_s, h_s, hc_s)
    return xp_new, xs_new, st_p, st_s


def kernel(x_prompt, x_sample, cache_k, cache_v, state_delta, state_delta_conv, state_lru, state_lru_conv, c_prompt, c_sample, w_ada, b_ada, norm1_g, norm2_g, w_in, q_norm_g, k_norm_g, sinks, dn_conv_w, dn_a_log, dn_dt_bias, dn_norm_g, lru_conv_w, lru_conv_b, lru_w_r, lru_b_r, lru_w_i, lru_b_i, lru_lambda, w_branch, w_out, w_router, b_router, w_gate_up, b_gate_up, w_down, b_down):
    w = dict(w_in=w_in, w_branch=w_branch, w_out=w_out, w_router=w_router, b_router=b_router,
             norm1_g=norm1_g, norm2_g=norm2_g, q_norm_g=q_norm_g, k_norm_g=k_norm_g, sinks=sinks,
             dn_conv_w=dn_conv_w, dn_a_log=dn_a_log, dn_dt_bias=dn_dt_bias, dn_norm_g=dn_norm_g,
             lru_conv_w=lru_conv_w, lru_conv_b=lru_conv_b, lru_w_r=lru_w_r, lru_b_r=lru_b_r,
             lru_w_i=lru_w_i, lru_b_i=lru_b_i, lru_lambda=lru_lambda, w_gate_up=w_gate_up,
             b_gate_up=b_gate_up, w_down=w_down, b_down=b_down)
    n_c = BATCH + DEC_BATCH
    c_all = jnp.concatenate([c_prompt, c_sample, jnp.zeros((40 - n_c, D), F32)], axis=0)
    mod_all = _ada_mod(c_all, w_ada, b_ada)

    xp = x_prompt.reshape(MP, D)
    xs = x_sample.reshape(MS, D)
    new_p, new_s = [], []
    for l in range(DEPTH):
        mod_p = mod_all[l, :BATCH].reshape(BATCH, 6, 1, D).transpose(1, 0, 2, 3)
        mod_s = jnp.repeat(mod_all[l, BATCH:n_c].reshape(DEC_BATCH, 6, D), DEC_SEQ, axis=0)
        mod_s = mod_s.transpose(1, 0, 2).reshape(6, 1, MS, D)
        st = (cache_k[l], cache_v[l], state_delta[l], state_delta_conv[l], state_lru[l],
              state_lru_conv[l])
        xp, xs, st_p, st_s = _layer(l, xp, xs, st, w, mod_p, mod_s)
        new_p.append(st_p)
        new_s.append(st_s)
    k_p, v_p, d_p, dc_p, h_p, hc_p = (jnp.stack(z) for z in zip(*new_p))
    k_s, v_s, d_s, dc_s, h_s, hc_s = (jnp.stack(z) for z in zip(*new_s))
    return (xp.reshape(BATCH, SEQ, D), xs.reshape(DEC_BATCH, DEC_SEQ, D),
            k_p, v_p, d_p, dc_p, h_p, hc_p, k_s, v_s, d_s, dc_s, h_s, hc_s)
```
